```python
import jax, jax.numpy as jnp
from jax import lax
import numpy as np

D_MODEL = 1024
BATCH = 32
SEQ = 256
DEPTH = 4
DEC_BATCH = 2
DEC_SEQ = 2048
PAST_LEN = 512

GRID_W = 64
D_FF = 2816
F_GROUPS = 4
F_GC = 128
F_W = F_GROUPS * F_GC
MLA_HEADS = 8
MLA_Q_LORA = 384
MLA_KV_LORA = 256
MLA_NOPE = 64
MLA_ROPE = 32
MLA_V = 64
NA_HEADS = 8
NA_HEAD_DIM = 64
NA_KH = 8
NA_KW = 16
NA_W = NA_HEADS * NA_HEAD_DIM
IN_SPLITS = (F_W, F_W + MLA_Q_LORA, F_W + MLA_Q_LORA + MLA_KV_LORA, F_W + MLA_Q_LORA + MLA_KV_LORA + MLA_ROPE)
IN_W = F_W + MLA_Q_LORA + MLA_KV_LORA + MLA_ROPE + 3 * NA_W
ROPE_BASE = 10000.0
AXIS_DIM = MLA_ROPE // 2
Q_BLOCK = 128
ALPHA = (2.0 * DEPTH) ** 0.25
BETA = (8.0 * DEPTH) ** -0.25
MLA_SCALE = (MLA_NOPE + MLA_ROPE) ** -0.5
NA_SCALE = NA_HEAD_DIM ** -0.5
NEG_INF = -1e30
F32 = jnp.float32

kernel_name = 'hybrid_dit_fourier_mla_natten_step'


def _ln_plain(x, eps=1e-6):
    xf = x.astype(F32)
    mu = jnp.mean(xf, axis=-1, keepdims=True)
    var = jnp.mean(jnp.square(xf - mu), axis=-1, keepdims=True)
    return (xf - mu) * lax.rsqrt(var + eps)


def _layernorm(x, g, b):
    return (_ln_plain(x, 1e-5) * g.astype(F32) + b.astype(F32)).astype(x.dtype)


def _rmsnorm(x, g, eps=1e-6):
    xf = x.astype(F32)
    y = xf * lax.rsqrt(jnp.mean(jnp.square(xf), axis=-1, keepdims=True) + eps)
    return (y * g.astype(F32)).astype(x.dtype)


def _modulate(x, shift, scale):
    return (_ln_plain(x) * (1.0 + scale.astype(F32)) + shift.astype(F32)).astype(x.dtype)


def _adaln(cvec, w_ada, b_ada):
    return jnp.split(jax.nn.silu(cvec) @ w_ada + b_ada, 9, axis=-1)


def _swiglu(h, w1, w3, w2):
    return (jax.nn.silu(h @ w1) * (h @ w3)) @ w2


def _ffn_sublayer(x, shift, scale, gate, w1, w3, w2, g, b):
    y = _swiglu(_modulate(x, shift, scale), w1, w3, w2)
    return _layernorm(ALPHA * x + 0.5 * gate * y, g, b)


def _axial_rope_tables(n):
    t = jnp.arange(n, dtype=jnp.int32)
    pos = jnp.stack([t // GRID_W, t % GRID_W], axis=-1).astype(F32)
    half = AXIS_DIM // 2
    inv_freq = ROPE_BASE ** (-jnp.arange(half, dtype=F32) / half)
    ang = pos[:, :, None] * inv_freq
    ang = jnp.concatenate([ang, ang], axis=-1)
    return jnp.cos(ang), jnp.sin(ang)


def _apply_axial_rope(x, cos, sin):
    B, n, h, d = x.shape
    xr = x.reshape(B, n, h, 2, AXIS_DIM)
    x1 = xr[..., :AXIS_DIM // 2]
    x2 = xr[..., AXIS_DIM // 2:]
    rot = jnp.concatenate([-x2, x1], axis=-1)
    cs = cos[:, None].astype(x.dtype)
    sn = sin[:, None].astype(x.dtype)
    return (xr * cs + rot * sn).reshape(B, n, h, d)


def _block_attention(q, k, v, scale):
    B, Lq, H, dk = q.shape
    dv = v.shape[-1]
    nb = Lq // Q_BLOCK
    qb = jnp.moveaxis(q.reshape(B, nb, Q_BLOCK, H, dk), 1, 0)

    def one_block(qblk):
        s = jnp.einsum('bqhd,bkhd->bhqk', qblk, k).astype(F32) * scale
        p = jax.nn.softmax(s, axis=-1).astype(v.dtype)
        return jnp.einsum('bhqk,bkhd->bqhd', p, v)

    o = lax.map(one_block, qb)
    return jnp.moveaxis(o, 0, 1).reshape(B, Lq, H * dv)


def _neighbourhood_attention(q, k, v, k_ctx, v_ctx, rpb):
    B, N, H, dh = q.shape
    rows = N // GRID_W
    kh = min(NA_KH, rows)
    r = jnp.arange(rows, dtype=jnp.int32)
    col = jnp.arange(GRID_W, dtype=jnp.int32)
    row_start = jnp.clip(r - kh // 2, 0, rows - kh)
    row_idx = row_start[:, None] + jnp.arange(kh, dtype=jnp.int32)[None, :]
    col_start = jnp.clip(col - NA_KW // 2, 0, GRID_W - NA_KW)
    col_ok = (col[None, :] >= col_start[:, None]) & (col[None, :] < col_start[:, None] + NA_KW)
    qg = q.reshape(B, rows, GRID_W, H, dh)
    kg = k.reshape(B, rows, GRID_W, H, dh)[:, row_idx]
    vg = v.reshape(B, rows, GRID_W, H, dh)[:, row_idx]
    s_loc = jnp.einsum('brqhd,brkwhd->bhrqkw', qg, kg).astype(F32) * NA_SCALE
    dr = row_idx - r[:, None] + (NA_KH - 1)
    dc = jnp.clip(col[None, :] - col[:, None], -(NA_KW - 1), NA_KW - 1) + (NA_KW - 1)
    bias = rpb[:, dr[:, None, :, None], dc[None, :, None, :]]
    s_loc = s_loc + bias[None].astype(F32)
    s_loc = jnp.where(col_ok[None, None, None, :, None, :], s_loc, NEG_INF)
    s_ctx = jnp.einsum('brqhd,bchd->bhrqc', qg, k_ctx).astype(F32) * NA_SCALE
    n_loc = kh * GRID_W
    s = jnp.concatenate([s_loc.reshape(B, H, rows, GRID_W, n_loc), s_ctx], axis=-1)
    p = jax.nn.softmax(s, axis=-1).astype(v.dtype)
    p_loc = p[..., :n_loc].reshape(B, H, rows, GRID_W, kh, GRID_W)
    p_ctx = p[..., n_loc:]
    o = (jnp.einsum('bhrqkw,brkwhd->brqhd', p_loc, vg)
         + jnp.einsum('bhrqc,bchd->brqhd', p_ctx, v_ctx))
    return o.reshape(B, N, H * dh)


def _fourier(u_f):
    B, L, _ = u_f.shape
    f = u_f.reshape(B, L, F_GROUPS, F_GC).astype(F32)
    spec = jnp.fft.fft2(f, axes=(1, 3), norm='ortho').real
    return spec.reshape(B, L, F_W).astype(u_f.dtype)


def _mla_q(u_q, g, w_uq):
    B, L, _ = u_q.shape
    q = (_rmsnorm(u_q, g) @ w_uq).reshape(B, L, MLA_HEADS, MLA_NOPE + MLA_ROPE)
    return q[..., :MLA_NOPE], q[..., MLA_NOPE:]


def _mla_kv_up(c_kv, w_ukv):
    B, L, _ = c_kv.shape
    kv = (c_kv @ w_ukv).reshape(B, L, MLA_HEADS, MLA_NOPE + MLA_V)
    return kv[..., :MLA_NOPE], kv[..., MLA_NOPE:]


def _mla_keys(k_nope, k_rope):
    kr = jnp.broadcast_to(k_rope[:, :, None, :], k_nope.shape[:3] + (MLA_ROPE,))
    return jnp.concatenate([k_nope, kr], axis=-1)


def _na_qkv(u_na):
    B, L, _ = u_na.shape
    q, k, v = jnp.split(u_na, 3, axis=-1)
    return (q.reshape(B, L, NA_HEADS, NA_HEAD_DIM), k.reshape(B, L, NA_HEADS, NA_HEAD_DIM),
            v.reshape(B, L, NA_HEADS, NA_HEAD_DIM))


def _merge(h, y_f, y_m, y_n, p):
    g_f, g_m, g_n = jnp.split(jax.nn.sigmoid(h @ p['w_gate'] + p['b_gate']), 3, axis=-1)
    return (g_f * y_f + g_m * y_m + g_n * y_n) @ p['w_out']


def _mixer_context(h, p):
    u_f, u_q, u_kv, k_rope, u_na = jnp.split(h @ p['w_in'], IN_SPLITS, axis=-1)
    y_f = _fourier(u_f) @ p['w_branch_f']
    q_nope, q_rope = _mla_q(u_q, p['mla_q_norm'], p['mla_w_uq'])
    c_kv = _rmsnorm(u_kv, p['mla_kv_norm'])
    k_nope, v_m = _mla_kv_up(c_kv, p['mla_w_ukv'])
    q_m = jnp.concatenate([q_nope, q_rope], axis=-1)
    y_m = _block_attention(q_m, _mla_keys(k_nope, k_rope), v_m, MLA_SCALE) @ p['w_branch_m']
    q_n, k_n, v_n = _na_qkv(u_na)
    y_n = _block_attention(q_n, k_n, v_n, NA_SCALE) @ p['w_branch_n']
    return _merge(h, y_f, y_m, y_n, p), c_kv, k_rope, k_n, v_n


def _mixer_latent(h, p, ckv_ctx, krope_ctx, k_ctx, v_ctx):
    B, N, _ = h.shape
    cos, sin = _axial_rope_tables(N)
    u_f, u_q, u_kv, u_r, u_na = jnp.split(h @ p['w_in'], IN_SPLITS, axis=-1)
    y_f = _fourier(u_f) @ p['w_branch_f']
    q_nope, q_rope = _mla_q(u_q, p['mla_q_norm'], p['mla_w_uq'])
    q_m = jnp.concatenate([q_nope, _apply_axial_rope(q_rope, cos, sin)], axis=-1)
    c_kv = _rmsnorm(u_kv, p['mla_kv_norm'])
    k_nope, v_m = _mla_kv_up(c_kv, p['mla_w_ukv'])
    k_rope = _apply_axial_rope(u_r[:, :, None, :], cos, sin)[:, :, 0, :]
    k_nope_c, v_c = _mla_kv_up(ckv_ctx, p['mla_w_ukv'])
    k_all = jnp.concatenate([_mla_keys(k_nope, k_rope), _mla_keys(k_nope_c, krope_ctx)], axis=1)
    v_all = jnp.concatenate([v_m, v_c], axis=1)
    y_m = _block_attention(q_m, k_all, v_all, MLA_SCALE) @ p['w_branch_m']
    q_n, k_n, v_n = _na_qkv(u_na)
    y_n = _neighbourhood_attention(q_n, k_n, v_n, k_ctx, v_ctx, p['na_rpb']) @ p['w_branch_n']
    return _merge(h, y_f, y_m, y_n, p)


def setup_inputs(seed: int = 0) -> dict:
    key = jax.random.key(seed)
    ks = list(jax.random.split(key, 40))

    def nrm(i, shape, s):
        return jax.random.normal(ks[i], shape, F32) * s

    D = D_MODEL
    return {
        'x_prompt': nrm(0, (BATCH, SEQ, D), 1.0),
        'x_sample': nrm(1, (DEC_BATCH, DEC_SEQ, D), 1.0),
        'cache_mla_ckv': nrm(2, (DEC_BATCH, DEPTH, PAST_LEN, MLA_KV_LORA), 1.0),
        'cache_mla_krope': nrm(3, (DEC_BATCH, DEPTH, PAST_LEN, MLA_ROPE), 1.0),
        'cache_na_k': nrm(4, (DEC_BATCH, DEPTH, PAST_LEN, NA_HEADS, NA_HEAD_DIM), 1.0),
        'cache_na_v': nrm(5, (DEC_BATCH, DEPTH, PAST_LEN, NA_HEADS, NA_HEAD_DIM), 1.0),
        'c': nrm(6, (DEC_BATCH, D), 1.0),
        'c_ctx': nrm(7, (D,), 1.0),
        'w_ada': nrm(8, (DEPTH, D, 9 * D), 0.5 * D ** -0.5),
        'b_ada': nrm(9, (DEPTH, 9 * D), 0.01),
        'ffn1_w1': nrm(10, (DEPTH, D, D_FF), D ** -0.5),
        'ffn1_w3': nrm(11, (DEPTH, D, D_FF), D ** -0.5),
        'ffn1_w2': nrm(12, (DEPTH, D_FF, D), BETA * D_FF ** -0.5),
        'ffn2_w1': nrm(13, (DEPTH, D, D_FF), D ** -0.5),
        'ffn2_w3': nrm(14, (DEPTH, D, D_FF), D ** -0.5),
        'ffn2_w2': nrm(15, (DEPTH, D_FF, D), BETA * D_FF ** -0.5),
        'w_in': nrm(16, (DEPTH, D, IN_W), D ** -0.5),
        'mla_q_norm': 1.0 + nrm(17, (DEPTH, MLA_Q_LORA), 0.01),
        'mla_w_uq': nrm(18, (DEPTH, MLA_Q_LORA, MLA_HEADS * (MLA_NOPE + MLA_ROPE)), MLA_Q_LORA ** -0.5),
        'mla_kv_norm': 1.0 + nrm(19, (DEPTH, MLA_KV_LORA), 0.01),
        'mla_w_ukv': nrm(20, (DEPTH, MLA_KV_LORA, MLA_HEADS * (MLA_NOPE + MLA_V)), MLA_KV_LORA ** -0.5),
        'na_rpb': nrm(21, (DEPTH, NA_HEADS, 2 * NA_KH - 1, 2 * NA_KW - 1), 0.1),
        'w_branch_f': nrm(22, (DEPTH, F_W, D), BETA * F_W ** -0.5),
        'w_branch_m': nrm(23, (DEPTH, MLA_HEADS * MLA_V, D), BETA * (MLA_HEADS * MLA_V) ** -0.5),
        'w_branch_n': nrm(24, (DEPTH, NA_W, D), BETA * NA_W ** -0.5),
        'w_gate': nrm(25, (DEPTH, D, 3 * D), D ** -0.5),
        'b_gate': nrm(26, (DEPTH, 3 * D), 0.01),
        'w_out': nrm(27, (DEPTH, D, D), BETA * D ** -0.5),
        'ln_g': 1.0 + nrm(28, (DEPTH, 3, D), 0.01),
        'ln_b': nrm(29, (DEPTH, 3, D), 0.01),
    }


def reference(x_prompt, x_sample, cache_mla_ckv, cache_mla_krope, cache_na_k, cache_na_v, c, c_ctx,
              w_ada, b_ada, ffn1_w1, ffn1_w3, ffn1_w2, ffn2_w1, ffn2_w3, ffn2_w2, w_in,
              mla_q_norm, mla_w_uq, mla_kv_norm, mla_w_ukv, na_rpb, w_branch_f, w_branch_m, w_branch_n,
              w_gate, b_gate, w_out, ln_g, ln_b):
    xp = x_prompt
    xs = x_sample
    ckv_list, krope_list, nak_list, nav_list = [], [], [], []
    c_lat = c[:, None, :]
    for l in range(DEPTH):
        p = {'w_in': w_in[l], 'mla_q_norm': mla_q_norm[l], 'mla_w_uq': mla_w_uq[l],
             'mla_kv_norm': mla_kv_norm[l], 'mla_w_ukv': mla_w_ukv[l], 'na_rpb': na_rpb[l],
             'w_branch_f': w_branch_f[l], 'w_branch_m': w_branch_m[l], 'w_branch_n': w_branch_n[l],
             'w_gate': w_gate[l], 'b_gate': b_gate[l], 'w_out': w_out[l]}
        m = _adaln(c_ctx, w_ada[l], b_ada[l])
        xp = _ffn_sublayer(xp, m[0], m[1], m[2], ffn1_w1[l], ffn1_w3[l], ffn1_w2[l], ln_g[l, 0], ln_b[l, 0])
        mix, ckv, krope, nak, nav = _mixer_context(_modulate(xp, m[3], m[4]), p)
        xp = _layernorm(ALPHA * xp + m[5] * mix, ln_g[l, 1], ln_b[l, 1])
        xp = _ffn_sublayer(xp, m[6], m[7], m[8], ffn2_w1[l], ffn2_w3[l], ffn2_w2[l], ln_g[l, 2], ln_b[l, 2])
        ckv_list.append(ckv)
        krope_list.append(krope)
        nak_list.append(nak)
        nav_list.append(nav)
        ms = _adaln(c_lat, w_ada[l], b_ada[l])
        xs = _ffn_sublayer(xs, ms[0], ms[1], ms[2], ffn1_w1[l], ffn1_w3[l], ffn1_w2[l], ln_g[l, 0], ln_b[l, 0])
        mix_s = _mixer_latent(_modulate(xs, ms[3], ms[4]), p, cache_mla_ckv[:, l], cache_mla_krope[:, l],
                              cache_na_k[:, l], cache_na_v[:, l])
        xs = _layernorm(ALPHA * xs + ms[5] * mix_s, ln_g[l, 1], ln_b[l, 1])
        xs = _ffn_sublayer(xs, ms[6], ms[7], ms[8], ffn2_w1[l], ffn2_w3[l], ffn2_w2[l], ln_g[l, 2], ln_b[l, 2])
    new_mla_ckv = jnp.stack(ckv_list, axis=1)
    new_mla_krope = jnp.stack(krope_list, axis=1)
    new_na_k = jnp.stack(nak_list, axis=1)
    new_na_v = jnp.stack(nav_list, axis=1)
    return (xp, xs, new_mla_ckv, new_mla_krope, new_na_k, new_na_v)
```

```python
import functools

import numpy as np
import jax
import jax.numpy as jnp
from jax import lax
from jax.experimental import pallas as pl
from jax.experimental.pallas import tpu as pltpu

F32 = jnp.float32
BF16 = jnp.bfloat16

D_MODEL = 1024
BATCH = 32
SEQ = 256
DEPTH = 4
DEC_BATCH = 2
DEC_SEQ = 2048
PAST_LEN = 512
GRID_W = 64
GRID_H = DEC_SEQ // GRID_W
D_FF = 2816
F_GROUPS = 4
F_GC = 128
F_W = F_GROUPS * F_GC
MLA_HEADS = 8
MLA_Q_LORA = 384
MLA_KV_LORA = 256
MLA_NOPE = 64
MLA_ROPE = 32
MLA_V = 64
NA_HEADS = 8
NA_HEAD_DIM = 64
NA_KH = 8
NA_KW = 16
NA_W = NA_HEADS * NA_HEAD_DIM
ROPE_BASE = 10000.0
AXIS_DIM = MLA_ROPE // 2
ALPHA = (2.0 * DEPTH) ** 0.25
MLA_SCALE = (MLA_NOPE + MLA_ROPE) ** -0.5
NA_SCALE = NA_HEAD_DIM ** -0.5
NEG_INF = -1e30

LANE = 128
MLA_HP = LANE
MLA_QW = MLA_HEADS * MLA_HP
P_TOK = BATCH * SEQ
S_TOK = DEC_BATCH * DEC_SEQ
TOKENS = P_TOK + S_TOK
TM = 512
N_TILES = TOKENS // TM
P_TILES = P_TOK // TM
S_TILES_PER_REQ = DEC_SEQ // TM
FF_CHUNK = D_FF // 2
U_Q0 = F_W
U_KV0 = U_Q0 + MLA_Q_LORA
U_KR0 = U_KV0 + MLA_KV_LORA
U_NA0 = U_KR0 + LANE
U_W = U_NA0 + 3 * NA_W
ROPE_LANE0 = MLA_NOPE
NA_QROWS = 4
NA_QB = NA_QROWS * GRID_W
NA_WIN_ROWS = NA_QROWS + NA_KH
NA_WIN = NA_WIN_ROWS * GRID_W
NA_BLOCKS = GRID_H // NA_QROWS
MLA_QB = 256
VMEM_LIMIT = 56 * 1024 * 1024


def _group(i):
    return jnp.where(i < P_TILES, 0, 1 + (i - P_TILES) // S_TILES_PER_REQ)


def _rope_block(i):
    return jnp.where(i < P_TILES, 0, 1 + (i - P_TILES) % S_TILES_PER_REQ)


def _ln(x, eps):
    mu = jnp.mean(x, axis=-1, keepdims=True)
    xc = x - mu
    var = jnp.mean(xc * xc, axis=-1, keepdims=True)
    return xc * lax.rsqrt(var + eps)


def _rms(x, g):
    return x * lax.rsqrt(jnp.mean(x * x, axis=-1, keepdims=True) + 1e-6) * g


def _dot(a, b):
    return jnp.dot(a, b, preferred_element_type=F32)


def _dot_nt(a, b):
    return lax.dot_general(a, b, (((1,), (1,)), ((), ())), preferred_element_type=F32)


def _mod_rows(mod_ref, i):
    m = mod_ref[pl.ds(_group(i), 1), :]
    return m[:, :D_MODEL], m[:, D_MODEL:2 * D_MODEL], m[:, 2 * D_MODEL:]


def _rope(x, cos, sin):
    lane = lax.broadcasted_iota(jnp.int32, x.shape, 1)
    first_half = (lane % AXIS_DIM) < (AXIS_DIM // 2)
    rot = jnp.where(first_half, -pltpu.roll(x, LANE - AXIS_DIM // 2, 1), pltpu.roll(x, AXIS_DIM // 2, 1))
    return x * cos + rot * sin


def _adaln_kernel(c_ref, w_ref, b_ref, o_ref):
    c = c_ref[...]
    s = (c * jax.nn.sigmoid(c)).astype(BF16)
    o_ref[...] = _dot(s, w_ref[...].astype(BF16)) + b_ref[...]


def _adaln(cvec, w_ada, b_ada):
    n_col = 9 * D_MODEL // D_MODEL
    return pl.pallas_call(
        _adaln_kernel,
        grid=(DEPTH, n_col),
        in_specs=[pl.BlockSpec((8, D_MODEL), lambda l, j: (0, 0)),
                  pl.BlockSpec((None, D_MODEL, D_MODEL), lambda l, j: (l, 0, j)),
                  pl.BlockSpec((None, 1, D_MODEL), lambda l, j: (l, 0, j))],
        out_specs=pl.BlockSpec((None, 8, D_MODEL), lambda l, j: (l, 0, j)),
        out_shape=jax.ShapeDtypeStruct((DEPTH, 8, 9 * D_MODEL), F32),
        name="adaln",
    )(cvec, w_ada, b_ada.reshape(DEPTH, 1, 9 * D_MODEL))


def _ffn_kernel(x_ref, mod_ref, w1_ref, w3_ref, w2_ref, g_ref, b_ref, o_ref):
    i = pl.program_id(0)
    shift, scale, gate = _mod_rows(mod_ref, i)
    x = x_ref[...]
    h = (_ln(x, 1e-6) * (1.0 + scale) + shift).astype(BF16)
    y = None
    for c in range(D_FF // FF_CHUNK):
        sl = slice(c * FF_CHUNK, (c + 1) * FF_CHUNK)
        a = _dot(h, w1_ref[:, sl])
        t = (a * jax.nn.sigmoid(a) * _dot(h, w3_ref[:, sl])).astype(BF16)
        yc = _dot(t, w2_ref[sl, :])
        y = yc if y is None else y + yc
    z = ALPHA * x + (0.5 * gate) * y
    o_ref[...] = _ln(z, 1e-5) * g_ref[...] + b_ref[...]


def _resident(shape, index_map):
    return pl.BlockSpec(shape, index_map, pipeline_mode=pl.Buffered(1))


def _ffn(x, mod, w1, w3, w2, ln_g, ln_b, l, sub):
    return pl.pallas_call(
        _ffn_kernel,
        grid=(N_TILES,),
        in_specs=[pl.BlockSpec((TM, D_MODEL), lambda i: (i, 0)),
                  _resident((None, 8, 3 * D_MODEL), lambda i: (l, 0, sub)),
                  _resident((None, D_MODEL, D_FF), lambda i: (l, 0, 0)),
                  _resident((None, D_MODEL, D_FF), lambda i: (l, 0, 0)),
                  _resident((None, D_FF, D_MODEL), lambda i: (l, 0, 0)),
                  _resident((None, 1, D_MODEL), lambda i: (3 * l + sub, 0, 0)),
                  _resident((None, 1, D_MODEL), lambda i: (3 * l + sub, 0, 0))],
        out_specs=pl.BlockSpec((TM, D_MODEL), lambda i: (i, 0)),
        out_shape=jax.ShapeDtypeStruct((TOKENS, D_MODEL), F32),
        compiler_params=pltpu.CompilerParams(dimension_semantics=("arbitrary",),
                                             vmem_limit_bytes=VMEM_LIMIT),
        name="ffn",
    )(x, mod, w1, w3, w2, ln_g, ln_b)


def _mixer_in_kernel(x_ref, mod_ref, cos_ref, sin_ref, w_in_ref, qn_ref, w_uq_ref, kvn_ref, w_uk_ref, w_uv_ref,
                     uf_ref, q_ref, ckv_ref, kr_ref, k_ref, v_ref, qna_ref, kna_ref, vna_ref, knab_ref, vnab_ref):
    i = pl.program_id(0)
    shift, scale, _ = _mod_rows(mod_ref, i)
    h = (_ln(x_ref[...], 1e-6) * (1.0 + scale) + shift).astype(BF16)
    cos = cos_ref[...]
    sin = sin_ref[...]

    uf_ref[...] = _dot(h, w_in_ref[:, :U_Q0])

    u_q = _dot(h, w_in_ref[:, U_Q0:U_KV0])
    q = _dot(_rms(u_q, qn_ref[...]).astype(BF16), w_uq_ref[...])
    for hd in range(MLA_HEADS):
        sl = slice(hd * MLA_HP, (hd + 1) * MLA_HP)
        q_ref[:, sl] = _rope(q[:, sl], cos, sin).astype(BF16)

    c_kv = _rms(_dot(h, w_in_ref[:, U_KV0:U_KR0]), kvn_ref[...])
    ckv_ref[...] = c_kv
    c_kv = c_kv.astype(BF16)
    kr = _dot(h, w_in_ref[:, U_KR0:U_NA0])
    kr_ref[...] = kr[:, ROPE_LANE0:ROPE_LANE0 + MLA_ROPE]
    kr = _rope(kr, cos, sin)
    k = _dot(c_kv, w_uk_ref[...])
    for hd in range(MLA_HEADS):
        sl = slice(hd * MLA_HP, (hd + 1) * MLA_HP)
        k_ref[:, sl] = (k[:, sl] + kr).astype(BF16)
    v_ref[...] = _dot(c_kv, w_uv_ref[...]).astype(BF16)

    qna_ref[...] = _dot(h, w_in_ref[:, U_NA0:U_NA0 + NA_W]).astype(BF16)
    k_na = _dot(h, w_in_ref[:, U_NA0 + NA_W:U_NA0 + 2 * NA_W])
    kna_ref[...] = k_na
    knab_ref[...] = k_na.astype(BF16)
    v_na = _dot(h, w_in_ref[:, U_NA0 + 2 * NA_W:])
    vna_ref[...] = v_na
    vnab_ref[...] = v_na.astype(BF16)


def _mixer_in(x, mod, cos_t, sin_t, w_in, q_norm, w_uq, kv_norm, w_uk, w_uv, l):
    tile = lambda w: pl.BlockSpec((TM, w), lambda i: (i, 0))
    out_w = [(F_W, F32), (MLA_QW, BF16), (MLA_KV_LORA, F32), (MLA_ROPE, F32), (MLA_QW, BF16),
             (MLA_HEADS * MLA_V, BF16), (NA_W, BF16), (NA_W, F32), (NA_W, F32), (NA_W, BF16), (NA_W, BF16)]
    return pl.pallas_call(
        _mixer_in_kernel,
        grid=(N_TILES,),
        in_specs=[tile(D_MODEL),
                  _resident((None, 8, 3 * D_MODEL), lambda i: (l, 0, 1)),
                  pl.BlockSpec((TM, LANE), lambda i: (_rope_block(i), 0)),
                  pl.BlockSpec((TM, LANE), lambda i: (_rope_block(i), 0)),
                  _resident((None, D_MODEL, U_W), lambda i: (l, 0, 0)),
                  _resident((None, 1, MLA_Q_LORA), lambda i: (l, 0, 0)),
                  _resident((None, MLA_Q_LORA, MLA_QW), lambda i: (l, 0, 0)),
                  _resident((None, 1, MLA_KV_LORA), lambda i: (l, 0, 0)),
                  _resident((None, MLA_KV_LORA, MLA_QW), lambda i: (l, 0, 0)),
                  _resident((None, MLA_KV_LORA, MLA_HEADS * MLA_V), lambda i: (l, 0, 0))],
        out_specs=[tile(w) for w, _ in out_w],
        out_shape=[jax.ShapeDtypeStruct((TOKENS, w), dt) for w, dt in out_w],
        compiler_params=pltpu.CompilerParams(dimension_semantics=("arbitrary",),
                                             vmem_limit_bytes=VMEM_LIMIT),
        name="mixer_in",
    )(x, mod, cos_t, sin_t, w_in, q_norm, w_uq, kv_norm, w_uk, w_uv)


def _ctx_kv_kernel(ckv_ref, kr_ref, w_uk_ref, w_uv_ref, k_ref, v_ref):
    c = ckv_ref[...].astype(BF16)
    k = _dot(c, w_uk_ref[...])
    kr = kr_ref[...]
    for hd in range(MLA_HEADS):
        sl = slice(hd * MLA_HP, (hd + 1) * MLA_HP)
        k_ref[:, sl] = (k[:, sl] + kr).astype(BF16)
    v_ref[...] = _dot(c, w_uv_ref[...]).astype(BF16)


def _ctx_kv(cache_ckv, cache_kr_pad, w_uk, w_uv):
    return pl.pallas_call(
        _ctx_kv_kernel,
        grid=(DEC_BATCH, DEPTH),
        in_specs=[pl.BlockSpec((None, None, PAST_LEN, MLA_KV_LORA), lambda b, l: (b, l, 0, 0)),
                  pl.BlockSpec((None, None, PAST_LEN, LANE), lambda b, l: (b, l, 0, 0)),
                  pl.BlockSpec((None, MLA_KV_LORA, MLA_QW), lambda b, l: (l, 0, 0)),
                  pl.BlockSpec((None, MLA_KV_LORA, MLA_HEADS * MLA_V), lambda b, l: (l, 0, 0))],
        out_specs=[pl.BlockSpec((None, None, PAST_LEN, MLA_QW), lambda b, l: (b, l, 0, 0)),
                   pl.BlockSpec((None, None, PAST_LEN, MLA_HEADS * MLA_V), lambda b, l: (b, l, 0, 0))],
        out_shape=[jax.ShapeDtypeStruct((DEC_BATCH, DEPTH, PAST_LEN, MLA_QW), BF16),
                   jax.ShapeDtypeStruct((DEC_BATCH, DEPTH, PAST_LEN, MLA_HEADS * MLA_V), BF16)],
        name="ctx_kv",
    )(cache_ckv, cache_kr_pad, w_uk, w_uv)


def _fourier_kernel(x_ref, cs_ref, cl_ref, sl_ref, o_ref):
    x = x_ref[...].astype(BF16)
    xc, xs = [], []
    for g in range(F_GROUPS):
        t = _dot(x[:, g * F_GC:(g + 1) * F_GC], cs_ref[...])
        xc.append(t[:, :F_GC])
        xs.append(t[:, F_GC:])
    xc = jnp.concatenate(xc, axis=1).astype(BF16)
    xs = jnp.concatenate(xs, axis=1).astype(BF16)
    o_ref[...] = (_dot(cl_ref[...], xc) - _dot(sl_ref[...], xs)).astype(BF16)


def _dft_tables(n):
    k = np.arange(n, dtype=np.int64)
    ang = 2.0 * np.pi * ((k[:, None] * k[None, :]) % n).astype(np.float64) / n
    s = n ** -0.5
    return np.cos(ang) * s, np.sin(ang) * s


def _fourier_tables(length):
    cc, sc = _dft_tables(F_GC)
    cl, sl = _dft_tables(length)
    as_bf16 = lambda a: jnp.asarray(a, F32).astype(BF16)
    return as_bf16(np.concatenate([cc, sc], axis=1)), as_bf16(cl), as_bf16(sl)


def _fourier(u_f, prev, tables, length, first_block, n_blocks):
    cs, cl, sl = tables
    in_specs = [pl.BlockSpec((length, F_W), lambda b: (first_block + b, 0)),
                _resident((F_GC, 2 * F_GC), lambda b: (0, 0)),
                _resident((length, length), lambda b: (0, 0)),
                _resident((length, length), lambda b: (0, 0))]
    args = [u_f, cs, cl, sl]
    aliases = {}
    kernel = _fourier_kernel
    if prev is not None:
        in_specs.append(pl.BlockSpec(memory_space=pl.ANY))
        args.append(prev)
        aliases = {4: 0}
        kernel = lambda x, a, b, c, p, o: _fourier_kernel(x, a, b, c, o)
    return pl.pallas_call(
        kernel,
        grid=(n_blocks,),
        in_specs=in_specs,
        out_specs=pl.BlockSpec((length, F_W), lambda b: (first_block + b, 0)),
        out_shape=jax.ShapeDtypeStruct((TOKENS, F_W), BF16),
        input_output_aliases=aliases,
        compiler_params=pltpu.CompilerParams(dimension_semantics=("arbitrary",),
                                             vmem_limit_bytes=VMEM_LIMIT),
        name="fourier_%d" % length,
    )(*args)


def _softmax_pv(scores, values, scale):
    m = None
    for s in scores:
        sm = jnp.max(s, axis=-1, keepdims=True)
        m = sm if m is None else jnp.maximum(m, sm)
    den = None
    acc = None
    for s, v in zip(scores, values):
        p = jnp.exp((s - m) * scale)
        ps = jnp.sum(p, axis=-1, keepdims=True)
        den = ps if den is None else den + ps
        pv = _dot(p.astype(BF16), v)
        acc = pv if acc is None else acc + pv
    return acc / den


def _ctx_attn_kernel(q_ref, k_ref, v_ref, qn_ref, kn_ref, vn_ref, om_ref, on_ref):
    for hd in range(MLA_HEADS):
        sl = slice(hd * MLA_HP, (hd + 1) * MLA_HP)
        vs = slice(hd * MLA_V, (hd + 1) * MLA_V)
        s = _dot_nt(q_ref[:, sl], k_ref[:, sl])
        om_ref[:, vs] = _softmax_pv([s], [v_ref[:, vs]], MLA_SCALE).astype(BF16)
    for hd in range(NA_HEADS):
        sl = slice(hd * NA_HEAD_DIM, (hd + 1) * NA_HEAD_DIM)
        s = _dot_nt(qn_ref[:, sl], kn_ref[:, sl])
        on_ref[:, sl] = _softmax_pv([s], [vn_ref[:, sl]], NA_SCALE).astype(BF16)


def _ctx_attn(q, k, v, qn, kn, vn):
    blk = lambda w: pl.BlockSpec((SEQ, w), lambda b: (b, 0))
    return pl.pallas_call(
        _ctx_attn_kernel,
        grid=(BATCH,),
        in_specs=[blk(MLA_QW), blk(MLA_QW), blk(MLA_HEADS * MLA_V), blk(NA_W), blk(NA_W), blk(NA_W)],
        out_specs=[blk(MLA_HEADS * MLA_V), blk(NA_W)],
        out_shape=[jax.ShapeDtypeStruct((TOKENS, MLA_HEADS * MLA_V), BF16),
                   jax.ShapeDtypeStruct((TOKENS, NA_W), BF16)],
        compiler_params=pltpu.CompilerParams(dimension_semantics=("arbitrary",)),
        name="ctx_attn",
    )(q, k, v, qn, kn, vn)


def _lat_mla_kernel(q_ref, k_ref, v_ref, kc_ref, vc_ref, prev_ref, o_ref):
    del prev_ref
    for hd in range(MLA_HEADS):
        sl = slice(hd * MLA_HP, (hd + 1) * MLA_HP)
        vs = slice(hd * MLA_V, (hd + 1) * MLA_V)
        q = q_ref[:, sl]
        s_lat = _dot_nt(q, k_ref[:, sl])
        s_ctx = _dot_nt(q, kc_ref[:, sl])
        o_ref[:, vs] = _softmax_pv([s_lat, s_ctx], [v_ref[:, vs], vc_ref[:, vs]], MLA_SCALE).astype(BF16)


def _lat_mla(q, k, v, kc, vc, prev, l):
    nq = DEC_SEQ // MLA_QB
    q0 = P_TOK // MLA_QB
    r0 = P_TOK // DEC_SEQ
    return pl.pallas_call(
        _lat_mla_kernel,
        grid=(DEC_BATCH, nq),
        in_specs=[pl.BlockSpec((MLA_QB, MLA_QW), lambda b, j: (q0 + b * nq + j, 0)),
                  pl.BlockSpec((DEC_SEQ, MLA_QW), lambda b, j: (r0 + b, 0)),
                  pl.BlockSpec((DEC_SEQ, MLA_HEADS * MLA_V), lambda b, j: (r0 + b, 0)),
                  pl.BlockSpec((None, None, PAST_LEN, MLA_QW), lambda b, j: (b, l, 0, 0)),
                  pl.BlockSpec((None, None, PAST_LEN, MLA_HEADS * MLA_V), lambda b, j: (b, l, 0, 0)),
                  pl.BlockSpec(memory_space=pl.ANY)],
        out_specs=pl.BlockSpec((MLA_QB, MLA_HEADS * MLA_V), lambda b, j: (q0 + b * nq + j, 0)),
        out_shape=jax.ShapeDtypeStruct((TOKENS, MLA_HEADS * MLA_V), BF16),
        input_output_aliases={5: 0},
        compiler_params=pltpu.CompilerParams(dimension_semantics=("arbitrary", "arbitrary"),
                                             vmem_limit_bytes=VMEM_LIMIT),
        name="lat_mla",
    )(q, k, v, kc, vc, prev)


def _na_window_block(j):
    return jnp.clip(j - 1, 0, NA_BLOCKS - NA_WIN_ROWS // NA_QROWS)


def _lat_na_kernel(q_ref, k_ref, v_ref, kc_ref, vc_ref, bias_ref, prev_ref, o_ref):
    del prev_ref
    j = pl.program_id(1)
    start = pl.multiple_of(_na_window_block(j) * NA_QB, NA_QB)
    k_win = k_ref[pl.ds(start, NA_WIN), :]
    v_win = v_ref[pl.ds(start, NA_WIN), :]
    k_ctx = kc_ref[...].astype(BF16)
    v_ctx = vc_ref[...].astype(BF16)
    for hd in range(NA_HEADS):
        sl = slice(hd * NA_HEAD_DIM, (hd + 1) * NA_HEAD_DIM)
        q = q_ref[:, sl]
        s_loc = _dot_nt(q, k_win[:, sl]) + bias_ref[hd]
        s_ctx = _dot_nt(q, k_ctx[:, sl])
        o_ref[:, sl] = _softmax_pv([s_loc, s_ctx], [v_win[:, sl], v_ctx[:, sl]], NA_SCALE).astype(BF16)


def _na_bias_table(rpb):
    blocks = np.array([0, 1, NA_BLOCKS - 1])
    start_row = NA_QROWS * np.clip(blocks - 1, 0, NA_BLOCKS - NA_WIN_ROWS // NA_QROWS)
    qi = np.arange(NA_QB)
    ki = np.arange(NA_WIN)
    q_row = NA_QROWS * blocks[:, None] + qi[None, :] // GRID_W
    q_col = qi % GRID_W
    k_row = start_row[:, None] + ki[None, :] // GRID_W
    k_col = ki % GRID_W
    row_start = np.clip(q_row - NA_KH // 2, 0, GRID_H - NA_KH)
    row_ok = (k_row[:, None, :] >= row_start[:, :, None]) & (k_row[:, None, :] < row_start[:, :, None] + NA_KH)
    col_start = np.clip(q_col - NA_KW // 2, 0, GRID_W - NA_KW)
    col_ok = (k_col[None, :] >= col_start[:, None]) & (k_col[None, :] < col_start[:, None] + NA_KW)
    ok = row_ok & col_ok[None]
    dr = np.clip(k_row[:, None, :] - q_row[:, :, None] + (NA_KH - 1), 0, 2 * NA_KH - 2)
    dc = np.clip(k_col[None, :] - q_col[:, None], -(NA_KW - 1), NA_KW - 1) + (NA_KW - 1)
    dc = np.broadcast_to(dc[None], dr.shape)
    bias = rpb[:, dr, dc]
    bias = jnp.where(ok[None], bias * (1.0 / NA_SCALE), NEG_INF)
    return jnp.transpose(bias, (1, 0, 2, 3))


def _lat_na(q, k, v, cache_k, cache_v, bias, prev, l):
    q0 = P_TOK // NA_QB
    r0 = P_TOK // DEC_SEQ
    kind = lambda j: (j > 0).astype(jnp.int32) + (j == NA_BLOCKS - 1).astype(jnp.int32)
    return pl.pallas_call(
        _lat_na_kernel,
        grid=(DEC_BATCH, NA_BLOCKS),
        in_specs=[pl.BlockSpec((NA_QB, NA_W), lambda b, j: (q0 + b * NA_BLOCKS + j, 0)),
                  pl.BlockSpec((DEC_SEQ, NA_W), lambda b, j: (r0 + b, 0)),
                  pl.BlockSpec((DEC_SEQ, NA_W), lambda b, j: (r0 + b, 0)),
                  pl.BlockSpec((None, None, PAST_LEN, NA_W), lambda b, j: (b, l, 0, 0)),
                  pl.BlockSpec((None, None, PAST_LEN, NA_W), lambda b, j: (b, l, 0, 0)),
                  pl.BlockSpec((None, NA_HEADS, NA_QB, NA_WIN), lambda b, j: (kind(j), 0, 0, 0)),
                  pl.BlockSpec(memory_space=pl.ANY)],
        out_specs=pl.BlockSpec((NA_QB, NA_W), lambda b, j: (q0 + b * NA_BLOCKS + j, 0)),
        out_shape=jax.ShapeDtypeStruct((TOKENS, NA_W), BF16),
        input_output_aliases={6: 0},
        compiler_params=pltpu.CompilerParams(dimension_semantics=("arbitrary", "arbitrary"),
                                             vmem_limit_bytes=VMEM_LIMIT),
        name="lat_na",
    )(q, k, v, cache_k, cache_v, bias, prev)


def _merge_kernel(x_ref, mod_ref, yf_ref, ym_ref, yn_ref, wg_ref, bg_ref, wf_ref, wm_ref, wn_ref, wo_ref,
                  g_ref, b_ref, o_ref):
    i = pl.program_id(0)
    shift, scale, gate = _mod_rows(mod_ref, i)
    x = x_ref[...]
    h = (_ln(x, 1e-6) * (1.0 + scale) + shift).astype(BF16)
    mix = None
    for n, (y_ref, w_ref) in enumerate(((yf_ref, wf_ref), (ym_ref, wm_ref), (yn_ref, wn_ref))):
        sl = slice(n * D_MODEL, (n + 1) * D_MODEL)
        g = jax.nn.sigmoid(_dot(h, wg_ref[:, sl]) + bg_ref[:, sl])
        t = g * _dot(y_ref[...], w_ref[...])
        mix = t if mix is None else mix + t
    z = ALPHA * x + gate * _dot(mix.astype(BF16), wo_ref[...])
    o_ref[...] = _ln(z, 1e-5) * g_ref[...] + b_ref[...]


def _merge(x, mod, y_f, y_m, y_n, w_gate, b_gate, w_f, w_m, w_n, w_out, ln_g, ln_b, l):
    tile = lambda w: pl.BlockSpec((TM, w), lambda i: (i, 0))
    return pl.pallas_call(
        _merge_kernel,
        grid=(N_TILES,),
        in_specs=[tile(D_MODEL),
                  _resident((None, 8, 3 * D_MODEL), lambda i: (l, 0, 1)),
                  tile(F_W), tile(MLA_HEADS * MLA_V), tile(NA_W),
                  _resident((None, D_MODEL, 3 * D_MODEL), lambda i: (l, 0, 0)),
                  _resident((None, 1, 3 * D_MODEL), lambda i: (l, 0, 0)),
                  _resident((None, F_W, D_MODEL), lambda i: (l, 0, 0)),
                  _resident((None, MLA_HEADS * MLA_V, D_MODEL), lambda i: (l, 0, 0)),
                  _resident((None, NA_W, D_MODEL), lambda i: (l, 0, 0)),
                  _resident((None, D_MODEL, D_MODEL), lambda i: (l, 0, 0)),
                  _resident((None, 1, D_MODEL), lambda i: (3 * l + 1, 0, 0)),
                  _resident((None, 1, D_MODEL), lambda i: (3 * l + 1, 0, 0))],
        out_specs=tile(D_MODEL),
        out_shape=jax.ShapeDtypeStruct((TOKENS, D_MODEL), F32),
        compiler_params=pltpu.CompilerParams(dimension_semantics=("arbitrary",),
                                             vmem_limit_bytes=VMEM_LIMIT),
        name="merge",
    )(x, mod, y_f, y_m, y_n, w_gate, b_gate, w_f, w_m, w_n, w_out, ln_g, ln_b)


def _pad_heads(w, n_heads, width):
    lead = w.shape[:-1]
    w = w.reshape(lead + (n_heads, width))
    w = jnp.pad(w, [(0, 0)] * len(lead) + [(0, 0), (0, MLA_HP - width)])
    return w.reshape(lead + (n_heads * MLA_HP,))


def _rope_tables():
    t = jnp.arange(DEC_SEQ, dtype=jnp.int32)
    pos = jnp.stack([t // GRID_W, t % GRID_W], axis=-1).astype(F32)
    half = AXIS_DIM // 2
    inv_freq = ROPE_BASE ** (-jnp.arange(half, dtype=F32) / half)
    ang = pos[:, :, None] * inv_freq
    ang = jnp.concatenate([ang, ang], axis=-1).reshape(DEC_SEQ, MLA_ROPE)
    pad = lambda a, fill: jnp.pad(a, ((0, 0), (ROPE_LANE0, LANE - ROPE_LANE0 - MLA_ROPE)), constant_values=fill)
    cos = jnp.concatenate([jnp.ones((TM, LANE), F32), pad(jnp.cos(ang), 1.0)], axis=0)
    sin = jnp.concatenate([jnp.zeros((TM, LANE), F32), pad(jnp.sin(ang), 0.0)], axis=0)
    return cos, sin


def kernel(x_prompt, x_sample, cache_mla_ckv, cache_mla_krope, cache_na_k, cache_na_v, c, c_ctx, w_ada, b_ada, ffn1_w1, ffn1_w3, ffn1_w2, ffn2_w1, ffn2_w3, ffn2_w2, w_in, mla_q_norm, mla_w_uq, mla_kv_norm, mla_w_ukv, na_rpb, w_branch_f, w_branch_m, w_branch_n, w_gate, b_gate, w_out, ln_g, ln_b):
    bf = lambda w: w.astype(BF16)
    f1 = (bf(ffn1_w1), bf(ffn1_w3), bf(ffn1_w2))
    f2 = (bf(ffn2_w1), bf(ffn2_w3), bf(ffn2_w2))
    w_in_p = bf(jnp.concatenate(
        [w_in[..., :U_KR0],
         jnp.pad(w_in[..., U_KR0:U_KR0 + MLA_ROPE], ((0, 0), (0, 0), (ROPE_LANE0, LANE - ROPE_LANE0 - MLA_ROPE))),
         w_in[..., U_KR0 + MLA_ROPE:]], axis=-1))
    w_uq_p = bf(_pad_heads(mla_w_uq, MLA_HEADS, MLA_NOPE + MLA_ROPE))
    ukv = mla_w_ukv.reshape(DEPTH, MLA_KV_LORA, MLA_HEADS, MLA_NOPE + MLA_V)
    w_uk_p = bf(_pad_heads(ukv[..., :MLA_NOPE].reshape(DEPTH, MLA_KV_LORA, MLA_HEADS * MLA_NOPE), MLA_HEADS, MLA_NOPE))
    w_uv = bf(ukv[..., MLA_NOPE:].reshape(DEPTH, MLA_KV_LORA, MLA_HEADS * MLA_V))
    w_gate_b, w_f, w_m, w_n, w_out_b = bf(w_gate), bf(w_branch_f), bf(w_branch_m), bf(w_branch_n), bf(w_out)
    q_norm = mla_q_norm.reshape(DEPTH, 1, MLA_Q_LORA)
    kv_norm = mla_kv_norm.reshape(DEPTH, 1, MLA_KV_LORA)
    b_gate3 = b_gate.reshape(DEPTH, 1, 3 * D_MODEL)
    g3 = ln_g.reshape(DEPTH * 3, 1, D_MODEL)
    b3 = ln_b.reshape(DEPTH * 3, 1, D_MODEL)
    cos_t, sin_t = _rope_tables()
    dft_ctx = _fourier_tables(SEQ)
    dft_lat = _fourier_tables(DEC_SEQ)
    kr_pad = jnp.pad(cache_mla_krope, ((0, 0), (0, 0), (0, 0), (ROPE_LANE0, LANE - ROPE_LANE0 - MLA_ROPE)))
    cache_k = cache_na_k.reshape(DEC_BATCH, DEPTH, PAST_LEN, NA_W)
    cache_v = cache_na_v.reshape(DEC_BATCH, DEPTH, PAST_LEN, NA_W)

    cvec = jnp.concatenate([c_ctx[None], c, jnp.zeros((8 - 1 - DEC_BATCH, D_MODEL), F32)], axis=0)
    mod = _adaln(cvec, w_ada, b_ada)
    kc, vc = _ctx_kv(cache_mla_ckv, kr_pad, w_uk_p, w_uv)

    x = jnp.concatenate([x_prompt.reshape(P_TOK, D_MODEL), x_sample.reshape(S_TOK, D_MODEL)], axis=0)
    ckv_l, kr_l, nak_l, nav_l = [], [], [], []
    for l in range(DEPTH):
        x = _ffn(x, mod, *f1, g3, b3, l, 0)
        u_f, q, c_kv, k_r, k, v, q_n, k_n, v_n, k_nb, v_nb = _mixer_in(
            x, mod, cos_t, sin_t, w_in_p, q_norm, w_uq_p, kv_norm, w_uk_p, w_uv, l)
        y_f = _fourier(u_f, None, dft_ctx, SEQ, 0, BATCH)
        y_f = _fourier(u_f, y_f, dft_lat, DEC_SEQ, P_TOK // DEC_SEQ, DEC_BATCH)
        y_m, y_n = _ctx_attn(q, k, v, q_n, k_nb, v_nb)
        y_m = _lat_mla(q, k, v, kc, vc, y_m, l)
        y_n = _lat_na(q_n, k_nb, v_nb, cache_k, cache_v, _na_bias_table(na_rpb[l]), y_n, l)
        x = _merge(x, mod, y_f, y_m, y_n, w_gate_b, b_gate3, w_f, w_m, w_n, w_out_b, g3, b3, l)
        x = _ffn(x, mod, *f2, g3, b3, l, 2)
        ckv_l.append(c_kv[:P_TOK].reshape(BATCH, SEQ, MLA_KV_LORA))
        kr_l.append(k_r[:P_TOK].reshape(BATCH, SEQ, MLA_ROPE))
        nak_l.append(k_n[:P_TOK].reshape(BATCH, SEQ, NA_HEADS, NA_HEAD_DIM))
        nav_l.append(v_n[:P_TOK].reshape(BATCH, SEQ, NA_HEADS, NA_HEAD_DIM))
    return (x[:P_TOK].reshape(BATCH, SEQ, D_MODEL), x[P_TOK:].reshape(DEC_BATCH, DEC_SEQ, D_MODEL),
            jnp.stack(ckv_l, axis=1), jnp.stack(kr_l, axis=1), jnp.stack(nak_l, axis=1), jnp.stack(nav_l, axis=1))
```

```python
import functools

import numpy as np
import jax
import jax.numpy as jnp
from jax import lax
from jax.experimental import pallas as pl
from jax.experimental.pallas import tpu as pltpu

F32 = jnp.float32
BF16 = jnp.bfloat16

D_MODEL = 1024
BATCH = 32
SEQ = 256
DEPTH = 4
DEC_BATCH = 2
DEC_SEQ = 2048
PAST_LEN = 512
GRID_W = 64
GRID_H = DEC_SEQ // GRID_W
D_FF = 2816
F_GROUPS = 4
F_GC = 128
F_W = F_GROUPS * F_GC
MLA_HEADS = 8
MLA_Q_LORA = 384
MLA_KV_LORA = 256
MLA_NOPE = 64
MLA_ROPE = 32
MLA_V = 64
NA_HEADS = 8
NA_HEAD_DIM = 64
NA_KH = 8
NA_KW = 16
NA_W = NA_HEADS * NA_HEAD_DIM
ROPE_BASE = 10000.0
AXIS_DIM = MLA_ROPE // 2
ALPHA = (2.0 * DEPTH) ** 0.25
MLA_SCALE = (MLA_NOPE + MLA_ROPE) ** -0.5
NA_SCALE = NA_HEAD_DIM ** -0.5
NEG_INF = -1e30

LANE = 128
MLA_HP = LANE
MLA_QW = MLA_HEADS * MLA_HP
P_TOK = BATCH * SEQ
S_TOK = DEC_BATCH * DEC_SEQ
TOKENS = P_TOK + S_TOK
TM = 512
N_TILES = TOKENS // TM
P_TILES = P_TOK // TM
S_TILES_PER_REQ = DEC_SEQ // TM
FF_CHUNK = D_FF // 2
U_Q0 = F_W
U_KV0 = U_Q0 + MLA_Q_LORA
U_KR0 = U_KV0 + MLA_KV_LORA
U_NA0 = U_KR0 + LANE
U_W = U_NA0 + 3 * NA_W
ROPE_LANE0 = MLA_NOPE
NA_QROWS = 4
NA_QB = NA_QROWS * GRID_W
NA_WIN_ROWS = NA_QROWS + NA_KH
NA_WIN = NA_WIN_ROWS * GRID_W
NA_BLOCKS = GRID_H // NA_QROWS
MLA_QB = 256
VMEM_LIMIT = 56 * 1024 * 1024


def _group(i):
    return jnp.where(i < P_TILES, 0, 1 + (i - P_TILES) // S_TILES_PER_REQ)


def _rope_block(i):
    return jnp.where(i < P_TILES, 0, 1 + (i - P_TILES) % S_TILES_PER_REQ)


def _ln(x, eps):
    mu = jnp.mean(x, axis=-1, keepdims=True)
    xc = x - mu
    var = jnp.mean(xc * xc, axis=-1, keepdims=True)
    return xc * lax.rsqrt(var + eps)


def _rms(x, g):
    return x * lax.rsqrt(jnp.mean(x * x, axis=-1, keepdims=True) + 1e-6) * g


def _dot(a, b):
    return jnp.dot(a, b, preferred_element_type=F32)


def _dot_nt(a, b):
    return lax.dot_general(a, b, (((1,), (1,)), ((), ())), preferred_element_type=F32)


def _mod_rows(mod_ref, i):
    m = mod_ref[pl.ds(_group(i), 1), :]
    return m[:, :D_MODEL], m[:, D_MODEL:2 * D_MODEL], m[:, 2 * D_MODEL:]


def _rope(x, cos, sin):
    lane = lax.broadcasted_iota(jnp.int32, x.shape, 1)
    first_half = (lane % AXIS_DIM) < (AXIS_DIM // 2)
    rot = jnp.where(first_half, -pltpu.roll(x, LANE - AXIS_DIM // 2, 1), pltpu.roll(x, AXIS_DIM // 2, 1))
    return x * cos + rot * sin


def _adaln_kernel(c_ref, w_ref, b_ref, o_ref):
    c = c_ref[...]
    s = (c * jax.nn.sigmoid(c)).astype(BF16)
    o_ref[...] = _dot(s, w_ref[...].astype(BF16)) + b_ref[...]


def _adaln(cvec, w_ada, b_ada):
    n_col = 9 * D_MODEL // D_MODEL
    return pl.pallas_call(
        _adaln_kernel,
        grid=(DEPTH, n_col),
        in_specs=[pl.BlockSpec((8, D_MODEL), lambda l, j: (0, 0)),
                  pl.BlockSpec((None, D_MODEL, D_MODEL), lambda l, j: (l, 0, j)),
                  pl.BlockSpec((None, 1, D_MODEL), lambda l, j: (l, 0, j))],
        out_specs=pl.BlockSpec((None, 8, D_MODEL), lambda l, j: (l, 0, j)),
        out_shape=jax.ShapeDtypeStruct((DEPTH, 8, 9 * D_MODEL), F32),
        name="adaln",
    )(cvec, w_ada, b_ada.reshape(DEPTH, 1, 9 * D_MODEL))


def _ffn_kernel(x_ref, mod_ref, w1_ref, w3_ref, w2_ref, g_ref, b_ref, o_ref):
    i = pl.program_id(0)
    shift, scale, gate = _mod_rows(mod_ref, i)
    x = x_ref[...]
    h = (_ln(x, 1e-6) * (1.0 + scale) + shift).astype(BF16)
    y = None
    for c in range(D_FF // FF_CHUNK):
        sl = slice(c * FF_CHUNK, (c + 1) * FF_CHUNK)
        a = _dot(h, w1_ref[:, sl])
        t = (a * jax.nn.sigmoid(a) * _dot(h, w3_ref[:, sl])).astype(BF16)
        yc = _dot(t, w2_ref[sl, :])
        y = yc if y is None else y + yc
    z = ALPHA * x + (0.5 * gate) * y
    o_ref[...] = _ln(z, 1e-5) * g_ref[...] + b_ref[...]


def _resident(shape, index_map):
    return pl.BlockSpec(shape, index_map, pipeline_mode=pl.Buffered(1))


def _ffn(x, mod, w1, w3, w2, ln_g, ln_b, l, sub):
    return pl.pallas_call(
        _ffn_kernel,
        grid=(N_TILES,),
        in_specs=[pl.BlockSpec((TM, D_MODEL), lambda i: (i, 0)),
                  _resident((None, 8, 3 * D_MODEL), lambda i: (l, 0, sub)),
                  _resident((None, D_MODEL, D_FF), lambda i: (l, 0, 0)),
                  _resident((None, D_MODEL, D_FF), lambda i: (l, 0, 0)),
                  _resident((None, D_FF, D_MODEL), lambda i: (l, 0, 0)),
                  _resident((None, 1, D_MODEL), lambda i: (3 * l + sub, 0, 0)),
                  _resident((None, 1, D_MODEL), lambda i: (3 * l + sub, 0, 0))],
        out_specs=pl.BlockSpec((TM, D_MODEL), lambda i: (i, 0)),
        out_shape=jax.ShapeDtypeStruct((TOKENS, D_MODEL), F32),
        compiler_params=pltpu.CompilerParams(dimension_semantics=("arbitrary",),
                                             vmem_limit_bytes=VMEM_LIMIT),
        name="ffn",
    )(x, mod, w1, w3, w2, ln_g, ln_b)


def _mixer_in_kernel(x_ref, mod_ref, cos_ref, sin_ref, w_in_ref, qn_ref, w_uq_ref, kvn_ref, w_uk_ref, w_uv_ref,
                     uf_ref, q_ref, ckv_ref, kr_ref, k_ref, v_ref, qna_ref, kna_ref, vna_ref, knab_ref, vnab_ref):
    i = pl.program_id(0)
    shift, scale, _ = _mod_rows(mod_ref, i)
    h = (_ln(x_ref[...], 1e-6) * (1.0 + scale) + shift).astype(BF16)
    cos = cos_ref[...]
    sin = sin_ref[...]

    uf_ref[...] = _dot(h, w_in_ref[:, :U_Q0])

    u_q = _dot(h, w_in_ref[:, U_Q0:U_KV0])
    q = _dot(_rms(u_q, qn_ref[...]).astype(BF16), w_uq_ref[...])
    for hd in range(MLA_HEADS):
        sl = slice(hd * MLA_HP, (hd + 1) * MLA_HP)
        q_ref[:, sl] = _rope(q[:, sl], cos, sin).astype(BF16)

    c_kv = _rms(_dot(h, w_in_ref[:, U_KV0:U_KR0]), kvn_ref[...])
    kr = _dot(h, w_in_ref[:, U_KR0:U_NA0])
    k_na = _dot(h, w_in_ref[:, U_NA0 + NA_W:U_NA0 + 2 * NA_W])
    v_na = _dot(h, w_in_ref[:, U_NA0 + 2 * NA_W:])

    @pl.when(i < P_TILES)
    def _():
        ckv_ref[...] = c_kv
        kr_ref[...] = kr[:, ROPE_LANE0:ROPE_LANE0 + MLA_ROPE]
        kna_ref[...] = k_na
        vna_ref[...] = v_na

    c_kv = c_kv.astype(BF16)
    kr = _rope(kr, cos, sin)
    k = _dot(c_kv, w_uk_ref[...])
    for hd in range(MLA_HEADS):
        sl = slice(hd * MLA_HP, (hd + 1) * MLA_HP)
        k_ref[:, sl] = (k[:, sl] + kr).astype(BF16)
    v_ref[...] = _dot(c_kv, w_uv_ref[...]).astype(BF16)

    qna_ref[...] = _dot(h, w_in_ref[:, U_NA0:U_NA0 + NA_W]).astype(BF16)
    knab_ref[...] = k_na.astype(BF16)
    vnab_ref[...] = v_na.astype(BF16)


def _mixer_in(x, mod, cos_t, sin_t, w_in, q_norm, w_uq, kv_norm, w_uk, w_uv, l):
    tile = lambda w: pl.BlockSpec((TM, w), lambda i: (i, 0))
    ctx_tile = lambda w: pl.BlockSpec((TM, w), lambda i: (jnp.minimum(i, P_TILES - 1), 0))
    outs = [(F_W, F32, False), (MLA_QW, BF16, False), (MLA_KV_LORA, F32, True), (MLA_ROPE, F32, True),
            (MLA_QW, BF16, False), (MLA_HEADS * MLA_V, BF16, False), (NA_W, BF16, False), (NA_W, F32, True),
            (NA_W, F32, True), (NA_W, BF16, False), (NA_W, BF16, False)]
    return pl.pallas_call(
        _mixer_in_kernel,
        grid=(N_TILES,),
        in_specs=[tile(D_MODEL),
                  _resident((None, 8, 3 * D_MODEL), lambda i: (l, 0, 1)),
                  pl.BlockSpec((TM, LANE), lambda i: (_rope_block(i), 0)),
                  pl.BlockSpec((TM, LANE), lambda i: (_rope_block(i), 0)),
                  _resident((None, D_MODEL, U_W), lambda i: (l, 0, 0)),
                  _resident((None, 1, MLA_Q_LORA), lambda i: (l, 0, 0)),
                  _resident((None, MLA_Q_LORA, MLA_QW), lambda i: (l, 0, 0)),
                  _resident((None, 1, MLA_KV_LORA), lambda i: (l, 0, 0)),
                  _resident((None, MLA_KV_LORA, MLA_QW), lambda i: (l, 0, 0)),
                  _resident((None, MLA_KV_LORA, MLA_HEADS * MLA_V), lambda i: (l, 0, 0))],
        out_specs=[ctx_tile(w) if ctx_only else tile(w) for w, _, ctx_only in outs],
        out_shape=[jax.ShapeDtypeStruct((P_TOK if ctx_only else TOKENS, w), dt) for w, dt, ctx_only in outs],
        compiler_params=pltpu.CompilerParams(dimension_semantics=("arbitrary",),
                                             vmem_limit_bytes=VMEM_LIMIT),
        name="mixer_in",
    )(x, mod, cos_t, sin_t, w_in, q_norm, w_uq, kv_norm, w_uk, w_uv)


def _ctx_kv_kernel(ckv_ref, kr_ref, w_uk_ref, w_uv_ref, k_ref, v_ref):
    c = ckv_ref[...].astype(BF16)
    k = _dot(c, w_uk_ref[...])
    kr = kr_ref[...]
    for hd in range(MLA_HEADS):
        sl = slice(hd * MLA_HP, (hd + 1) * MLA_HP)
        k_ref[:, sl] = (k[:, sl] + kr).astype(BF16)
    v_ref[...] = _dot(c, w_uv_ref[...]).astype(BF16)


def _ctx_kv(cache_ckv, cache_kr_pad, w_uk, w_uv):
    return pl.pallas_call(
        _ctx_kv_kernel,
        grid=(DEC_BATCH, DEPTH),
        in_specs=[pl.BlockSpec((None, None, PAST_LEN, MLA_KV_LORA), lambda b, l: (b, l, 0, 0)),
                  pl.BlockSpec((None, None, PAST_LEN, LANE), lambda b, l: (b, l, 0, 0)),
                  pl.BlockSpec((None, MLA_KV_LORA, MLA_QW), lambda b, l: (l, 0, 0)),
                  pl.BlockSpec((None, MLA_KV_LORA, MLA_HEADS * MLA_V), lambda b, l: (l, 0, 0))],
        out_specs=[pl.BlockSpec((None, None, PAST_LEN, MLA_QW), lambda b, l: (b, l, 0, 0)),
                   pl.BlockSpec((None, None, PAST_LEN, MLA_HEADS * MLA_V), lambda b, l: (b, l, 0, 0))],
        out_shape=[jax.ShapeDtypeStruct((DEC_BATCH, DEPTH, PAST_LEN, MLA_QW), BF16),
                   jax.ShapeDtypeStruct((DEC_BATCH, DEPTH, PAST_LEN, MLA_HEADS * MLA_V), BF16)],
        name="ctx_kv",
    )(cache_ckv, cache_kr_pad, w_uk, w_uv)


def _fourier_kernel(x_ref, cs_ref, cl_ref, sl_ref, o_ref):
    x = x_ref[...].astype(BF16)
    xc, xs = [], []
    for g in range(F_GROUPS):
        t = _dot(x[:, g * F_GC:(g + 1) * F_GC], cs_ref[...])
        xc.append(t[:, :F_GC])
        xs.append(t[:, F_GC:])
    xc = jnp.concatenate(xc, axis=1).astype(BF16)
    xs = jnp.concatenate(xs, axis=1).astype(BF16)
    o_ref[...] = (_dot(cl_ref[...], xc) - _dot(sl_ref[...], xs)).astype(BF16)


def _dft_tables(n):
    k = np.arange(n, dtype=np.int64)
    ang = 2.0 * np.pi * ((k[:, None] * k[None, :]) % n).astype(np.float64) / n
    s = n ** -0.5
    return np.cos(ang) * s, np.sin(ang) * s


def _fourier_tables(length):
    cc, sc = _dft_tables(F_GC)
    cl, sl = _dft_tables(length)
    as_bf16 = lambda a: jnp.asarray(a, F32).astype(BF16)
    return as_bf16(np.concatenate([cc, sc], axis=1)), as_bf16(cl), as_bf16(sl)


def _fourier(u_f, tables, length, first_block, n_blocks):
    cs, cl, sl = tables
    return pl.pallas_call(
        _fourier_kernel,
        grid=(n_blocks,),
        in_specs=[pl.BlockSpec((length, F_W), lambda b: (first_block + b, 0)),
                  _resident((F_GC, 2 * F_GC), lambda b: (0, 0)),
                  _resident((length, length), lambda b: (0, 0)),
                  _resident((length, length), lambda b: (0, 0))],
        out_specs=pl.BlockSpec((length, F_W), lambda b: (b, 0)),
        out_shape=jax.ShapeDtypeStruct((n_blocks * length, F_W), BF16),
        compiler_params=pltpu.CompilerParams(dimension_semantics=("arbitrary",),
                                             vmem_limit_bytes=VMEM_LIMIT),
        name="fourier_%d" % length,
    )(u_f, cs, cl, sl)


def _softmax_pv(scores, values, scale):
    m = None
    for s in scores:
        sm = jnp.max(s, axis=-1, keepdims=True)
        m = sm if m is None else jnp.maximum(m, sm)
    den = None
    acc = None
    for s, v in zip(scores, values):
        p = jnp.exp((s - m) * scale)
        ps = jnp.sum(p, axis=-1, keepdims=True)
        den = ps if den is None else den + ps
        pv = _dot(p.astype(BF16), v)
        acc = pv if acc is None else acc + pv
    return acc / den


def _ctx_attn_kernel(q_ref, k_ref, v_ref, qn_ref, kn_ref, vn_ref, om_ref, on_ref):
    for hd in range(MLA_HEADS):
        sl = slice(hd * MLA_HP, (hd + 1) * MLA_HP)
        vs = slice(hd * MLA_V, (hd + 1) * MLA_V)
        s = _dot_nt(q_ref[:, sl], k_ref[:, sl])
        om_ref[:, vs] = _softmax_pv([s], [v_ref[:, vs]], MLA_SCALE).astype(BF16)
    for hd in range(NA_HEADS):
        sl = slice(hd * NA_HEAD_DIM, (hd + 1) * NA_HEAD_DIM)
        s = _dot_nt(qn_ref[:, sl], kn_ref[:, sl])
        on_ref[:, sl] = _softmax_pv([s], [vn_ref[:, sl]], NA_SCALE).astype(BF16)


def _ctx_attn(q, k, v, qn, kn, vn):
    blk = lambda w: pl.BlockSpec((SEQ, w), lambda b: (b, 0))
    return pl.pallas_call(
        _ctx_attn_kernel,
        grid=(BATCH,),
        in_specs=[blk(MLA_QW), blk(MLA_QW), blk(MLA_HEADS * MLA_V), blk(NA_W), blk(NA_W), blk(NA_W)],
        out_specs=[blk(MLA_HEADS * MLA_V), blk(NA_W)],
        out_shape=[jax.ShapeDtypeStruct((P_TOK, MLA_HEADS * MLA_V), BF16),
                   jax.ShapeDtypeStruct((P_TOK, NA_W), BF16)],
        compiler_params=pltpu.CompilerParams(dimension_semantics=("arbitrary",)),
        name="ctx_attn",
    )(q, k, v, qn, kn, vn)


def _lat_mla_kernel(q_ref, k_ref, v_ref, kc_ref, vc_ref, o_ref):
    for hd in range(MLA_HEADS):
        sl = slice(hd * MLA_HP, (hd + 1) * MLA_HP)
        vs = slice(hd * MLA_V, (hd + 1) * MLA_V)
        q = q_ref[:, sl]
        s_lat = _dot_nt(q, k_ref[:, sl])
        s_ctx = _dot_nt(q, kc_ref[:, sl])
        o_ref[:, vs] = _softmax_pv([s_lat, s_ctx], [v_ref[:, vs], vc_ref[:, vs]], MLA_SCALE).astype(BF16)


def _lat_mla(q, k, v, kc, vc, l):
    nq = DEC_SEQ // MLA_QB
    q0 = P_TOK // MLA_QB
    r0 = P_TOK // DEC_SEQ
    return pl.pallas_call(
        _lat_mla_kernel,
        grid=(DEC_BATCH, nq),
        in_specs=[pl.BlockSpec((MLA_QB, MLA_QW), lambda b, j: (q0 + b * nq + j, 0)),
                  pl.BlockSpec((DEC_SEQ, MLA_QW), lambda b, j: (r0 + b, 0)),
                  pl.BlockSpec((DEC_SEQ, MLA_HEADS * MLA_V), lambda b, j: (r0 + b, 0)),
                  pl.BlockSpec((None, None, PAST_LEN, MLA_QW), lambda b, j: (b, l, 0, 0)),
                  pl.BlockSpec((None, None, PAST_LEN, MLA_HEADS * MLA_V), lambda b, j: (b, l, 0, 0))],
        out_specs=pl.BlockSpec((MLA_QB, MLA_HEADS * MLA_V), lambda b, j: (b * nq + j, 0)),
        out_shape=jax.ShapeDtypeStruct((S_TOK, MLA_HEADS * MLA_V), BF16),
        compiler_params=pltpu.CompilerParams(dimension_semantics=("arbitrary", "arbitrary"),
                                             vmem_limit_bytes=VMEM_LIMIT),
        name="lat_mla",
    )(q, k, v, kc, vc)


def _na_window_block(j):
    return jnp.clip(j - 1, 0, NA_BLOCKS - NA_WIN_ROWS // NA_QROWS)


def _lat_na_kernel(q_ref, k_ref, v_ref, kc_ref, vc_ref, bias_ref, o_ref):
    j = pl.program_id(1)
    start = pl.multiple_of(_na_window_block(j) * NA_QB, NA_QB)
    k_win = k_ref[pl.ds(start, NA_WIN), :]
    v_win = v_ref[pl.ds(start, NA_WIN), :]
    k_ctx = kc_ref[...].astype(BF16)
    v_ctx = vc_ref[...].astype(BF16)
    for hd in range(NA_HEADS):
        sl = slice(hd * NA_HEAD_DIM, (hd + 1) * NA_HEAD_DIM)
        q = q_ref[:, sl]
        s_loc = _dot_nt(q, k_win[:, sl]) + bias_ref[hd]
        s_ctx = _dot_nt(q, k_ctx[:, sl])
        o_ref[:, sl] = _softmax_pv([s_loc, s_ctx], [v_win[:, sl], v_ctx[:, sl]], NA_SCALE).astype(BF16)


NA_BLOCK_KINDS = (0, 1, NA_BLOCKS - 1)
NA_DIAG = GRID_W - 1
NA_N_DR = 2 * NA_KH - 1


def _na_bias_kernel(e_ref, o_ref):
    lane = lax.broadcasted_iota(jnp.int32, (GRID_W, LANE), 1)
    q_col = lax.broadcasted_iota(jnp.int32, (GRID_W, LANE), 0)
    k_col = lane % GRID_W
    col_start = jnp.clip(q_col - NA_KW // 2, 0, GRID_W - NA_KW)
    col_ok = (k_col >= col_start) & (k_col < col_start + NA_KW)
    low_half = lane < GRID_W
    neg = jnp.full((GRID_W, LANE), NEG_INF, F32)
    tiles = {}

    def toeplitz(dr, half):
        if (dr, half) not in tiles:
            row = jnp.broadcast_to(e_ref[dr:dr + 1, :], (GRID_W, LANE))
            shift = (LANE - NA_DIAG + half * GRID_W) % LANE
            tiles[dr, half] = pltpu.roll(row, shift, 1, stride=1, stride_axis=0)
        return tiles[dr, half]

    for kind, blk in enumerate(NA_BLOCK_KINDS):
        first_key_row = NA_QROWS * min(max(blk - 1, 0), NA_BLOCKS - NA_WIN_ROWS // NA_QROWS)
        for qr in range(NA_QROWS):
            q_row = NA_QROWS * blk + qr
            row_start = min(max(q_row - NA_KH // 2, 0), GRID_H - NA_KH)
            for pair in range(NA_WIN_ROWS // 2):
                halves = []
                for half in range(2):
                    k_row = first_key_row + 2 * pair + half
                    inside = row_start <= k_row < row_start + NA_KH
                    halves.append(toeplitz(k_row - q_row + NA_KH - 1, half) if inside else neg)
                tile = jnp.where(low_half, halves[0], halves[1])
                o_ref[kind, qr * GRID_W:(qr + 1) * GRID_W, pair * LANE:(pair + 1) * LANE] = (
                    jnp.where(col_ok, tile, NEG_INF))


def _na_bias_table(rpb):
    left = NA_DIAG - (NA_KW - 1)
    e = jnp.concatenate([jnp.broadcast_to(rpb[..., :1], rpb.shape[:-1] + (left,)), rpb,
                         jnp.broadcast_to(rpb[..., -1:], rpb.shape[:-1] + (LANE - left - rpb.shape[-1],))],
                        axis=-1) * (1.0 / NA_SCALE)
    return pl.pallas_call(
        _na_bias_kernel,
        grid=(DEPTH, NA_HEADS),
        in_specs=[pl.BlockSpec((None, None, NA_N_DR, LANE), lambda l, h: (l, h, 0, 0))],
        out_specs=pl.BlockSpec((None, len(NA_BLOCK_KINDS), None, NA_QB, NA_WIN), lambda l, h: (l, 0, h, 0, 0)),
        out_shape=jax.ShapeDtypeStruct((DEPTH, len(NA_BLOCK_KINDS), NA_HEADS, NA_QB, NA_WIN), F32),
        name="na_bias",
    )(e)


def _lat_na(q, k, v, cache_k, cache_v, bias, l):
    q0 = P_TOK // NA_QB
    r0 = P_TOK // DEC_SEQ
    kind = lambda j: (j > 0).astype(jnp.int32) + (j == NA_BLOCKS - 1).astype(jnp.int32)
    return pl.pallas_call(
        _lat_na_kernel,
        grid=(DEC_BATCH, NA_BLOCKS),
        in_specs=[pl.BlockSpec((NA_QB, NA_W), lambda b, j: (q0 + b * NA_BLOCKS + j, 0)),
                  pl.BlockSpec((DEC_SEQ, NA_W), lambda b, j: (r0 + b, 0)),
                  pl.BlockSpec((DEC_SEQ, NA_W), lambda b, j: (r0 + b, 0)),
                  pl.BlockSpec((None, None, PAST_LEN, NA_W), lambda b, j: (b, l, 0, 0)),
                  pl.BlockSpec((None, None, PAST_LEN, NA_W), lambda b, j: (b, l, 0, 0)),
                  pl.BlockSpec((None, None, NA_HEADS, NA_QB, NA_WIN), lambda b, j: (l, kind(j), 0, 0, 0))],
        out_specs=pl.BlockSpec((NA_QB, NA_W), lambda b, j: (b * NA_BLOCKS + j, 0)),
        out_shape=jax.ShapeDtypeStruct((S_TOK, NA_W), BF16),
        compiler_params=pltpu.CompilerParams(dimension_semantics=("arbitrary", "arbitrary"),
                                             vmem_limit_bytes=VMEM_LIMIT),
        name="lat_na",
    )(q, k, v, cache_k, cache_v, bias)


def _merge_kernel(x_ref, mod_ref, yfc_ref, ymc_ref, ync_ref, yfl_ref, yml_ref, ynl_ref,
                  wg_ref, bg_ref, wf_ref, wm_ref, wn_ref, wo_ref, g_ref, b_ref, o_ref):
    i = pl.program_id(0)
    shift, scale, gate = _mod_rows(mod_ref, i)
    x = x_ref[...]
    h = (_ln(x, 1e-6) * (1.0 + scale) + shift).astype(BF16)
    is_ctx = i < P_TILES
    mix = None
    for n, (yc_ref, yl_ref, w_ref) in enumerate(((yfc_ref, yfl_ref, wf_ref), (ymc_ref, yml_ref, wm_ref),
                                                 (ync_ref, ynl_ref, wn_ref))):
        sl = slice(n * D_MODEL, (n + 1) * D_MODEL)
        g = jax.nn.sigmoid(_dot(h, wg_ref[:, sl]) + bg_ref[:, sl])
        y = jnp.where(is_ctx, yc_ref[...], yl_ref[...])
        t = g * _dot(y, w_ref[...])
        mix = t if mix is None else mix + t
    z = ALPHA * x + gate * _dot(mix.astype(BF16), wo_ref[...])
    o_ref[...] = _ln(z, 1e-5) * g_ref[...] + b_ref[...]


def _merge(x, mod, y_ctx, y_lat, w_gate, b_gate, w_f, w_m, w_n, w_out, ln_g, ln_b, l):
    tile = lambda w: pl.BlockSpec((TM, w), lambda i: (i, 0))
    ctx_tile = lambda w: pl.BlockSpec((TM, w), lambda i: (jnp.minimum(i, P_TILES - 1), 0))
    lat_tile = lambda w: pl.BlockSpec((TM, w), lambda i: (jnp.maximum(i - P_TILES, 0), 0))
    widths = (F_W, MLA_HEADS * MLA_V, NA_W)
    return pl.pallas_call(
        _merge_kernel,
        grid=(N_TILES,),
        in_specs=[tile(D_MODEL),
                  _resident((None, 8, 3 * D_MODEL), lambda i: (l, 0, 1))]
                 + [ctx_tile(w) for w in widths] + [lat_tile(w) for w in widths] + [
                  _resident((None, D_MODEL, 3 * D_MODEL), lambda i: (l, 0, 0)),
                  _resident((None, 1, 3 * D_MODEL), lambda i: (l, 0, 0)),
                  _resident((None, F_W, D_MODEL), lambda i: (l, 0, 0)),
                  _resident((None, MLA_HEADS * MLA_V, D_MODEL), lambda i: (l, 0, 0)),
                  _resident((None, NA_W, D_MODEL), lambda i: (l, 0, 0)),
                  _resident((None, D_MODEL, D_MODEL), lambda i: (l, 0, 0)),
                  _resident((None, 1, D_MODEL), lambda i: (3 * l + 1, 0, 0)),
                  _resident((None, 1, D_MODEL), lambda i: (3 * l + 1, 0, 0))],
        out_specs=tile(D_MODEL),
        out_shape=jax.ShapeDtypeStruct((TOKENS, D_MODEL), F32),
        compiler_params=pltpu.CompilerParams(dimension_semantics=("arbitrary",),
                                             vmem_limit_bytes=VMEM_LIMIT),
        name="merge",
    )(x, mod, *y_ctx, *y_lat, w_gate, b_gate, w_f, w_m, w_n, w_out, ln_g, ln_b)


def _pad_heads(w, n_heads, width):
    lead = w.shape[:-1]
    w = w.reshape(lead + (n_heads, width))
    w = jnp.pad(w, [(0, 0)] * len(lead) + [(0, 0), (0, MLA_HP - width)])
    return w.reshape(lead + (n_heads * MLA_HP,))


def _rope_tables():
    t = jnp.arange(DEC_SEQ, dtype=jnp.int32)
    pos = jnp.stack([t // GRID_W, t % GRID_W], axis=-1).astype(F32)
    half = AXIS_DIM // 2
    inv_freq = ROPE_BASE ** (-jnp.arange(half, dtype=F32) / half)
    ang = pos[:, :, None] * inv_freq
    ang = jnp.concatenate([ang, ang], axis=-1).reshape(DEC_SEQ, MLA_ROPE)
    pad = lambda a, fill: jnp.pad(a, ((0, 0), (ROPE_LANE0, LANE - ROPE_LANE0 - MLA_ROPE)), constant_values=fill)
    cos = jnp.concatenate([jnp.ones((TM, LANE), F32), pad(jnp.cos(ang), 1.0)], axis=0)
    sin = jnp.concatenate([jnp.zeros((TM, LANE), F32), pad(jnp.sin(ang), 0.0)], axis=0)
    return cos, sin


def kernel(x_prompt, x_sample, cache_mla_ckv, cache_mla_krope, cache_na_k, cache_na_v, c, c_ctx, w_ada, b_ada, ffn1_w1, ffn1_w3, ffn1_w2, ffn2_w1, ffn2_w3, ffn2_w2, w_in, mla_q_norm, mla_w_uq, mla_kv_norm, mla_w_ukv, na_rpb, w_branch_f, w_branch_m, w_branch_n, w_gate, b_gate, w_out, ln_g, ln_b):
    bf = lambda w: w.astype(BF16)
    f1 = (bf(ffn1_w1), bf(ffn1_w3), bf(ffn1_w2))
    f2 = (bf(ffn2_w1), bf(ffn2_w3), bf(ffn2_w2))
    w_in_p = bf(jnp.concatenate(
        [w_in[..., :U_KR0],
         jnp.pad(w_in[..., U_KR0:U_KR0 + MLA_ROPE], ((0, 0), (0, 0), (ROPE_LANE0, LANE - ROPE_LANE0 - MLA_ROPE))),
         w_in[..., U_KR0 + MLA_ROPE:]], axis=-1))
    w_uq_p = bf(_pad_heads(mla_w_uq, MLA_HEADS, MLA_NOPE + MLA_ROPE))
    ukv = mla_w_ukv.reshape(DEPTH, MLA_KV_LORA, MLA_HEADS, MLA_NOPE + MLA_V)
    w_uk_p = bf(_pad_heads(ukv[..., :MLA_NOPE].reshape(DEPTH, MLA_KV_LORA, MLA_HEADS * MLA_NOPE), MLA_HEADS, MLA_NOPE))
    w_uv = bf(ukv[..., MLA_NOPE:].reshape(DEPTH, MLA_KV_LORA, MLA_HEADS * MLA_V))
    w_gate_b, w_f, w_m, w_n, w_out_b = bf(w_gate), bf(w_branch_f), bf(w_branch_m), bf(w_branch_n), bf(w_out)
    q_norm = mla_q_norm.reshape(DEPTH, 1, MLA_Q_LORA)
    kv_norm = mla_kv_norm.reshape(DEPTH, 1, MLA_KV_LORA)
    b_gate3 = b_gate.reshape(DEPTH, 1, 3 * D_MODEL)
    g3 = ln_g.reshape(DEPTH * 3, 1, D_MODEL)
    b3 = ln_b.reshape(DEPTH * 3, 1, D_MODEL)
    cos_t, sin_t = _rope_tables()
    dft_ctx = _fourier_tables(SEQ)
    dft_lat = _fourier_tables(DEC_SEQ)
    kr_pad = jnp.pad(cache_mla_krope, ((0, 0), (0, 0), (0, 0), (ROPE_LANE0, LANE - ROPE_LANE0 - MLA_ROPE)))
    cache_k = cache_na_k.reshape(DEC_BATCH, DEPTH, PAST_LEN, NA_W)
    cache_v = cache_na_v.reshape(DEC_BATCH, DEPTH, PAST_LEN, NA_W)

    cvec = jnp.concatenate([c_ctx[None], c, jnp.zeros((8 - 1 - DEC_BATCH, D_MODEL), F32)], axis=0)
    mod = _adaln(cvec, w_ada, b_ada)
    kc, vc = _ctx_kv(cache_mla_ckv, kr_pad, w_uk_p, w_uv)
    na_bias = _na_bias_table(na_rpb)

    x = jnp.concatenate([x_prompt.reshape(P_TOK, D_MODEL), x_sample.reshape(S_TOK, D_MODEL)], axis=0)
    ckv_l, kr_l, nak_l, nav_l = [], [], [], []
    for l in range(DEPTH):
        x = _ffn(x, mod, *f1, g3, b3, l, 0)
        u_f, q, c_kv, k_r, k, v, q_n, k_n, v_n, k_nb, v_nb = _mixer_in(
            x, mod, cos_t, sin_t, w_in_p, q_norm, w_uq_p, kv_norm, w_uk_p, w_uv, l)
        yf_c = _fourier(u_f, dft_ctx, SEQ, 0, BATCH)
        yf_l = _fourier(u_f, dft_lat, DEC_SEQ, P_TOK // DEC_SEQ, DEC_BATCH)
        ym_c, yn_c = _ctx_attn(q, k, v, q_n, k_nb, v_nb)
        ym_l = _lat_mla(q, k, v, kc, vc, l)
        yn_l = _lat_na(q_n, k_nb, v_nb, cache_k, cache_v, na_bias, l)
        x = _merge(x, mod, (yf_c, ym_c, yn_c), (yf_l, ym_l, yn_l), w_gate_b, b_gate3, w_f, w_m, w_n, w_out_b,
                   g3, b3, l)
        x = _ffn(x, mod, *f2, g3, b3, l, 2)
        ckv_l.append(c_kv.reshape(BATCH, SEQ, MLA_KV_LORA))
        kr_l.append(k_r.reshape(BATCH, SEQ, MLA_ROPE))
        nak_l.append(k_n.reshape(BATCH, SEQ, NA_HEADS, NA_HEAD_DIM))
        nav_l.append(v_n.reshape(BATCH, SEQ, NA_HEADS, NA_HEAD_DIM))
    return (x[:P_TOK].reshape(BATCH, SEQ, D_MODEL), x[P_TOK:].reshape(DEC_BATCH, DEC_SEQ, D_MODEL),
            jnp.stack(ckv_l, axis=1), jnp.stack(kr_l, axis=1), jnp.stack(nak_l, axis=1), jnp.stack(nav_l, axis=1))
```

```python
import functools

import numpy as np
import jax
import jax.numpy as jnp
from jax import lax
from jax.experimental import pallas as pl
from jax.experimental.pallas import tpu as pltpu

F32 = jnp.float32
BF16 = jnp.bfloat16

D_MODEL = 1024
BATCH = 32
SEQ = 256
DEPTH = 4
DEC_BATCH = 2
DEC_SEQ = 2048
PAST_LEN = 512
GRID_W = 64
GRID_H = DEC_SEQ // GRID_W
D_FF = 2816
F_GROUPS = 4
F_GC = 128
F_W = F_GROUPS * F_GC
MLA_HEADS = 8
MLA_Q_LORA = 384
MLA_KV_LORA = 256
MLA_NOPE = 64
MLA_ROPE = 32
MLA_V = 64
NA_HEADS = 8
NA_HEAD_DIM = 64
NA_KH = 8
NA_KW = 16
NA_W = NA_HEADS * NA_HEAD_DIM
ROPE_BASE = 10000.0
AXIS_DIM = MLA_ROPE // 2
ALPHA = (2.0 * DEPTH) ** 0.25
MLA_SCALE = (MLA_NOPE + MLA_ROPE) ** -0.5
NA_SCALE = NA_HEAD_DIM ** -0.5
NEG_INF = -1e30

LANE = 128
MLA_HP = LANE
MLA_QW = MLA_HEADS * MLA_HP
P_TOK = BATCH * SEQ
S_TOK = DEC_BATCH * DEC_SEQ
TOKENS = P_TOK + S_TOK
TM = 512
N_TILES = TOKENS // TM
P_TILES = P_TOK // TM
S_TILES_PER_REQ = DEC_SEQ // TM
MXU_TILE = 256
FF_CHUNKS = (0, 4 * MXU_TILE, 8 * MXU_TILE, D_FF)
U_Q0 = F_W
U_KV0 = U_Q0 + MLA_Q_LORA
U_KR0 = U_KV0 + MLA_KV_LORA
U_NA0 = U_KR0 + LANE
U_W = U_NA0 + 3 * NA_W
ROPE_LANE0 = MLA_NOPE
NA_QROWS = 4
NA_QB = NA_QROWS * GRID_W
NA_WIN_ROWS = NA_QROWS + NA_KH
NA_WIN = NA_WIN_ROWS * GRID_W
NA_BLOCKS = GRID_H // NA_QROWS
MLA_QB = 256
VMEM_LIMIT = 56 * 1024 * 1024


def _group(i):
    return jnp.where(i < P_TILES, 0, 1 + (i - P_TILES) // S_TILES_PER_REQ)


def _rope_block(i):
    return jnp.where(i < P_TILES, 0, 1 + (i - P_TILES) % S_TILES_PER_REQ)


def _ln(x, eps):
    mu = jnp.mean(x, axis=-1, keepdims=True)
    xc = x - mu
    var = jnp.mean(xc * xc, axis=-1, keepdims=True)
    return xc * lax.rsqrt(var + eps)


def _rms(x, g):
    return x * lax.rsqrt(jnp.mean(x * x, axis=-1, keepdims=True) + 1e-6) * g


def _dot(a, b):
    return jnp.dot(a, b, preferred_element_type=F32)


def _dot_nt(a, b):
    return lax.dot_general(a, b, (((1,), (1,)), ((), ())), preferred_element_type=F32)


def _mod_rows(mod_ref, i):
    m = mod_ref[pl.ds(_group(i), 1), :]
    return m[:, :D_MODEL], m[:, D_MODEL:2 * D_MODEL], m[:, 2 * D_MODEL:]


def _rope(x, cos, sin):
    lane = lax.broadcasted_iota(jnp.int32, x.shape, 1)
    first_half = (lane % AXIS_DIM) < (AXIS_DIM // 2)
    rot = jnp.where(first_half, -pltpu.roll(x, LANE - AXIS_DIM // 2, 1), pltpu.roll(x, AXIS_DIM // 2, 1))
    return x * cos + rot * sin


def _adaln_kernel(c_ref, w_ref, b_ref, o_ref):
    c = c_ref[...]
    s = (c * jax.nn.sigmoid(c)).astype(BF16)
    o_ref[...] = _dot(s, w_ref[...].astype(BF16)) + b_ref[...]


def _adaln(cvec, w_ada, b_ada):
    n_col = 9 * D_MODEL // D_MODEL
    return pl.pallas_call(
        _adaln_kernel,
        grid=(DEPTH, n_col),
        in_specs=[pl.BlockSpec((8, D_MODEL), lambda l, j: (0, 0)),
                  pl.BlockSpec((None, D_MODEL, D_MODEL), lambda l, j: (l, 0, j)),
                  pl.BlockSpec((None, 1, D_MODEL), lambda l, j: (l, 0, j))],
        out_specs=pl.BlockSpec((None, 8, D_MODEL), lambda l, j: (l, 0, j)),
        out_shape=jax.ShapeDtypeStruct((DEPTH, 8, 9 * D_MODEL), F32),
        name="adaln",
    )(cvec, w_ada, b_ada.reshape(DEPTH, 1, 9 * D_MODEL))


def _ffn_kernel(x_ref, mod_ref, w1_ref, w3_ref, w2_ref, g_ref, b_ref, o_ref):
    i = pl.program_id(0)
    shift, scale, gate = _mod_rows(mod_ref, i)
    x = x_ref[...]
    h = (_ln(x, 1e-6) * (1.0 + scale) + shift).astype(BF16)
    y = None
    for c0, c1 in zip(FF_CHUNKS[:-1], FF_CHUNKS[1:]):
        sl = slice(c0, c1)
        a = _dot(h, w1_ref[:, sl])
        t = (a * jax.nn.sigmoid(a) * _dot(h, w3_ref[:, sl])).astype(BF16)
        yc = _dot(t, w2_ref[sl, :])
        y = yc if y is None else y + yc
    z = ALPHA * x + (0.5 * gate) * y
    o_ref[...] = _ln(z, 1e-5) * g_ref[...] + b_ref[...]


def _resident(shape, index_map):
    return pl.BlockSpec(shape, index_map, pipeline_mode=pl.Buffered(1))


def _ffn(x, mod, w1, w3, w2, ln_g, ln_b, l, sub):
    return pl.pallas_call(
        _ffn_kernel,
        grid=(N_TILES,),
        in_specs=[pl.BlockSpec((TM, D_MODEL), lambda i: (i, 0)),
                  _resident((None, 8, 3 * D_MODEL), lambda i: (l, 0, sub)),
                  _resident((None, D_MODEL, D_FF), lambda i: (l, 0, 0)),
                  _resident((None, D_MODEL, D_FF), lambda i: (l, 0, 0)),
                  _resident((None, D_FF, D_MODEL), lambda i: (l, 0, 0)),
                  _resident((None, 1, D_MODEL), lambda i: (3 * l + sub, 0, 0)),
                  _resident((None, 1, D_MODEL), lambda i: (3 * l + sub, 0, 0))],
        out_specs=pl.BlockSpec((TM, D_MODEL), lambda i: (i, 0)),
        out_shape=jax.ShapeDtypeStruct((TOKENS, D_MODEL), F32),
        compiler_params=pltpu.CompilerParams(dimension_semantics=("arbitrary",),
                                             vmem_limit_bytes=VMEM_LIMIT),
        name="ffn",
    )(x, mod, w1, w3, w2, ln_g, ln_b)


def _mixer_in_kernel(x_ref, mod_ref, cos_ref, sin_ref, w_in_ref, qn_ref, w_uq_ref, kvn_ref, w_uk_ref, w_uv_ref,
                     ckv_prev, kr_prev, kna_prev, vna_prev,
                     uf_ref, q_ref, k_ref, v_ref, qna_ref, knab_ref, vnab_ref, ckv_ref, kr_ref, kna_ref, vna_ref):
    del ckv_prev, kr_prev, kna_prev, vna_prev
    i = pl.program_id(0)
    shift, scale, _ = _mod_rows(mod_ref, i)
    h = (_ln(x_ref[...], 1e-6) * (1.0 + scale) + shift).astype(BF16)
    cos = cos_ref[...]
    sin = sin_ref[...]

    uf_ref[...] = _dot(h, w_in_ref[:, :U_Q0])

    u_q = _dot(h, w_in_ref[:, U_Q0:U_KV0])
    q = _dot(_rms(u_q, qn_ref[...]).astype(BF16), w_uq_ref[...])
    for hd in range(MLA_HEADS):
        sl = slice(hd * MLA_HP, (hd + 1) * MLA_HP)
        q_ref[:, sl] = _rope(q[:, sl], cos, sin).astype(BF16)

    c_kv = _rms(_dot(h, w_in_ref[:, U_KV0:U_KR0]), kvn_ref[...])
    kr = _dot(h, w_in_ref[:, U_KR0:U_NA0])
    k_na = _dot(h, w_in_ref[:, U_NA0 + NA_W:U_NA0 + 2 * NA_W])
    v_na = _dot(h, w_in_ref[:, U_NA0 + 2 * NA_W:])

    @pl.when(i < P_TILES)
    def _():
        kr_t = kr.T[ROPE_LANE0:ROPE_LANE0 + MLA_ROPE, :]
        kna_t = k_na.T
        vna_t = v_na.T
        for b in range(TM // SEQ):
            rows = slice(b * SEQ, (b + 1) * SEQ)
            ckv_ref[b] = c_kv[rows, :]
            kr_ref[b] = kr_t[:, rows]
            kna_ref[b] = kna_t[:, rows]
            vna_ref[b] = vna_t[:, rows]

    c_kv = c_kv.astype(BF16)
    kr = _rope(kr, cos, sin)
    k = _dot(c_kv, w_uk_ref[...])
    for hd in range(MLA_HEADS):
        sl = slice(hd * MLA_HP, (hd + 1) * MLA_HP)
        k_ref[:, sl] = (k[:, sl] + kr).astype(BF16)
    v_ref[...] = _dot(c_kv, w_uv_ref[...]).astype(BF16)

    qna_ref[...] = _dot(h, w_in_ref[:, U_NA0:U_NA0 + NA_W]).astype(BF16)
    knab_ref[...] = k_na.astype(BF16)
    vnab_ref[...] = v_na.astype(BF16)


def _cache_shapes():
    return [(BATCH, DEPTH, SEQ, MLA_KV_LORA), (BATCH, DEPTH, MLA_ROPE, SEQ),
            (BATCH, DEPTH, NA_W, SEQ), (BATCH, DEPTH, NA_W, SEQ)]


def _mixer_in(x, mod, cos_t, sin_t, w_in, q_norm, w_uq, kv_norm, w_uk, w_uv, caches, l):
    tile = lambda w: pl.BlockSpec((TM, w), lambda i: (i, 0))
    acts = [(F_W, F32), (MLA_QW, BF16), (MLA_QW, BF16), (MLA_HEADS * MLA_V, BF16), (NA_W, BF16), (NA_W, BF16),
            (NA_W, BF16)]
    cache_spec = lambda s: pl.BlockSpec((TM // SEQ, None) + s[2:],
                                        lambda i: (jnp.minimum(i, P_TILES - 1), l, 0, 0))
    n_in = 10
    return pl.pallas_call(
        _mixer_in_kernel,
        grid=(N_TILES,),
        in_specs=[tile(D_MODEL),
                  _resident((None, 8, 3 * D_MODEL), lambda i: (l, 0, 1)),
                  pl.BlockSpec((TM, LANE), lambda i: (_rope_block(i), 0)),
                  pl.BlockSpec((TM, LANE), lambda i: (_rope_block(i), 0)),
                  _resident((None, D_MODEL, U_W), lambda i: (l, 0, 0)),
                  _resident((None, 1, MLA_Q_LORA), lambda i: (l, 0, 0)),
                  _resident((None, MLA_Q_LORA, MLA_QW), lambda i: (l, 0, 0)),
                  _resident((None, 1, MLA_KV_LORA), lambda i: (l, 0, 0)),
                  _resident((None, MLA_KV_LORA, MLA_QW), lambda i: (l, 0, 0)),
                  _resident((None, MLA_KV_LORA, MLA_HEADS * MLA_V), lambda i: (l, 0, 0))]
                 + [pl.BlockSpec(memory_space=pl.ANY)] * len(caches),
        out_specs=[tile(w) for w, _ in acts] + [cache_spec(s) for s in _cache_shapes()],
        out_shape=[jax.ShapeDtypeStruct((TOKENS, w), dt) for w, dt in acts]
                  + [jax.ShapeDtypeStruct(s, F32) for s in _cache_shapes()],
        input_output_aliases={n_in + n: len(acts) + n for n in range(len(caches))},
        compiler_params=pltpu.CompilerParams(dimension_semantics=("arbitrary",),
                                             vmem_limit_bytes=VMEM_LIMIT),
        name="mixer_in",
    )(x, mod, cos_t, sin_t, w_in, q_norm, w_uq, kv_norm, w_uk, w_uv, *caches)


def _ctx_kv_kernel(ckv_ref, kr_ref, w_uk_ref, w_uv_ref, k_ref, v_ref):
    c = ckv_ref[...].astype(BF16)
    k = _dot(c, w_uk_ref[...])
    kr = kr_ref[...]
    for hd in range(MLA_HEADS):
        sl = slice(hd * MLA_HP, (hd + 1) * MLA_HP)
        k_ref[:, sl] = (k[:, sl] + kr).astype(BF16)
    v_ref[...] = _dot(c, w_uv_ref[...]).astype(BF16)


def _ctx_kv(cache_ckv, cache_kr_pad, w_uk, w_uv):
    return pl.pallas_call(
        _ctx_kv_kernel,
        grid=(DEC_BATCH, DEPTH),
        in_specs=[pl.BlockSpec((None, None, PAST_LEN, MLA_KV_LORA), lambda b, l: (b, l, 0, 0)),
                  pl.BlockSpec((None, None, PAST_LEN, LANE), lambda b, l: (b, l, 0, 0)),
                  pl.BlockSpec((None, MLA_KV_LORA, MLA_QW), lambda b, l: (l, 0, 0)),
                  pl.BlockSpec((None, MLA_KV_LORA, MLA_HEADS * MLA_V), lambda b, l: (l, 0, 0))],
        out_specs=[pl.BlockSpec((None, None, PAST_LEN, MLA_QW), lambda b, l: (b, l, 0, 0)),
                   pl.BlockSpec((None, None, PAST_LEN, MLA_HEADS * MLA_V), lambda b, l: (b, l, 0, 0))],
        out_shape=[jax.ShapeDtypeStruct((DEC_BATCH, DEPTH, PAST_LEN, MLA_QW), BF16),
                   jax.ShapeDtypeStruct((DEC_BATCH, DEPTH, PAST_LEN, MLA_HEADS * MLA_V), BF16)],
        name="ctx_kv",
    )(cache_ckv, cache_kr_pad, w_uk, w_uv)


def _fourier_kernel(x_ref, cs_ref, cl_ref, sl_ref, o_ref):
    x = x_ref[...].astype(BF16)
    xc, xs = [], []
    for g in range(F_GROUPS):
        t = _dot(x[:, g * F_GC:(g + 1) * F_GC], cs_ref[...])
        xc.append(t[:, :F_GC])
        xs.append(t[:, F_GC:])
    xc = jnp.concatenate(xc, axis=1).astype(BF16)
    xs = jnp.concatenate(xs, axis=1).astype(BF16)
    o_ref[...] = (_dot(cl_ref[...], xc) - _dot(sl_ref[...], xs)).astype(BF16)


def _dft_tables(n):
    k = np.arange(n, dtype=np.int64)
    ang = 2.0 * np.pi * ((k[:, None] * k[None, :]) % n).astype(np.float64) / n
    s = n ** -0.5
    return np.cos(ang) * s, np.sin(ang) * s


def _fourier_tables(length):
    cc, sc = _dft_tables(F_GC)
    cl, sl = _dft_tables(length)
    as_bf16 = lambda a: jnp.asarray(a, F32).astype(BF16)
    return as_bf16(np.concatenate([cc, sc], axis=1)), as_bf16(cl), as_bf16(sl)


def _fourier(u_f, tables, length, first_block, n_blocks):
    cs, cl, sl = tables
    return pl.pallas_call(
        _fourier_kernel,
        grid=(n_blocks,),
        in_specs=[pl.BlockSpec((length, F_W), lambda b: (first_block + b, 0)),
                  _resident((F_GC, 2 * F_GC), lambda b: (0, 0)),
                  _resident((length, length), lambda b: (0, 0)),
                  _resident((length, length), lambda b: (0, 0))],
        out_specs=pl.BlockSpec((length, F_W), lambda b: (b, 0)),
        out_shape=jax.ShapeDtypeStruct((n_blocks * length, F_W), BF16),
        compiler_params=pltpu.CompilerParams(dimension_semantics=("arbitrary",),
                                             vmem_limit_bytes=VMEM_LIMIT),
        name="fourier_%d" % length,
    )(u_f, cs, cl, sl)


LOG2E = 1.4426950408889634


def _softmax_pv(scores, values_ext, scale):
    m = None
    for s in scores:
        sm = jnp.max(s, axis=-1, keepdims=True)
        m = sm if m is None else jnp.maximum(m, sm)
    acc = None
    for s, v in zip(scores, values_ext):
        p = jnp.exp2((s - m) * (scale * LOG2E)).astype(BF16)
        pv = _dot(p, v)
        acc = pv if acc is None else acc + pv
    return acc[:, :LANE] / acc[:, LANE:]


def _with_ones(v_pair):
    return jnp.concatenate([v_pair, jnp.ones_like(v_pair)], axis=1)


def _half_masks(rows):
    low = lax.broadcasted_iota(jnp.int32, (rows, LANE), 1) < LANE // 2
    return low, jnp.logical_not(low)


def _head_of_pair(x_pair, mask):
    return jnp.where(mask, x_pair, jnp.zeros_like(x_pair))


def _ctx_attn_kernel(q_ref, k_ref, v_ref, qn_ref, kn_ref, vn_ref, om_ref, on_ref):
    masks = _half_masks(SEQ)
    for pair in range(MLA_HEADS // 2):
        ps = slice(pair * LANE, (pair + 1) * LANE)
        v_ext = _with_ones(v_ref[:, ps])
        o = []
        for half in range(2):
            sl = slice((2 * pair + half) * MLA_HP, (2 * pair + half + 1) * MLA_HP)
            s = _dot_nt(q_ref[:, sl], k_ref[:, sl])
            o.append(_softmax_pv([s], [v_ext], MLA_SCALE))
        om_ref[:, ps] = jnp.where(masks[0], o[0], o[1]).astype(BF16)
    for pair in range(NA_HEADS // 2):
        ps = slice(pair * LANE, (pair + 1) * LANE)
        v_ext = _with_ones(vn_ref[:, ps])
        q_pair = qn_ref[:, ps]
        k_pair = kn_ref[:, ps]
        o = []
        for half in range(2):
            s = _dot_nt(_head_of_pair(q_pair, masks[half]), k_pair)
            o.append(_softmax_pv([s], [v_ext], NA_SCALE))
        on_ref[:, ps] = jnp.where(masks[0], o[0], o[1]).astype(BF16)


def _ctx_attn(q, k, v, qn, kn, vn):
    blk = lambda w: pl.BlockSpec((SEQ, w), lambda b: (b, 0))
    return pl.pallas_call(
        _ctx_attn_kernel,
        grid=(BATCH,),
        in_specs=[blk(MLA_QW), blk(MLA_QW), blk(MLA_HEADS * MLA_V), blk(NA_W), blk(NA_W), blk(NA_W)],
        out_specs=[blk(MLA_HEADS * MLA_V), blk(NA_W)],
        out_shape=[jax.ShapeDtypeStruct((P_TOK, MLA_HEADS * MLA_V), BF16),
                   jax.ShapeDtypeStruct((P_TOK, NA_W), BF16)],
        compiler_params=pltpu.CompilerParams(dimension_semantics=("arbitrary",)),
        name="ctx_attn",
    )(q, k, v, qn, kn, vn)


def _lat_mla_kernel(q_ref, k_ref, v_ref, kc_ref, vc_ref, o_ref):
    low, _ = _half_masks(MLA_QB)
    for pair in range(MLA_HEADS // 2):
        ps = slice(pair * LANE, (pair + 1) * LANE)
        v_ext = [_with_ones(v_ref[:, ps]), _with_ones(vc_ref[:, ps])]
        o = []
        for half in range(2):
            sl = slice((2 * pair + half) * MLA_HP, (2 * pair + half + 1) * MLA_HP)
            q = q_ref[:, sl]
            o.append(_softmax_pv([_dot_nt(q, k_ref[:, sl]), _dot_nt(q, kc_ref[:, sl])], v_ext, MLA_SCALE))
        o_ref[:, ps] = jnp.where(low, o[0], o[1]).astype(BF16)


def _lat_mla(q, k, v, kc, vc, l):
    nq = DEC_SEQ // MLA_QB
    q0 = P_TOK // MLA_QB
    r0 = P_TOK // DEC_SEQ
    return pl.pallas_call(
        _lat_mla_kernel,
        grid=(DEC_BATCH, nq),
        in_specs=[pl.BlockSpec((MLA_QB, MLA_QW), lambda b, j: (q0 + b * nq + j, 0)),
                  pl.BlockSpec((DEC_SEQ, MLA_QW), lambda b, j: (r0 + b, 0)),
                  pl.BlockSpec((DEC_SEQ, MLA_HEADS * MLA_V), lambda b, j: (r0 + b, 0)),
                  pl.BlockSpec((None, None, PAST_LEN, MLA_QW), lambda b, j: (b, l, 0, 0)),
                  pl.BlockSpec((None, None, PAST_LEN, MLA_HEADS * MLA_V), lambda b, j: (b, l, 0, 0))],
        out_specs=pl.BlockSpec((MLA_QB, MLA_HEADS * MLA_V), lambda b, j: (b * nq + j, 0)),
        out_shape=jax.ShapeDtypeStruct((S_TOK, MLA_HEADS * MLA_V), BF16),
        compiler_params=pltpu.CompilerParams(dimension_semantics=("arbitrary", "arbitrary"),
                                             vmem_limit_bytes=VMEM_LIMIT),
        name="lat_mla",
    )(q, k, v, kc, vc)


def _na_window_block(j):
    return jnp.clip(j - 1, 0, NA_BLOCKS - NA_WIN_ROWS // NA_QROWS)


def _lat_na_kernel(q_ref, k_ref, v_ref, kc_ref, vc_ref, bias_ref, o_ref):
    j = pl.program_id(1)
    start = pl.multiple_of(_na_window_block(j) * NA_QB, NA_QB)
    k_win = k_ref[pl.ds(start, NA_WIN), :]
    v_win = v_ref[pl.ds(start, NA_WIN), :]
    masks = _half_masks(NA_QB)
    for pair in range(NA_HEADS // 2):
        ps = slice(pair * LANE, (pair + 1) * LANE)
        k_loc = k_win[:, ps]
        k_ctx = kc_ref[:, ps].astype(BF16)
        v_ext = [_with_ones(v_win[:, ps]), _with_ones(vc_ref[:, ps].astype(BF16))]
        q_pair = q_ref[:, ps]
        o = []
        for half in range(2):
            q = _head_of_pair(q_pair, masks[half])
            s_loc = _dot_nt(q, k_loc) + bias_ref[2 * pair + half]
            o.append(_softmax_pv([s_loc, _dot_nt(q, k_ctx)], v_ext, NA_SCALE))
        o_ref[:, ps] = jnp.where(masks[0], o[0], o[1]).astype(BF16)


NA_BLOCK_KINDS = (0, 1, NA_BLOCKS - 1)
NA_DIAG = GRID_W - 1
NA_N_DR = 2 * NA_KH - 1


def _na_bias_kernel(e_ref, o_ref):
    lane = lax.broadcasted_iota(jnp.int32, (GRID_W, LANE), 1)
    q_col = lax.broadcasted_iota(jnp.int32, (GRID_W, LANE), 0)
    k_col = lane % GRID_W
    col_start = jnp.clip(q_col - NA_KW // 2, 0, GRID_W - NA_KW)
    col_ok = (k_col >= col_start) & (k_col < col_start + NA_KW)
    low_half = lane < GRID_W
    neg = jnp.full((GRID_W, LANE), NEG_INF, F32)
    tiles = {}

    def toeplitz(dr, half):
        if (dr, half) not in tiles:
            row = jnp.broadcast_to(e_ref[dr:dr + 1, :], (GRID_W, LANE))
            shift = (LANE - NA_DIAG + half * GRID_W) % LANE
            tiles[dr, half] = pltpu.roll(row, shift, 1, stride=1, stride_axis=0)
        return tiles[dr, half]

    for kind, blk in enumerate(NA_BLOCK_KINDS):
        first_key_row = NA_QROWS * min(max(blk - 1, 0), NA_BLOCKS - NA_WIN_ROWS // NA_QROWS)
        for qr in range(NA_QROWS):
            q_row = NA_QROWS * blk + qr
            row_start = min(max(q_row - NA_KH // 2, 0), GRID_H - NA_KH)
            for pair in range(NA_WIN_ROWS // 2):
                halves = []
                for half in range(2):
                    k_row = first_key_row + 2 * pair + half
                    inside = row_start <= k_row < row_start + NA_KH
                    halves.append(toeplitz(k_row - q_row + NA_KH - 1, half) if inside else neg)
                tile = jnp.where(low_half, halves[0], halves[1])
                o_ref[kind, qr * GRID_W:(qr + 1) * GRID_W, pair * LANE:(pair + 1) * LANE] = (
                    jnp.where(col_ok, tile, NEG_INF))


def _na_bias_table(rpb):
    left = NA_DIAG - (NA_KW - 1)
    e = jnp.concatenate([jnp.broadcast_to(rpb[..., :1], rpb.shape[:-1] + (left,)), rpb,
                         jnp.broadcast_to(rpb[..., -1:], rpb.shape[:-1] + (LANE - left - rpb.shape[-1],))],
                        axis=-1) * (1.0 / NA_SCALE)
    return pl.pallas_call(
        _na_bias_kernel,
        grid=(DEPTH, NA_HEADS),
        in_specs=[pl.BlockSpec((None, None, NA_N_DR, LANE), lambda l, h: (l, h, 0, 0))],
        out_specs=pl.BlockSpec((None, len(NA_BLOCK_KINDS), None, NA_QB, NA_WIN), lambda l, h: (l, 0, h, 0, 0)),
        out_shape=jax.ShapeDtypeStruct((DEPTH, len(NA_BLOCK_KINDS), NA_HEADS, NA_QB, NA_WIN), F32),
        name="na_bias",
    )(e)


def _lat_na(q, k, v, cache_k, cache_v, bias, l):
    q0 = P_TOK // NA_QB
    r0 = P_TOK // DEC_SEQ
    kind = lambda j: (j > 0).astype(jnp.int32) + (j == NA_BLOCKS - 1).astype(jnp.int32)
    return pl.pallas_call(
        _lat_na_kernel,
        grid=(DEC_BATCH, NA_BLOCKS),
        in_specs=[pl.BlockSpec((NA_QB, NA_W), lambda b, j: (q0 + b * NA_BLOCKS + j, 0)),
                  pl.BlockSpec((DEC_SEQ, NA_W), lambda b, j: (r0 + b, 0)),
                  pl.BlockSpec((DEC_SEQ, NA_W), lambda b, j: (r0 + b, 0)),
                  pl.BlockSpec((None, None, PAST_LEN, NA_W), lambda b, j: (b, l, 0, 0)),
                  pl.BlockSpec((None, None, PAST_LEN, NA_W), lambda b, j: (b, l, 0, 0)),
                  pl.BlockSpec((None, None, NA_HEADS, NA_QB, NA_WIN), lambda b, j: (l, kind(j), 0, 0, 0))],
        out_specs=pl.BlockSpec((NA_QB, NA_W), lambda b, j: (b * NA_BLOCKS + j, 0)),
        out_shape=jax.ShapeDtypeStruct((S_TOK, NA_W), BF16),
        compiler_params=pltpu.CompilerParams(dimension_semantics=("arbitrary", "arbitrary"),
                                             vmem_limit_bytes=VMEM_LIMIT),
        name="lat_na",
    )(q, k, v, cache_k, cache_v, bias)


def _merge_kernel(x_ref, mod_ref, yfc_ref, ymc_ref, ync_ref, yfl_ref, yml_ref, ynl_ref,
                  wg_ref, bg_ref, wf_ref, wm_ref, wn_ref, wo_ref, g_ref, b_ref, o_ref):
    i = pl.program_id(0)
    shift, scale, gate = _mod_rows(mod_ref, i)
    x = x_ref[...]
    h = (_ln(x, 1e-6) * (1.0 + scale) + shift).astype(BF16)
    is_ctx = i < P_TILES
    mix = None
    for n, (yc_ref, yl_ref, w_ref) in enumerate(((yfc_ref, yfl_ref, wf_ref), (ymc_ref, yml_ref, wm_ref),
                                                 (ync_ref, ynl_ref, wn_ref))):
        sl = slice(n * D_MODEL, (n + 1) * D_MODEL)
        g = jax.nn.sigmoid(_dot(h, wg_ref[:, sl]) + bg_ref[:, sl])
        y = jnp.where(is_ctx, yc_ref[...], yl_ref[...])
        t = g * _dot(y, w_ref[...])
        mix = t if mix is None else mix + t
    z = ALPHA * x + gate * _dot(mix.astype(BF16), wo_ref[...])
    o_ref[...] = _ln(z, 1e-5) * g_ref[...] + b_ref[...]


def _merge(x, mod, y_ctx, y_lat, w_gate, b_gate, w_f, w_m, w_n, w_out, ln_g, ln_b, l):
    tile = lambda w: pl.BlockSpec((TM, w), lambda i: (i, 0))
    ctx_tile = lambda w: pl.BlockSpec((TM, w), lambda i: (jnp.minimum(i, P_TILES - 1), 0))
    lat_tile = lambda w: pl.BlockSpec((TM, w), lambda i: (jnp.maximum(i - P_TILES, 0), 0))
    widths = (F_W, MLA_HEADS * MLA_V, NA_W)
    return pl.pallas_call(
        _merge_kernel,
        grid=(N_TILES,),
        in_specs=[tile(D_MODEL),
                  _resident((None, 8, 3 * D_MODEL), lambda i: (l, 0, 1))]
                 + [ctx_tile(w) for w in widths] + [lat_tile(w) for w in widths] + [
                  _resident((None, D_MODEL, 3 * D_MODEL), lambda i: (l, 0, 0)),
                  _resident((None, 1, 3 * D_MODEL), lambda i: (l, 0, 0)),
                  _resident((None, F_W, D_MODEL), lambda i: (l, 0, 0)),
                  _resident((None, MLA_HEADS * MLA_V, D_MODEL), lambda i: (l, 0, 0)),
                  _resident((None, NA_W, D_MODEL), lambda i: (l, 0, 0)),
                  _resident((None, D_MODEL, D_MODEL), lambda i: (l, 0, 0)),
                  _resident((None, 1, D_MODEL), lambda i: (3 * l + 1, 0, 0)),
                  _resident((None, 1, D_MODEL), lambda i: (3 * l + 1, 0, 0))],
        out_specs=tile(D_MODEL),
        out_shape=jax.ShapeDtypeStruct((TOKENS, D_MODEL), F32),
        compiler_params=pltpu.CompilerParams(dimension_semantics=("arbitrary",),
                                             vmem_limit_bytes=VMEM_LIMIT),
        name="merge",
    )(x, mod, *y_ctx, *y_lat, w_gate, b_gate, w_f, w_m, w_n, w_out, ln_g, ln_b)


def _pad_heads(w, n_heads, width):
    lead = w.shape[:-1]
    w = w.reshape(lead + (n_heads, width))
    w = jnp.pad(w, [(0, 0)] * len(lead) + [(0, 0), (0, MLA_HP - width)])
    return w.reshape(lead + (n_heads * MLA_HP,))


def _rope_tables():
    t = jnp.arange(DEC_SEQ, dtype=jnp.int32)
    pos = jnp.stack([t // GRID_W, t % GRID_W], axis=-1).astype(F32)
    half = AXIS_DIM // 2
    inv_freq = ROPE_BASE ** (-jnp.arange(half, dtype=F32) / half)
    ang = pos[:, :, None] * inv_freq
    ang = jnp.concatenate([ang, ang], axis=-1).reshape(DEC_SEQ, MLA_ROPE)
    pad = lambda a, fill: jnp.pad(a, ((0, 0), (ROPE_LANE0, LANE - ROPE_LANE0 - MLA_ROPE)), constant_values=fill)
    cos = jnp.concatenate([jnp.ones((TM, LANE), F32), pad(jnp.cos(ang), 1.0)], axis=0)
    sin = jnp.concatenate([jnp.zeros((TM, LANE), F32), pad(jnp.sin(ang), 0.0)], axis=0)
    return cos, sin


def kernel(x_prompt, x_sample, cache_mla_ckv, cache_mla_krope, cache_na_k, cache_na_v, c, c_ctx, w_ada, b_ada, ffn1_w1, ffn1_w3, ffn1_w2, ffn2_w1, ffn2_w3, ffn2_w2, w_in, mla_q_norm, mla_w_uq, mla_kv_norm, mla_w_ukv, na_rpb, w_branch_f, w_branch_m, w_branch_n, w_gate, b_gate, w_out, ln_g, ln_b):
    bf = lambda w: w.astype(BF16)
    f1 = (bf(ffn1_w1), bf(ffn1_w3), bf(ffn1_w2))
    f2 = (bf(ffn2_w1), bf(ffn2_w3), bf(ffn2_w2))
    w_in_p = bf(jnp.concatenate(
        [w_in[..., :U_KR0],
         jnp.pad(w_in[..., U_KR0:U_KR0 + MLA_ROPE], ((0, 0), (0, 0), (ROPE_LANE0, LANE - ROPE_LANE0 - MLA_ROPE))),
         w_in[..., U_KR0 + MLA_ROPE:]], axis=-1))
    w_uq_p = bf(_pad_heads(mla_w_uq, MLA_HEADS, MLA_NOPE + MLA_ROPE))
    ukv = mla_w_ukv.reshape(DEPTH, MLA_KV_LORA, MLA_HEADS, MLA_NOPE + MLA_V)
    w_uk_p = bf(_pad_heads(ukv[..., :MLA_NOPE].reshape(DEPTH, MLA_KV_LORA, MLA_HEADS * MLA_NOPE), MLA_HEADS, MLA_NOPE))
    w_uv = bf(ukv[..., MLA_NOPE:].reshape(DEPTH, MLA_KV_LORA, MLA_HEADS * MLA_V))
    w_gate_b, w_f, w_m, w_n, w_out_b = bf(w_gate), bf(w_branch_f), bf(w_branch_m), bf(w_branch_n), bf(w_out)
    q_norm = mla_q_norm.reshape(DEPTH, 1, MLA_Q_LORA)
    kv_norm = mla_kv_norm.reshape(DEPTH, 1, MLA_KV_LORA)
    b_gate3 = b_gate.reshape(DEPTH, 1, 3 * D_MODEL)
    g3 = ln_g.reshape(DEPTH * 3, 1, D_MODEL)
    b3 = ln_b.reshape(DEPTH * 3, 1, D_MODEL)
    cos_t, sin_t = _rope_tables()
    dft_ctx = _fourier_tables(SEQ)
    dft_lat = _fourier_tables(DEC_SEQ)
    kr_pad = jnp.pad(cache_mla_krope, ((0, 0), (0, 0), (0, 0), (ROPE_LANE0, LANE - ROPE_LANE0 - MLA_ROPE)))
    cache_k = cache_na_k.reshape(DEC_BATCH, DEPTH, PAST_LEN, NA_W)
    cache_v = cache_na_v.reshape(DEC_BATCH, DEPTH, PAST_LEN, NA_W)

    cvec = jnp.concatenate([c_ctx[None], c, jnp.zeros((8 - 1 - DEC_BATCH, D_MODEL), F32)], axis=0)
    mod = _adaln(cvec, w_ada, b_ada)
    kc, vc = _ctx_kv(cache_mla_ckv, kr_pad, w_uk_p, w_uv)
    na_bias = _na_bias_table(na_rpb)

    x = jnp.concatenate([x_prompt.reshape(P_TOK, D_MODEL), x_sample.reshape(S_TOK, D_MODEL)], axis=0)
    caches = [jnp.zeros(s, F32) for s in _cache_shapes()]
    for l in range(DEPTH):
        x = _ffn(x, mod, *f1, g3, b3, l, 0)
        u_f, q, k, v, q_n, k_nb, v_nb, *caches = _mixer_in(
            x, mod, cos_t, sin_t, w_in_p, q_norm, w_uq_p, kv_norm, w_uk_p, w_uv, caches, l)
        yf_c = _fourier(u_f, dft_ctx, SEQ, 0, BATCH)
        yf_l = _fourier(u_f, dft_lat, DEC_SEQ, P_TOK // DEC_SEQ, DEC_BATCH)
        ym_c, yn_c = _ctx_attn(q, k, v, q_n, k_nb, v_nb)
        ym_l = _lat_mla(q, k, v, kc, vc, l)
        yn_l = _lat_na(q_n, k_nb, v_nb, cache_k, cache_v, na_bias, l)
        x = _merge(x, mod, (yf_c, ym_c, yn_c), (yf_l, ym_l, yn_l), w_gate_b, b_gate3, w_f, w_m, w_n, w_out_b,
                   g3, b3, l)
        x = _ffn(x, mod, *f2, g3, b3, l, 2)
    ckv, kr_t, nak_t, nav_t = caches
    per_head = lambda a: jnp.transpose(a.reshape(BATCH, DEPTH, NA_HEADS, NA_HEAD_DIM, SEQ), (0, 1, 4, 2, 3))
    return (x[:P_TOK].reshape(BATCH, SEQ, D_MODEL), x[P_TOK:].reshape(DEC_BATCH, DEC_SEQ, D_MODEL),
            ckv, jnp.transpose(kr_t, (0, 1, 3, 2)), per_head(nak_t), per_head(nav_t))
```

```python
import functools

import numpy as np
import jax
import jax.numpy as jnp
from jax import lax
from jax.experimental import pallas as pl
from jax.experimental.pallas import tpu as pltpu

F32 = jnp.float32
BF16 = jnp.bfloat16

D_MODEL = 1024
BATCH = 32
SEQ = 256
DEPTH = 4
DEC_BATCH = 2
DEC_SEQ = 2048
PAST_LEN = 512
GRID_W = 64
GRID_H = DEC_SEQ // GRID_W
D_FF = 2816
F_GROUPS = 4
F_GC = 128
F_W = F_GROUPS * F_GC
MLA_HEADS = 8
MLA_Q_LORA = 384
MLA_KV_LORA = 256
MLA_NOPE = 64
MLA_ROPE = 32
MLA_V = 64
NA_HEADS = 8
NA_HEAD_DIM = 64
NA_KH = 8
NA_KW = 16
NA_W = NA_HEADS * NA_HEAD_DIM
ROPE_BASE = 10000.0
AXIS_DIM = MLA_ROPE // 2
ALPHA = (2.0 * DEPTH) ** 0.25
MLA_SCALE = (MLA_NOPE + MLA_ROPE) ** -0.5
NA_SCALE = NA_HEAD_DIM ** -0.5
NEG_INF = -1e30

LANE = 128
MLA_HP = LANE
MLA_QW = MLA_HEADS * MLA_HP
P_TOK = BATCH * SEQ
S_TOK = DEC_BATCH * DEC_SEQ
TOKENS = P_TOK + S_TOK
TM = 512
SIDE_ROWS = 64
N_TILES = TOKENS // TM
P_TILES = P_TOK // TM
S_TILES_PER_REQ = DEC_SEQ // TM
MXU_TILE = 256
FF_CHUNKS = (0, 4 * MXU_TILE, 8 * MXU_TILE, D_FF)
U_Q0 = F_W
U_KV0 = U_Q0 + MLA_Q_LORA
U_KR0 = U_KV0 + MLA_KV_LORA
U_NA0 = U_KR0 + LANE
U_W = U_NA0 + 3 * NA_W
ROPE_LANE0 = MLA_NOPE
NA_QROWS = 4
NA_QB = NA_QROWS * GRID_W
NA_WIN_ROWS = NA_QROWS + NA_KH
NA_WIN = NA_WIN_ROWS * GRID_W
NA_BLOCKS = GRID_H // NA_QROWS
MLA_QB = 256
FOURIER_ROWS = 1024
VMEM_LIMIT = 56 * 1024 * 1024


def _group(i):
    return jnp.where(i < P_TILES, 0, 1 + (i - P_TILES) // S_TILES_PER_REQ)


def _rope_block(i):
    return jnp.where(i < P_TILES, 0, 1 + (i - P_TILES) % S_TILES_PER_REQ)


def _ln(x, eps):
    mu = jnp.mean(x, axis=-1, keepdims=True)
    xc = x - mu
    var = jnp.mean(xc * xc, axis=-1, keepdims=True)
    return xc * lax.rsqrt(var + eps)


def _rms(x, g):
    return x * lax.rsqrt(jnp.mean(x * x, axis=-1, keepdims=True) + 1e-6) * g


def _dot(a, b):
    return jnp.dot(a, b, preferred_element_type=F32)


def _dot_nt(a, b):
    return lax.dot_general(a, b, (((1,), (1,)), ((), ())), preferred_element_type=F32)


def _mod_rows(mod_ref, i):
    m = mod_ref[pl.ds(_group(i), 1), :]
    return m[:, :D_MODEL], m[:, D_MODEL:2 * D_MODEL], m[:, 2 * D_MODEL:]


def _rope(x, cos, sin):
    lane = lax.broadcasted_iota(jnp.int32, x.shape, 1)
    first_half = (lane % AXIS_DIM) < (AXIS_DIM // 2)
    rot = jnp.where(first_half, -pltpu.roll(x, LANE - AXIS_DIM // 2, 1), pltpu.roll(x, AXIS_DIM // 2, 1))
    return x * cos + rot * sin


def _adaln_kernel(c_ref, w_ref, b_ref, o_ref):
    c = c_ref[...]
    s = (c * jax.nn.sigmoid(c)).astype(BF16)
    o_ref[...] = _dot(s, w_ref[...].astype(BF16)) + b_ref[...]


def _adaln(cvec, w_ada, b_ada):
    n_col = 9 * D_MODEL // D_MODEL
    return pl.pallas_call(
        _adaln_kernel,
        grid=(DEPTH, n_col),
        in_specs=[pl.BlockSpec((8, D_MODEL), lambda l, j: (0, 0)),
                  pl.BlockSpec((None, D_MODEL, D_MODEL), lambda l, j: (l, 0, j)),
                  pl.BlockSpec((None, 1, D_MODEL), lambda l, j: (l, 0, j))],
        out_specs=pl.BlockSpec((None, 8, D_MODEL), lambda l, j: (l, 0, j)),
        out_shape=jax.ShapeDtypeStruct((DEPTH, 8, 9 * D_MODEL), F32),
        name="adaln",
    )(cvec, w_ada, b_ada.reshape(DEPTH, 1, 9 * D_MODEL))


def _ffn_kernel(*refs, split_in):
    n_x = 4 if split_in else 2
    x_refs, (mod_ref, w1_ref, w3_ref, w2_ref, g_ref, b_ref, o_ref, h_scr, y_scr) = refs[:n_x], refs[n_x:]
    s = pl.program_id(0)
    cur = s % 2
    nxt = 1 - cur
    pieces = [pl.ds(r, SIDE_ROWS) for r in range(0, TM, SIDE_ROWS)]

    def x_tile(which, tile):
        if not split_in:
            return lambda rows: x_refs[which][rows, :]
        ctx_ref, lat_ref = x_refs[2 * which:2 * which + 2]
        return lambda rows: jnp.where(tile < P_TILES, ctx_ref[rows, :], lat_ref[rows, :])

    x_prev = x_tile(0, s - 1)
    x_next = x_tile(1, s + 1)

    def modulate(x_rows, tile, slot):
        shift, scale, _ = _mod_rows(mod_ref, tile)

        def piece(rows):
            h_scr[slot, rows, :] = (_ln(x_rows(rows), 1e-6) * (1.0 + scale) + shift).astype(BF16)
        return [functools.partial(piece, rows) for rows in pieces]

    def finish():
        _, _, gate = _mod_rows(mod_ref, s - 1)

        def piece(rows):
            z = ALPHA * x_prev(rows) + (0.5 * gate) * y_scr[nxt, rows, :]
            out = _ln(z, 1e-5) * g_ref[...] + b_ref[...]
            o_ref[rows, :] = out
            y_scr[nxt, rows, :] = out
        return [functools.partial(piece, rows) for rows in pieces]

    def matmuls(side_work):
        side_work = list(side_work)
        chunks = [slice(c0, c1) for c0, c1 in zip(FF_CHUNKS[:-1], FF_CHUNKS[1:])]
        per_dot = -(-len(side_work) // (3 * len(chunks)))

        def dot_with_side(lhs, rhs):
            for _ in range(min(per_dot, len(side_work))):
                side_work.pop(0)()
            return _dot(lhs() if callable(lhs) else lhs, rhs)

        h = lambda: h_scr[cur]

        def gate_up(sl):
            a = dot_with_side(h, w1_ref[:, sl])
            return (a * jax.nn.sigmoid(a) * dot_with_side(h, w3_ref[:, sl])).astype(BF16)

        t_next = gate_up(chunks[0])
        for n, sl in enumerate(chunks):
            t = t_next
            if n + 1 < len(chunks):
                t_next = gate_up(chunks[n + 1])
            yc = dot_with_side(t, w2_ref[sl, :])
            if n == 0:
                y_scr[cur] = yc
            else:
                y_scr[cur] += yc
        for work in side_work:
            work()

    @pl.when(s == 0)
    def _():
        for work in modulate(x_tile(0, s), s, cur):
            work()
        matmuls(modulate(x_next, s + 1, nxt))

    @pl.when(jnp.logical_and(s > 0, s < N_TILES))
    def _():
        matmuls(finish() + modulate(x_next, s + 1, nxt))

    @pl.when(s == N_TILES)
    def _():
        for work in finish():
            work()


def _resident(shape, index_map):
    return pl.BlockSpec(shape, index_map, pipeline_mode=pl.Buffered(1))


def _ffn(x, mod, w1, w3, w2, ln_g, ln_b, l, sub):
    split_in = isinstance(x, tuple)
    x_tile = lambda lo, hi, off: pl.BlockSpec((TM, D_MODEL), lambda i: (jnp.clip(i + off, lo, hi) - lo, 0))
    if split_in:
        x_args = [x[0], x[1]] * 2
        x_specs = [x_tile(0, P_TILES - 1, off) if n == 0 else x_tile(P_TILES, N_TILES - 1, off)
                   for off in (-1, 1) for n in range(2)]
    else:
        x_args = [x, x]
        x_specs = [x_tile(0, N_TILES - 1, -1), x_tile(0, N_TILES - 1, 1)]
    return pl.pallas_call(
        functools.partial(_ffn_kernel, split_in=split_in),
        grid=(N_TILES + 1,),
        in_specs=x_specs + [
                  _resident((None, 8, 3 * D_MODEL), lambda i: (l, 0, sub)),
                  _resident((None, D_MODEL, D_FF), lambda i: (l, 0, 0)),
                  _resident((None, D_MODEL, D_FF), lambda i: (l, 0, 0)),
                  _resident((None, D_FF, D_MODEL), lambda i: (l, 0, 0)),
                  _resident((None, 1, D_MODEL), lambda i: (3 * l + sub, 0, 0)),
                  _resident((None, 1, D_MODEL), lambda i: (3 * l + sub, 0, 0))],
        out_specs=pl.BlockSpec((TM, D_MODEL), lambda i: (jnp.maximum(i - 1, 0), 0)),
        out_shape=jax.ShapeDtypeStruct((TOKENS, D_MODEL), F32),
        scratch_shapes=[pltpu.VMEM((2, TM, D_MODEL), BF16), pltpu.VMEM((2, TM, D_MODEL), F32)],
        compiler_params=pltpu.CompilerParams(dimension_semantics=("arbitrary",),
                                             vmem_limit_bytes=VMEM_LIMIT),
        name="ffn",
    )(*x_args, mod, w1, w3, w2, ln_g, ln_b)


def _mixer_in_kernel(x_ref, mod_ref, cos_ref, sin_ref, w_in_ref, qn_ref, w_uq_ref, kvn_ref, w_uk_ref, w_uv_ref,
                     ckv_prev, kr_prev, kna_prev, vna_prev,
                     uf_ref, q_ref, k_ref, v_ref, qna_ref, knab_ref, vnab_ref, ckv_ref, kr_ref, kna_ref, vna_ref):
    del ckv_prev, kr_prev, kna_prev, vna_prev
    i = pl.program_id(0)
    shift, scale, _ = _mod_rows(mod_ref, i)
    h = (_ln(x_ref[...], 1e-6) * (1.0 + scale) + shift).astype(BF16)
    cos = cos_ref[...]
    sin = sin_ref[...]

    uf_ref[...] = _dot(h, w_in_ref[:, :U_Q0])

    u_q = _dot(h, w_in_ref[:, U_Q0:U_KV0])
    q = _dot(_rms(u_q, qn_ref[...]).astype(BF16), w_uq_ref[...])
    for hd in range(MLA_HEADS):
        sl = slice(hd * MLA_HP, (hd + 1) * MLA_HP)
        q_ref[:, sl] = _rope(q[:, sl], cos, sin).astype(BF16)

    c_kv = _rms(_dot(h, w_in_ref[:, U_KV0:U_KR0]), kvn_ref[...])
    kr = _dot(h, w_in_ref[:, U_KR0:U_NA0])
    k_na = _dot(h, w_in_ref[:, U_NA0 + NA_W:U_NA0 + 2 * NA_W])
    v_na = _dot(h, w_in_ref[:, U_NA0 + 2 * NA_W:])

    @pl.when(i < P_TILES)
    def _():
        kr_t = kr.T[ROPE_LANE0:ROPE_LANE0 + MLA_ROPE, :]
        kna_t = k_na.T
        vna_t = v_na.T
        for b in range(TM // SEQ):
            rows = slice(b * SEQ, (b + 1) * SEQ)
            ckv_ref[b] = c_kv[rows, :]
            kr_ref[b] = kr_t[:, rows]
            kna_ref[b] = kna_t[:, rows]
            vna_ref[b] = vna_t[:, rows]

    c_kv = c_kv.astype(BF16)
    kr = _rope(kr, cos, sin)
    k = _dot(c_kv, w_uk_ref[...])
    for hd in range(MLA_HEADS):
        sl = slice(hd * MLA_HP, (hd + 1) * MLA_HP)
        k_ref[:, sl] = (k[:, sl] + kr).astype(BF16)
    v_ref[...] = _dot(c_kv, w_uv_ref[...]).astype(BF16)

    qna_ref[...] = _dot(h, w_in_ref[:, U_NA0:U_NA0 + NA_W]).astype(BF16)
    knab_ref[...] = k_na.astype(BF16)
    vnab_ref[...] = v_na.astype(BF16)


def _cache_shapes():
    return [(BATCH, DEPTH, SEQ, MLA_KV_LORA), (BATCH, DEPTH, MLA_ROPE, SEQ),
            (BATCH, DEPTH, NA_W, SEQ), (BATCH, DEPTH, NA_W, SEQ)]


def _mixer_in(x, mod, cos_t, sin_t, w_in, q_norm, w_uq, kv_norm, w_uk, w_uv, caches, l):
    tile = lambda w: pl.BlockSpec((TM, w), lambda i: (i, 0))
    acts = [(F_W, F32), (MLA_QW, BF16), (MLA_QW, BF16), (MLA_HEADS * MLA_V, BF16), (NA_W, BF16), (NA_W, BF16),
            (NA_W, BF16)]
    cache_spec = lambda s: pl.BlockSpec((TM // SEQ, None) + s[2:],
                                        lambda i: (jnp.minimum(i, P_TILES - 1), l, 0, 0))
    n_in = 10
    return pl.pallas_call(
        _mixer_in_kernel,
        grid=(N_TILES,),
        in_specs=[tile(D_MODEL),
                  _resident((None, 8, 3 * D_MODEL), lambda i: (l, 0, 1)),
                  pl.BlockSpec((TM, LANE), lambda i: (_rope_block(i), 0)),
                  pl.BlockSpec((TM, LANE), lambda i: (_rope_block(i), 0)),
                  _resident((None, D_MODEL, U_W), lambda i: (l, 0, 0)),
                  _resident((None, 1, MLA_Q_LORA), lambda i: (l, 0, 0)),
                  _resident((None, MLA_Q_LORA, MLA_QW), lambda i: (l, 0, 0)),
                  _resident((None, 1, MLA_KV_LORA), lambda i: (l, 0, 0)),
                  _resident((None, MLA_KV_LORA, MLA_QW), lambda i: (l, 0, 0)),
                  _resident((None, MLA_KV_LORA, MLA_HEADS * MLA_V), lambda i: (l, 0, 0))]
                 + [pl.BlockSpec(memory_space=pl.ANY)] * len(caches),
        out_specs=[tile(w) for w, _ in acts] + [cache_spec(s) for s in _cache_shapes()],
        out_shape=[jax.ShapeDtypeStruct((TOKENS, w), dt) for w, dt in acts]
                  + [jax.ShapeDtypeStruct(s, F32) for s in _cache_shapes()],
        input_output_aliases={n_in + n: len(acts) + n for n in range(len(caches))},
        compiler_params=pltpu.CompilerParams(dimension_semantics=("arbitrary",),
                                             vmem_limit_bytes=VMEM_LIMIT),
        name="mixer_in",
    )(x, mod, cos_t, sin_t, w_in, q_norm, w_uq, kv_norm, w_uk, w_uv, *caches)


def _ctx_kv_kernel(ckv_ref, kr_ref, w_uk_ref, w_uv_ref, k_ref, v_ref):
    c = ckv_ref[...].astype(BF16)
    k = _dot(c, w_uk_ref[...])
    kr = kr_ref[...]
    for hd in range(MLA_HEADS):
        sl = slice(hd * MLA_HP, (hd + 1) * MLA_HP)
        k_ref[:, sl] = (k[:, sl] + kr).astype(BF16)
    v_ref[...] = _dot(c, w_uv_ref[...]).astype(BF16)


def _ctx_kv(cache_ckv, cache_kr_pad, w_uk, w_uv):
    return pl.pallas_call(
        _ctx_kv_kernel,
        grid=(DEC_BATCH, DEPTH),
        in_specs=[pl.BlockSpec((None, None, PAST_LEN, MLA_KV_LORA), lambda b, l: (b, l, 0, 0)),
                  pl.BlockSpec((None, None, PAST_LEN, LANE), lambda b, l: (b, l, 0, 0)),
                  pl.BlockSpec((None, MLA_KV_LORA, MLA_QW), lambda b, l: (l, 0, 0)),
                  pl.BlockSpec((None, MLA_KV_LORA, MLA_HEADS * MLA_V), lambda b, l: (l, 0, 0))],
        out_specs=[pl.BlockSpec((None, None, PAST_LEN, MLA_QW), lambda b, l: (b, l, 0, 0)),
                   pl.BlockSpec((None, None, PAST_LEN, MLA_HEADS * MLA_V), lambda b, l: (b, l, 0, 0))],
        out_shape=[jax.ShapeDtypeStruct((DEC_BATCH, DEPTH, PAST_LEN, MLA_QW), BF16),
                   jax.ShapeDtypeStruct((DEC_BATCH, DEPTH, PAST_LEN, MLA_HEADS * MLA_V), BF16)],
        name="ctx_kv",
    )(cache_ckv, cache_kr_pad, w_uk, w_uv)


def _fourier_kernel(x_ref, cs_ref, cl_ref, sl_ref, o_ref):
    length = cl_ref.shape[0]
    x = x_ref[...].astype(BF16)
    xc, xs = [], []
    for g in range(F_GROUPS):
        t = _dot(x[:, g * F_GC:(g + 1) * F_GC], cs_ref[...])
        xc.append(t[:, :F_GC])
        xs.append(t[:, F_GC:])
    xc = jnp.concatenate(xc, axis=1).astype(BF16)
    xs = jnp.concatenate(xs, axis=1).astype(BF16)
    for r in range(0, x_ref.shape[0], length):
        rows = slice(r, r + length)
        o_ref[rows, :] = (_dot(cl_ref[...], xc[rows]) - _dot(sl_ref[...], xs[rows])).astype(BF16)


def _dft_tables(n):
    k = np.arange(n, dtype=np.int64)
    ang = 2.0 * np.pi * ((k[:, None] * k[None, :]) % n).astype(np.float64) / n
    s = n ** -0.5
    return np.cos(ang) * s, np.sin(ang) * s


def _fourier_tables(length):
    cc, sc = _dft_tables(F_GC)
    cl, sl = _dft_tables(length)
    as_bf16 = lambda a: jnp.asarray(a, F32).astype(BF16)
    return as_bf16(np.concatenate([cc, sc], axis=1)), as_bf16(cl), as_bf16(sl)


def _fourier(u_f, tables, length, first_block, n_blocks):
    cs, cl, sl = tables
    rows = max(length, FOURIER_ROWS)
    assert rows % length == 0 and (n_blocks * length) % rows == 0 and (first_block * length) % rows == 0
    first = first_block * length // rows
    return pl.pallas_call(
        _fourier_kernel,
        grid=(n_blocks * length // rows,),
        in_specs=[pl.BlockSpec((rows, F_W), lambda b: (first + b, 0)),
                  _resident((F_GC, 2 * F_GC), lambda b: (0, 0)),
                  _resident((length, length), lambda b: (0, 0)),
                  _resident((length, length), lambda b: (0, 0))],
        out_specs=pl.BlockSpec((rows, F_W), lambda b: (b, 0)),
        out_shape=jax.ShapeDtypeStruct((n_blocks * length, F_W), BF16),
        compiler_params=pltpu.CompilerParams(dimension_semantics=("arbitrary",),
                                             vmem_limit_bytes=VMEM_LIMIT),
        name="fourier_%d" % length,
    )(u_f, cs, cl, sl)


LOG2E = 1.4426950408889634


def _softmax_pv(scores, values_ext, scale):
    m = None
    for s in scores:
        sm = jnp.max(s, axis=-1, keepdims=True)
        m = sm if m is None else jnp.maximum(m, sm)
    acc = None
    for s, v in zip(scores, values_ext):
        p = jnp.exp2((s - m) * (scale * LOG2E)).astype(BF16)
        pv = _dot(p, v)
        acc = pv if acc is None else acc + pv
    return acc[:, :LANE] / acc[:, LANE:]


def _with_ones(v_pair):
    return jnp.concatenate([v_pair, jnp.ones_like(v_pair)], axis=1)


def _half_masks(rows):
    low = lax.broadcasted_iota(jnp.int32, (rows, LANE), 1) < LANE // 2
    return low, jnp.logical_not(low)


def _head_of_pair(x_pair, mask):
    return jnp.where(mask, x_pair, jnp.zeros_like(x_pair))


def _ctx_attn_kernel(q_ref, k_ref, v_ref, qn_ref, kn_ref, vn_ref, om_ref, on_ref):
    masks = _half_masks(SEQ)
    for pair in range(MLA_HEADS // 2):
        ps = slice(pair * LANE, (pair + 1) * LANE)
        v_ext = _with_ones(v_ref[:, ps])
        o = []
        for half in range(2):
            sl = slice((2 * pair + half) * MLA_HP, (2 * pair + half + 1) * MLA_HP)
            s = _dot_nt(q_ref[:, sl], k_ref[:, sl])
            o.append(_softmax_pv([s], [v_ext], MLA_SCALE))
        om_ref[:, ps] = jnp.where(masks[0], o[0], o[1]).astype(BF16)
    for pair in range(NA_HEADS // 2):
        ps = slice(pair * LANE, (pair + 1) * LANE)
        v_ext = _with_ones(vn_ref[:, ps])
        q_pair = qn_ref[:, ps]
        k_pair = kn_ref[:, ps]
        o = []
        for half in range(2):
            s = _dot_nt(_head_of_pair(q_pair, masks[half]), k_pair)
            o.append(_softmax_pv([s], [v_ext], NA_SCALE))
        on_ref[:, ps] = jnp.where(masks[0], o[0], o[1]).astype(BF16)


def _ctx_attn(q, k, v, qn, kn, vn):
    blk = lambda w: pl.BlockSpec((SEQ, w), lambda b: (b, 0))
    return pl.pallas_call(
        _ctx_attn_kernel,
        grid=(BATCH,),
        in_specs=[blk(MLA_QW), blk(MLA_QW), blk(MLA_HEADS * MLA_V), blk(NA_W), blk(NA_W), blk(NA_W)],
        out_specs=[blk(MLA_HEADS * MLA_V), blk(NA_W)],
        out_shape=[jax.ShapeDtypeStruct((P_TOK, MLA_HEADS * MLA_V), BF16),
                   jax.ShapeDtypeStruct((P_TOK, NA_W), BF16)],
        compiler_params=pltpu.CompilerParams(dimension_semantics=("arbitrary",)),
        name="ctx_attn",
    )(q, k, v, qn, kn, vn)


def _lat_mla_kernel(q_ref, k_ref, v_ref, kc_ref, vc_ref, o_ref):
    low, _ = _half_masks(MLA_QB)
    for pair in range(MLA_HEADS // 2):
        ps = slice(pair * LANE, (pair + 1) * LANE)
        v_ext = [_with_ones(v_ref[:, ps]), _with_ones(vc_ref[:, ps])]
        o = []
        for half in range(2):
            sl = slice((2 * pair + half) * MLA_HP, (2 * pair + half + 1) * MLA_HP)
            q = q_ref[:, sl]
            o.append(_softmax_pv([_dot_nt(q, k_ref[:, sl]), _dot_nt(q, kc_ref[:, sl])], v_ext, MLA_SCALE))
        o_ref[:, ps] = jnp.where(low, o[0], o[1]).astype(BF16)


def _lat_mla(q, k, v, kc, vc, l):
    nq = DEC_SEQ // MLA_QB
    q0 = P_TOK // MLA_QB
    r0 = P_TOK // DEC_SEQ
    return pl.pallas_call(
        _lat_mla_kernel,
        grid=(DEC_BATCH, nq),
        in_specs=[pl.BlockSpec((MLA_QB, MLA_QW), lambda b, j: (q0 + b * nq + j, 0)),
                  pl.BlockSpec((DEC_SEQ, MLA_QW), lambda b, j: (r0 + b, 0)),
                  pl.BlockSpec((DEC_SEQ, MLA_HEADS * MLA_V), lambda b, j: (r0 + b, 0)),
                  pl.BlockSpec((None, None, PAST_LEN, MLA_QW), lambda b, j: (b, l, 0, 0)),
                  pl.BlockSpec((None, None, PAST_LEN, MLA_HEADS * MLA_V), lambda b, j: (b, l, 0, 0))],
        out_specs=pl.BlockSpec((MLA_QB, MLA_HEADS * MLA_V), lambda b, j: (b * nq + j, 0)),
        out_shape=jax.ShapeDtypeStruct((S_TOK, MLA_HEADS * MLA_V), BF16),
        compiler_params=pltpu.CompilerParams(dimension_semantics=("arbitrary", "arbitrary"),
                                             vmem_limit_bytes=VMEM_LIMIT),
        name="lat_mla",
    )(q, k, v, kc, vc)


def _na_window_block(j):
    return jnp.clip(j - 1, 0, NA_BLOCKS - NA_WIN_ROWS // NA_QROWS)


def _lat_na_kernel(q_ref, k_ref, v_ref, kc_ref, vc_ref, bias_ref, o_ref):
    j = pl.program_id(1)
    start = pl.multiple_of(_na_window_block(j) * NA_QB, NA_QB)
    k_win = k_ref[pl.ds(start, NA_WIN), :]
    v_win = v_ref[pl.ds(start, NA_WIN), :]
    masks = _half_masks(NA_QB)
    for pair in range(NA_HEADS // 2):
        ps = slice(pair * LANE, (pair + 1) * LANE)
        k_loc = k_win[:, ps]
        k_ctx = kc_ref[:, ps].astype(BF16)
        v_ext = [_with_ones(v_win[:, ps]), _with_ones(vc_ref[:, ps].astype(BF16))]
        q_pair = q_ref[:, ps]
        o = []
        for half in range(2):
            q = _head_of_pair(q_pair, masks[half])
            s_loc = _dot_nt(q, k_loc) + bias_ref[2 * pair + half]
            o.append(_softmax_pv([s_loc, _dot_nt(q, k_ctx)], v_ext, NA_SCALE))
        o_ref[:, ps] = jnp.where(masks[0], o[0], o[1]).astype(BF16)


NA_BLOCK_KINDS = (0, 1, NA_BLOCKS - 1)
NA_DIAG = GRID_W - 1
NA_N_DR = 2 * NA_KH - 1


def _na_bias_kernel(e_ref, o_ref):
    lane = lax.broadcasted_iota(jnp.int32, (GRID_W, LANE), 1)
    q_col = lax.broadcasted_iota(jnp.int32, (GRID_W, LANE), 0)
    k_col = lane % GRID_W
    col_start = jnp.clip(q_col - NA_KW // 2, 0, GRID_W - NA_KW)
    col_ok = (k_col >= col_start) & (k_col < col_start + NA_KW)
    low_half = lane < GRID_W
    neg = jnp.full((GRID_W, LANE), NEG_INF, F32)
    tiles = {}

    def toeplitz(dr, half):
        if (dr, half) not in tiles:
            row = jnp.broadcast_to(e_ref[dr:dr + 1, :], (GRID_W, LANE))
            shift = (LANE - NA_DIAG + half * GRID_W) % LANE
            tiles[dr, half] = pltpu.roll(row, shift, 1, stride=1, stride_axis=0)
        return tiles[dr, half]

    for kind, blk in enumerate(NA_BLOCK_KINDS):
        first_key_row = NA_QROWS * min(max(blk - 1, 0), NA_BLOCKS - NA_WIN_ROWS // NA_QROWS)
        for qr in range(NA_QROWS):
            q_row = NA_QROWS * blk + qr
            row_start = min(max(q_row - NA_KH // 2, 0), GRID_H - NA_KH)
            for pair in range(NA_WIN_ROWS // 2):
                halves = []
                for half in range(2):
                    k_row = first_key_row + 2 * pair + half
                    inside = row_start <= k_row < row_start + NA_KH
                    halves.append(toeplitz(k_row - q_row + NA_KH - 1, half) if inside else neg)
                tile = jnp.where(low_half, halves[0], halves[1])
                o_ref[kind, qr * GRID_W:(qr + 1) * GRID_W, pair * LANE:(pair + 1) * LANE] = (
                    jnp.where(col_ok, tile, NEG_INF))


def _na_bias_table(rpb):
    left = NA_DIAG - (NA_KW - 1)
    e = jnp.concatenate([jnp.broadcast_to(rpb[..., :1], rpb.shape[:-1] + (left,)), rpb,
                         jnp.broadcast_to(rpb[..., -1:], rpb.shape[:-1] + (LANE - left - rpb.shape[-1],))],
                        axis=-1) * (1.0 / NA_SCALE)
    return pl.pallas_call(
        _na_bias_kernel,
        grid=(DEPTH, NA_HEADS),
        in_specs=[pl.BlockSpec((None, None, NA_N_DR, LANE), lambda l, h: (l, h, 0, 0))],
        out_specs=pl.BlockSpec((None, len(NA_BLOCK_KINDS), None, NA_QB, NA_WIN), lambda l, h: (l, 0, h, 0, 0)),
        out_shape=jax.ShapeDtypeStruct((DEPTH, len(NA_BLOCK_KINDS), NA_HEADS, NA_QB, NA_WIN), F32),
        name="na_bias",
    )(e)


def _lat_na(q, k, v, cache_k, cache_v, bias, l):
    q0 = P_TOK // NA_QB
    r0 = P_TOK // DEC_SEQ
    kind = lambda j: (j > 0).astype(jnp.int32) + (j == NA_BLOCKS - 1).astype(jnp.int32)
    return pl.pallas_call(
        _lat_na_kernel,
        grid=(DEC_BATCH, NA_BLOCKS),
        in_specs=[pl.BlockSpec((NA_QB, NA_W), lambda b, j: (q0 + b * NA_BLOCKS + j, 0)),
                  pl.BlockSpec((DEC_SEQ, NA_W), lambda b, j: (r0 + b, 0)),
                  pl.BlockSpec((DEC_SEQ, NA_W), lambda b, j: (r0 + b, 0)),
                  pl.BlockSpec((None, None, PAST_LEN, NA_W), lambda b, j: (b, l, 0, 0)),
                  pl.BlockSpec((None, None, PAST_LEN, NA_W), lambda b, j: (b, l, 0, 0)),
                  pl.BlockSpec((None, None, NA_HEADS, NA_QB, NA_WIN), lambda b, j: (l, kind(j), 0, 0, 0))],
        out_specs=pl.BlockSpec((NA_QB, NA_W), lambda b, j: (b * NA_BLOCKS + j, 0)),
        out_shape=jax.ShapeDtypeStruct((S_TOK, NA_W), BF16),
        compiler_params=pltpu.CompilerParams(dimension_semantics=("arbitrary", "arbitrary"),
                                             vmem_limit_bytes=VMEM_LIMIT),
        name="lat_na",
    )(q, k, v, cache_k, cache_v, bias)


def _merge_kernel(x_ref, mod_ref, yfc_ref, ymc_ref, ync_ref, yfl_ref, yml_ref, ynl_ref,
                  wg_ref, bg_ref, wf_ref, wm_ref, wn_ref, wo_ref, g_ref, b_ref, o_ref):
    i = pl.program_id(0)
    shift, scale, gate = _mod_rows(mod_ref, i)
    x = x_ref[...]
    h = (_ln(x, 1e-6) * (1.0 + scale) + shift).astype(BF16)
    is_ctx = i < P_TILES
    mix = None
    for n, (yc_ref, yl_ref, w_ref) in enumerate(((yfc_ref, yfl_ref, wf_ref), (ymc_ref, yml_ref, wm_ref),
                                                 (ync_ref, ynl_ref, wn_ref))):
        sl = slice(n * D_MODEL, (n + 1) * D_MODEL)
        g = jax.nn.sigmoid(_dot(h, wg_ref[:, sl]) + bg_ref[:, sl])
        y = jnp.where(is_ctx, yc_ref[...], yl_ref[...])
        t = g * _dot(y, w_ref[...])
        mix = t if mix is None else mix + t
    z = ALPHA * x + gate * _dot(mix.astype(BF16), wo_ref[...])
    o_ref[...] = _ln(z, 1e-5) * g_ref[...] + b_ref[...]


def _merge(x, mod, y_ctx, y_lat, w_gate, b_gate, w_f, w_m, w_n, w_out, ln_g, ln_b, l):
    tile = lambda w: pl.BlockSpec((TM, w), lambda i: (i, 0))
    ctx_tile = lambda w: pl.BlockSpec((TM, w), lambda i: (jnp.minimum(i, P_TILES - 1), 0))
    lat_tile = lambda w: pl.BlockSpec((TM, w), lambda i: (jnp.maximum(i - P_TILES, 0), 0))
    widths = (F_W, MLA_HEADS * MLA_V, NA_W)
    return pl.pallas_call(
        _merge_kernel,
        grid=(N_TILES,),
        in_specs=[tile(D_MODEL),
                  _resident((None, 8, 3 * D_MODEL), lambda i: (l, 0, 1))]
                 + [ctx_tile(w) for w in widths] + [lat_tile(w) for w in widths] + [
                  _resident((None, D_MODEL, 3 * D_MODEL), lambda i: (l, 0, 0)),
                  _resident((None, 1, 3 * D_MODEL), lambda i: (l, 0, 0)),
                  _resident((None, F_W, D_MODEL), lambda i: (l, 0, 0)),
                  _resident((None, MLA_HEADS * MLA_V, D_MODEL), lambda i: (l, 0, 0)),
                  _resident((None, NA_W, D_MODEL), lambda i: (l, 0, 0)),
                  _resident((None, D_MODEL, D_MODEL), lambda i: (l, 0, 0)),
                  _resident((None, 1, D_MODEL), lambda i: (3 * l + 1, 0, 0)),
                  _resident((None, 1, D_MODEL), lambda i: (3 * l + 1, 0, 0))],
        out_specs=tile(D_MODEL),
        out_shape=jax.ShapeDtypeStruct((TOKENS, D_MODEL), F32),
        compiler_params=pltpu.CompilerParams(dimension_semantics=("arbitrary",),
                                             vmem_limit_bytes=VMEM_LIMIT),
        name="merge",
    )(x, mod, *y_ctx, *y_lat, w_gate, b_gate, w_f, w_m, w_n, w_out, ln_g, ln_b)


def _pad_heads(w, n_heads, width):
    lead = w.shape[:-1]
    w = w.reshape(lead + (n_heads, width))
    w = jnp.pad(w, [(0, 0)] * len(lead) + [(0, 0), (0, MLA_HP - width)])
    return w.reshape(lead + (n_heads * MLA_HP,))


def _rope_tables():
    t = jnp.arange(DEC_SEQ, dtype=jnp.int32)
    pos = jnp.stack([t // GRID_W, t % GRID_W], axis=-1).astype(F32)
    half = AXIS_DIM // 2
    inv_freq = ROPE_BASE ** (-jnp.arange(half, dtype=F32) / half)
    ang = pos[:, :, None] * inv_freq
    ang = jnp.concatenate([ang, ang], axis=-1).reshape(DEC_SEQ, MLA_ROPE)
    pad = lambda a, fill: jnp.pad(a, ((0, 0), (ROPE_LANE0, LANE - ROPE_LANE0 - MLA_ROPE)), constant_values=fill)
    cos = jnp.concatenate([jnp.ones((TM, LANE), F32), pad(jnp.cos(ang), 1.0)], axis=0)
    sin = jnp.concatenate([jnp.zeros((TM, LANE), F32), pad(jnp.sin(ang), 0.0)], axis=0)
    return cos, sin


def kernel(x_prompt, x_sample, cache_mla_ckv, cache_mla_krope, cache_na_k, cache_na_v, c, c_ctx, w_ada, b_ada, ffn1_w1, ffn1_w3, ffn1_w2, ffn2_w1, ffn2_w3, ffn2_w2, w_in, mla_q_norm, mla_w_uq, mla_kv_norm, mla_w_ukv, na_rpb, w_branch_f, w_branch_m, w_branch_n, w_gate, b_gate, w_out, ln_g, ln_b):
    bf = lambda w: w.astype(BF16)
    f1 = (bf(ffn1_w1), bf(ffn1_w3), bf(ffn1_w2))
    f2 = (bf(ffn2_w1), bf(ffn2_w3), bf(ffn2_w2))
    w_in_p = bf(jnp.concatenate(
        [w_in[..., :U_KR0],
         jnp.pad(w_in[..., U_KR0:U_KR0 + MLA_ROPE], ((0, 0), (0, 0), (ROPE_LANE0, LANE - ROPE_LANE0 - MLA_ROPE))),
         w_in[..., U_KR0 + MLA_ROPE:]], axis=-1))
    w_uq_p = bf(_pad_heads(mla_w_uq, MLA_HEADS, MLA_NOPE + MLA_ROPE))
    ukv = mla_w_ukv.reshape(DEPTH, MLA_KV_LORA, MLA_HEADS, MLA_NOPE + MLA_V)
    w_uk_p = bf(_pad_heads(ukv[..., :MLA_NOPE].reshape(DEPTH, MLA_KV_LORA, MLA_HEADS * MLA_NOPE), MLA_HEADS, MLA_NOPE))
    w_uv = bf(ukv[..., MLA_NOPE:].reshape(DEPTH, MLA_KV_LORA, MLA_HEADS * MLA_V))
    w_gate_b, w_f, w_m, w_n, w_out_b = bf(w_gate), bf(w_branch_f), bf(w_branch_m), bf(w_branch_n), bf(w_out)
    q_norm = mla_q_norm.reshape(DEPTH, 1, MLA_Q_LORA)
    kv_norm = mla_kv_norm.reshape(DEPTH, 1, MLA_KV_LORA)
    b_gate3 = b_gate.reshape(DEPTH, 1, 3 * D_MODEL)
    g3 = ln_g.reshape(DEPTH * 3, 1, D_MODEL)
    b3 = ln_b.reshape(DEPTH * 3, 1, D_MODEL)
    cos_t, sin_t = _rope_tables()
    dft_ctx = _fourier_tables(SEQ)
    dft_lat = _fourier_tables(DEC_SEQ)
    kr_pad = jnp.pad(cache_mla_krope, ((0, 0), (0, 0), (0, 0), (ROPE_LANE0, LANE - ROPE_LANE0 - MLA_ROPE)))
    cache_k = cache_na_k.reshape(DEC_BATCH, DEPTH, PAST_LEN, NA_W)
    cache_v = cache_na_v.reshape(DEC_BATCH, DEPTH, PAST_LEN, NA_W)

    cvec = jnp.concatenate([c_ctx[None], c, jnp.zeros((8 - 1 - DEC_BATCH, D_MODEL), F32)], axis=0)
    mod = _adaln(cvec, w_ada, b_ada)
    kc, vc = _ctx_kv(cache_mla_ckv, kr_pad, w_uk_p, w_uv)
    na_bias = _na_bias_table(na_rpb)

    x = (x_prompt.reshape(P_TOK, D_MODEL), x_sample.reshape(S_TOK, D_MODEL))
    caches = [jnp.zeros(s, F32) for s in _cache_shapes()]
    for l in range(DEPTH):
        x = _ffn(x, mod, *f1, g3, b3, l, 0)
        u_f, q, k, v, q_n, k_nb, v_nb, *caches = _mixer_in(
            x, mod, cos_t, sin_t, w_in_p, q_norm, w_uq_p, kv_norm, w_uk_p, w_uv, caches, l)
        yf_c = _fourier(u_f, dft_ctx, SEQ, 0, BATCH)
        yf_l = _fourier(u_f, dft_lat, DEC_SEQ, P_TOK // DEC_SEQ, DEC_BATCH)
        ym_c, yn_c = _ctx_attn(q, k, v, q_n, k_nb, v_nb)
        ym_l = _lat_mla(q, k, v, kc, vc, l)
        yn_l = _lat_na(q_n, k_nb, v_nb, cache_k, cache_v, na_bias, l)
        x = _merge(x, mod, (yf_c, ym_c, yn_c), (yf_l, ym_l, yn_l), w_gate_b, b_gate3, w_f, w_m, w_n, w_out_b,
                   g3, b3, l)
        x = _ffn(x, mod, *f2, g3, b3, l, 2)
    ckv, kr_t, nak_t, nav_t = caches
    per_head = lambda a: jnp.transpose(a.reshape(BATCH, DEPTH, NA_HEADS, NA_HEAD_DIM, SEQ), (0, 1, 4, 2, 3))
    return (x[:P_TOK].reshape(BATCH, SEQ, D_MODEL), x[P_TOK:].reshape(DEC_BATCH, DEC_SEQ, D_MODEL),
            ckv, jnp.transpose(kr_t, (0, 1, 3, 2)), per_head(nak_t), per_head(nav_t))
```

```python
import functools

import numpy as np
import jax
import jax.numpy as jnp
from jax import lax
from jax.experimental import pallas as pl
from jax.experimental.pallas import tpu as pltpu

F32 = jnp.float32
BF16 = jnp.bfloat16

D_MODEL = 1024
BATCH = 32
SEQ = 256
DEPTH = 4
DEC_BATCH = 2
DEC_SEQ = 2048
PAST_LEN = 512
GRID_W = 64
GRID_H = DEC_SEQ // GRID_W
D_FF = 2816
F_GROUPS = 4
F_GC = 128
F_W = F_GROUPS * F_GC
MLA_HEADS = 8
MLA_Q_LORA = 384
MLA_KV_LORA = 256
MLA_NOPE = 64
MLA_ROPE = 32
MLA_V = 64
NA_HEADS = 8
NA_HEAD_DIM = 64
NA_KH = 8
NA_KW = 16
NA_W = NA_HEADS * NA_HEAD_DIM
ROPE_BASE = 10000.0
AXIS_DIM = MLA_ROPE // 2
ALPHA = (2.0 * DEPTH) ** 0.25
MLA_SCALE = (MLA_NOPE + MLA_ROPE) ** -0.5
NA_SCALE = NA_HEAD_DIM ** -0.5
NEG_INF = -1e30

LANE = 128
MLA_HP = LANE
MLA_QW = MLA_HEADS * MLA_HP
P_TOK = BATCH * SEQ
S_TOK = DEC_BATCH * DEC_SEQ
TOKENS = P_TOK + S_TOK
TM = 512
SIDE_ROWS = 64
N_TILES = TOKENS // TM
P_TILES = P_TOK // TM
S_TILES_PER_REQ = DEC_SEQ // TM
MXU_TILE = 256
FF_CHUNKS = (0, 4 * MXU_TILE, 8 * MXU_TILE, D_FF)
U_Q0 = F_W
U_KV0 = U_Q0 + MLA_Q_LORA
U_KR0 = U_KV0 + MLA_KV_LORA
U_NA0 = U_KR0 + LANE
U_W = U_NA0 + 3 * NA_W
ROPE_LANE0 = MLA_NOPE
NA_QROWS = 4
NA_QB = NA_QROWS * GRID_W
NA_WIN_ROWS = NA_QROWS + NA_KH
NA_WIN = NA_WIN_ROWS * GRID_W
NA_BLOCKS = GRID_H // NA_QROWS
MLA_QB = 256
FOURIER_ROWS = 1024
VMEM_LIMIT = 56 * 1024 * 1024
FFN_VMEM_LIMIT = 60 * 1024 * 1024
UP_STAGE_ROWS = 128
DOWN_STAGE_ROWS = D_FF // 8


def _group(i):
    return jnp.where(i < P_TILES, 0, 1 + (i - P_TILES) // S_TILES_PER_REQ)


def _rope_block(i):
    return jnp.where(i < P_TILES, 0, 1 + (i - P_TILES) % S_TILES_PER_REQ)


def _ln(x, eps):
    mu = jnp.mean(x, axis=-1, keepdims=True)
    xc = x - mu
    var = jnp.mean(xc * xc, axis=-1, keepdims=True)
    return xc * lax.rsqrt(var + eps)


def _rms(x, g):
    return x * lax.rsqrt(jnp.mean(x * x, axis=-1, keepdims=True) + 1e-6) * g


def _dot(a, b):
    return jnp.dot(a, b, preferred_element_type=F32)


def _dot_nt(a, b):
    return lax.dot_general(a, b, (((1,), (1,)), ((), ())), preferred_element_type=F32)


def _mod_rows(mod_ref, i):
    m = mod_ref[pl.ds(_group(i), 1), :]
    return m[:, :D_MODEL], m[:, D_MODEL:2 * D_MODEL], m[:, 2 * D_MODEL:]


def _rope(x, cos, sin):
    lane = lax.broadcasted_iota(jnp.int32, x.shape, 1)
    first_half = (lane % AXIS_DIM) < (AXIS_DIM // 2)
    rot = jnp.where(first_half, -pltpu.roll(x, LANE - AXIS_DIM // 2, 1), pltpu.roll(x, AXIS_DIM // 2, 1))
    return x * cos + rot * sin


def _adaln_kernel(c_ref, w_ref, b_ref, o_ref):
    c = c_ref[...]
    s = (c * jax.nn.sigmoid(c)).astype(BF16)
    o_ref[...] = _dot(s, w_ref[...].astype(BF16)) + b_ref[...]


def _adaln(cvec, w_ada, b_ada):
    n_col = 9 * D_MODEL // D_MODEL
    return pl.pallas_call(
        _adaln_kernel,
        grid=(DEPTH, n_col),
        in_specs=[pl.BlockSpec((8, D_MODEL), lambda l, j: (0, 0)),
                  pl.BlockSpec((None, D_MODEL, D_MODEL), lambda l, j: (l, 0, j)),
                  pl.BlockSpec((None, 1, D_MODEL), lambda l, j: (l, 0, j))],
        out_specs=pl.BlockSpec((None, 8, D_MODEL), lambda l, j: (l, 0, j)),
        out_shape=jax.ShapeDtypeStruct((DEPTH, 8, 9 * D_MODEL), F32),
        name="adaln",
    )(cvec, w_ada, b_ada.reshape(DEPTH, 1, 9 * D_MODEL))


def _ffn_kernel(*refs, layer, split_in, split_out):
    n_x = 4 if split_in else 2
    n_o = 2 if split_out else 1
    x_refs = refs[:n_x]
    mod_ref, w1_hbm, w3_hbm, w2_hbm, g_ref, b_ref = refs[n_x:n_x + 6]
    o_refs = refs[n_x + 6:n_x + 6 + n_o]
    h_scr, y_scr, w1_ref, w3_ref, w2_ref, stage_up, stage_down, sem = refs[n_x + 6 + n_o:]
    s = pl.program_id(0)
    cur = s % 2
    nxt = 1 - cur
    pieces = [pl.ds(r, SIDE_ROWS) for r in range(0, TM, SIDE_ROWS)]

    def x_tile(which, tile):
        if not split_in:
            return lambda rows: x_refs[which][rows, :]
        ctx_ref, lat_ref = x_refs[2 * which:2 * which + 2]
        return lambda rows: jnp.where(tile < P_TILES, ctx_ref[rows, :], lat_ref[rows, :])

    x_prev = x_tile(0, s - 1)
    x_next = x_tile(1, s + 1)

    def modulate(x_rows, tile, slot):
        shift, scale, _ = _mod_rows(mod_ref, tile)

        def piece(rows):
            h_scr[slot, rows, :] = (_ln(x_rows(rows), 1e-6) * (1.0 + scale) + shift).astype(BF16)
        return [functools.partial(piece, rows) for rows in pieces]

    def finish():
        _, _, gate = _mod_rows(mod_ref, s - 1)

        def piece(rows):
            z = ALPHA * x_prev(rows) + (0.5 * gate) * y_scr[nxt, rows, :]
            out = _ln(z, 1e-5) * g_ref[...] + b_ref[...]
            if split_out:
                to_ctx = jnp.broadcast_to(s - 1 < P_TILES, out.shape)
                pltpu.store(o_refs[0].at[rows, :], out, mask=to_ctx)
                pltpu.store(o_refs[1].at[rows, :], out, mask=jnp.logical_not(to_ctx))
            else:
                o_refs[0][rows, :] = out
            y_scr[nxt, rows, :] = out
        return [functools.partial(piece, rows) for rows in pieces]

    def load_weights():
        chunks = []
        for src, dst in ((w1_hbm, w1_ref), (w3_hbm, w3_ref)):
            for r in range(0, D_MODEL, UP_STAGE_ROWS):
                rows = pl.ds(r, UP_STAGE_ROWS)
                chunks.append((src.at[layer, rows, :], stage_up, dst.at[rows, :]))
        for r in range(0, D_FF, DOWN_STAGE_ROWS):
            rows = pl.ds(r, DOWN_STAGE_ROWS)
            chunks.append((w2_hbm.at[layer, rows, :], stage_down, w2_ref.at[rows, :]))

        def copy(k):
            src, stage, _ = chunks[k]
            return pltpu.make_async_copy(src, stage.at[k % 2], sem.at[k % 2])

        copy(0).start()
        for k, (_, stage, dst) in enumerate(chunks):
            if k + 1 < len(chunks):
                copy(k + 1).start()
            copy(k).wait()
            dst[...] = stage[k % 2].astype(BF16)

    def matmuls(side_work):
        side_work = list(side_work)
        chunks = [slice(c0, c1) for c0, c1 in zip(FF_CHUNKS[:-1], FF_CHUNKS[1:])]
        per_dot = -(-len(side_work) // (3 * len(chunks)))

        def dot_with_side(lhs, rhs):
            for _ in range(min(per_dot, len(side_work))):
                side_work.pop(0)()
            return _dot(lhs() if callable(lhs) else lhs, rhs)

        h = lambda: h_scr[cur]

        def gate_up(sl):
            a = dot_with_side(h, w1_ref[:, sl])
            return (a * jax.nn.sigmoid(a) * dot_with_side(h, w3_ref[:, sl])).astype(BF16)

        t_next = gate_up(chunks[0])
        for n, sl in enumerate(chunks):
            t = t_next
            if n + 1 < len(chunks):
                t_next = gate_up(chunks[n + 1])
            yc = dot_with_side(t, w2_ref[sl, :])
            if n == 0:
                y_scr[cur] = yc
            else:
                y_scr[cur] += yc
        for work in side_work:
            work()

    @pl.when(s == 0)
    def _():
        load_weights()
        for work in modulate(x_tile(0, s), s, cur):
            work()
        matmuls(modulate(x_next, s + 1, nxt))

    @pl.when(jnp.logical_and(s > 0, s < N_TILES))
    def _():
        matmuls(finish() + modulate(x_next, s + 1, nxt))

    @pl.when(s == N_TILES)
    def _():
        for work in finish():
            work()


def _resident(shape, index_map):
    return pl.BlockSpec(shape, index_map, pipeline_mode=pl.Buffered(1))


def _ffn(x, mod, w1, w3, w2, ln_g, ln_b, l, sub, split_out=False):
    split_in = isinstance(x, tuple)
    tile = lambda lo, hi, off: pl.BlockSpec((TM, D_MODEL), lambda i: (jnp.clip(i + off, lo, hi) - lo, 0))
    ctx_lat = lambda off: [tile(0, P_TILES - 1, off), tile(P_TILES, N_TILES - 1, off)]
    if split_in:
        x_args = [x[0], x[1]] * 2
        x_specs = ctx_lat(-1) + ctx_lat(1)
    else:
        x_args = [x, x]
        x_specs = [tile(0, N_TILES - 1, -1), tile(0, N_TILES - 1, 1)]
    if split_out:
        out_specs = ctx_lat(-1)
        out_shape = [jax.ShapeDtypeStruct((P_TOK, D_MODEL), F32), jax.ShapeDtypeStruct((S_TOK, D_MODEL), F32)]
    else:
        out_specs = tile(0, N_TILES - 1, -1)
        out_shape = jax.ShapeDtypeStruct((TOKENS, D_MODEL), F32)
    return pl.pallas_call(
        functools.partial(_ffn_kernel, layer=l, split_in=split_in, split_out=split_out),
        grid=(N_TILES + 1,),
        in_specs=x_specs + [
                  _resident((None, 8, 3 * D_MODEL), lambda i: (l, 0, sub)),
                  pl.BlockSpec(memory_space=pl.ANY), pl.BlockSpec(memory_space=pl.ANY),
                  pl.BlockSpec(memory_space=pl.ANY),
                  _resident((None, 1, D_MODEL), lambda i: (3 * l + sub, 0, 0)),
                  _resident((None, 1, D_MODEL), lambda i: (3 * l + sub, 0, 0))],
        out_specs=out_specs,
        out_shape=out_shape,
        scratch_shapes=[pltpu.VMEM((2, TM, D_MODEL), BF16), pltpu.VMEM((2, TM, D_MODEL), F32),
                        pltpu.VMEM((D_MODEL, D_FF), BF16), pltpu.VMEM((D_MODEL, D_FF), BF16),
                        pltpu.VMEM((D_FF, D_MODEL), BF16),
                        pltpu.VMEM((2, UP_STAGE_ROWS, D_FF), F32), pltpu.VMEM((2, DOWN_STAGE_ROWS, D_MODEL), F32),
                        pltpu.SemaphoreType.DMA((2,))],
        compiler_params=pltpu.CompilerParams(dimension_semantics=("arbitrary",),
                                             vmem_limit_bytes=FFN_VMEM_LIMIT),
        name="ffn",
    )(*x_args, mod, w1, w3, w2, ln_g, ln_b)


def _mixer_in_kernel(x_ref, mod_ref, cos_ref, sin_ref, w_in_ref, qn_ref, w_uq_ref, kvn_ref, w_uk_ref, w_uv_ref,
                     ckv_prev, kr_prev, kna_prev, vna_prev,
                     uf_ref, q_ref, k_ref, v_ref, qna_ref, knab_ref, vnab_ref, ckv_ref, kr_ref, kna_ref, vna_ref):
    del ckv_prev, kr_prev, kna_prev, vna_prev
    i = pl.program_id(0)
    shift, scale, _ = _mod_rows(mod_ref, i)
    h = (_ln(x_ref[...], 1e-6) * (1.0 + scale) + shift).astype(BF16)
    cos = cos_ref[...]
    sin = sin_ref[...]

    uf_ref[...] = _dot(h, w_in_ref[:, :U_Q0])

    u_q = _dot(h, w_in_ref[:, U_Q0:U_KV0])
    q = _dot(_rms(u_q, qn_ref[...]).astype(BF16), w_uq_ref[...])
    for hd in range(MLA_HEADS):
        sl = slice(hd * MLA_HP, (hd + 1) * MLA_HP)
        q_ref[:, sl] = _rope(q[:, sl], cos, sin).astype(BF16)

    c_kv = _rms(_dot(h, w_in_ref[:, U_KV0:U_KR0]), kvn_ref[...])
    kr = _dot(h, w_in_ref[:, U_KR0:U_NA0])
    k_na = _dot(h, w_in_ref[:, U_NA0 + NA_W:U_NA0 + 2 * NA_W])
    v_na = _dot(h, w_in_ref[:, U_NA0 + 2 * NA_W:])

    @pl.when(i < P_TILES)
    def _():
        kr_t = kr.T[ROPE_LANE0:ROPE_LANE0 + MLA_ROPE, :]
        kna_t = k_na.T
        vna_t = v_na.T
        for b in range(TM // SEQ):
            rows = slice(b * SEQ, (b + 1) * SEQ)
            ckv_ref[b] = c_kv[rows, :]
            kr_ref[b] = kr_t[:, rows]
            kna_ref[b] = kna_t[:, rows]
            vna_ref[b] = vna_t[:, rows]

    c_kv = c_kv.astype(BF16)
    kr = _rope(kr, cos, sin)
    k = _dot(c_kv, w_uk_ref[...])
    for hd in range(MLA_HEADS):
        sl = slice(hd * MLA_HP, (hd + 1) * MLA_HP)
        k_ref[:, sl] = (k[:, sl] + kr).astype(BF16)
    v_ref[...] = _dot(c_kv, w_uv_ref[...]).astype(BF16)

    qna_ref[...] = _dot(h, w_in_ref[:, U_NA0:U_NA0 + NA_W]).astype(BF16)
    knab_ref[...] = k_na.astype(BF16)
    vnab_ref[...] = v_na.astype(BF16)


def _cache_shapes():
    return [(BATCH, DEPTH, SEQ, MLA_KV_LORA), (BATCH, DEPTH, MLA_ROPE, SEQ),
            (BATCH, DEPTH, NA_W, SEQ), (BATCH, DEPTH, NA_W, SEQ)]


def _mixer_in(x, mod, cos_t, sin_t, w_in, q_norm, w_uq, kv_norm, w_uk, w_uv, caches, l):
    tile = lambda w: pl.BlockSpec((TM, w), lambda i: (i, 0))
    acts = [(F_W, F32), (MLA_QW, BF16), (MLA_QW, BF16), (MLA_HEADS * MLA_V, BF16), (NA_W, BF16), (NA_W, BF16),
            (NA_W, BF16)]
    cache_spec = lambda s: pl.BlockSpec((TM // SEQ, None) + s[2:],
                                        lambda i: (jnp.minimum(i, P_TILES - 1), l, 0, 0))
    n_in = 10
    return pl.pallas_call(
        _mixer_in_kernel,
        grid=(N_TILES,),
        in_specs=[tile(D_MODEL),
                  _resident((None, 8, 3 * D_MODEL), lambda i: (l, 0, 1)),
                  pl.BlockSpec((TM, LANE), lambda i: (_rope_block(i), 0)),
                  pl.BlockSpec((TM, LANE), lambda i: (_rope_block(i), 0)),
                  _resident((None, D_MODEL, U_W), lambda i: (l, 0, 0)),
                  _resident((None, 1, MLA_Q_LORA), lambda i: (l, 0, 0)),
                  _resident((None, MLA_Q_LORA, MLA_QW), lambda i: (l, 0, 0)),
                  _resident((None, 1, MLA_KV_LORA), lambda i: (l, 0, 0)),
                  _resident((None, MLA_KV_LORA, MLA_QW), lambda i: (l, 0, 0)),
                  _resident((None, MLA_KV_LORA, MLA_HEADS * MLA_V), lambda i: (l, 0, 0))]
                 + [pl.BlockSpec(memory_space=pl.ANY)] * len(caches),
        out_specs=[tile(w) for w, _ in acts] + [cache_spec(s) for s in _cache_shapes()],
        out_shape=[jax.ShapeDtypeStruct((TOKENS, w), dt) for w, dt in acts]
                  + [jax.ShapeDtypeStruct(s, F32) for s in _cache_shapes()],
        input_output_aliases={n_in + n: len(acts) + n for n in range(len(caches))},
        compiler_params=pltpu.CompilerParams(dimension_semantics=("arbitrary",),
                                             vmem_limit_bytes=VMEM_LIMIT),
        name="mixer_in",
    )(x, mod, cos_t, sin_t, w_in, q_norm, w_uq, kv_norm, w_uk, w_uv, *caches)


def _ctx_kv_kernel(ckv_ref, kr_ref, w_uk_ref, w_uv_ref, k_ref, v_ref):
    c = ckv_ref[...].astype(BF16)
    k = _dot(c, w_uk_ref[...])
    kr = kr_ref[...]
    for hd in range(MLA_HEADS):
        sl = slice(hd * MLA_HP, (hd + 1) * MLA_HP)
        k_ref[:, sl] = (k[:, sl] + kr).astype(BF16)
    v_ref[...] = _dot(c, w_uv_ref[...]).astype(BF16)


def _ctx_kv(cache_ckv, cache_kr_pad, w_uk, w_uv):
    return pl.pallas_call(
        _ctx_kv_kernel,
        grid=(DEC_BATCH, DEPTH),
        in_specs=[pl.BlockSpec((None, None, PAST_LEN, MLA_KV_LORA), lambda b, l: (b, l, 0, 0)),
                  pl.BlockSpec((None, None, PAST_LEN, LANE), lambda b, l: (b, l, 0, 0)),
                  pl.BlockSpec((None, MLA_KV_LORA, MLA_QW), lambda b, l: (l, 0, 0)),
                  pl.BlockSpec((None, MLA_KV_LORA, MLA_HEADS * MLA_V), lambda b, l: (l, 0, 0))],
        out_specs=[pl.BlockSpec((None, None, PAST_LEN, MLA_QW), lambda b, l: (b, l, 0, 0)),
                   pl.BlockSpec((None, None, PAST_LEN, MLA_HEADS * MLA_V), lambda b, l: (b, l, 0, 0))],
        out_shape=[jax.ShapeDtypeStruct((DEC_BATCH, DEPTH, PAST_LEN, MLA_QW), BF16),
                   jax.ShapeDtypeStruct((DEC_BATCH, DEPTH, PAST_LEN, MLA_HEADS * MLA_V), BF16)],
        name="ctx_kv",
    )(cache_ckv, cache_kr_pad, w_uk, w_uv)


def _fourier_kernel(x_ref, cs_ref, cl_ref, sl_ref, o_ref):
    length = cl_ref.shape[0]
    x = x_ref[...].astype(BF16)
    xc, xs = [], []
    for g in range(F_GROUPS):
        t = _dot(x[:, g * F_GC:(g + 1) * F_GC], cs_ref[...])
        xc.append(t[:, :F_GC])
        xs.append(t[:, F_GC:])
    xc = jnp.concatenate(xc, axis=1).astype(BF16)
    xs = jnp.concatenate(xs, axis=1).astype(BF16)
    for r in range(0, x_ref.shape[0], length):
        rows = slice(r, r + length)
        o_ref[rows, :] = (_dot(cl_ref[...], xc[rows]) - _dot(sl_ref[...], xs[rows])).astype(BF16)


def _dft_tables(n):
    k = np.arange(n, dtype=np.int64)
    ang = 2.0 * np.pi * ((k[:, None] * k[None, :]) % n).astype(np.float64) / n
    s = n ** -0.5
    return np.cos(ang) * s, np.sin(ang) * s


def _fourier_tables(length):
    cc, sc = _dft_tables(F_GC)
    cl, sl = _dft_tables(length)
    as_bf16 = lambda a: jnp.asarray(a, F32).astype(BF16)
    return as_bf16(np.concatenate([cc, sc], axis=1)), as_bf16(cl), as_bf16(sl)


def _fourier(u_f, tables, length, first_block, n_blocks):
    cs, cl, sl = tables
    rows = max(length, FOURIER_ROWS)
    assert rows % length == 0 and (n_blocks * length) % rows == 0 and (first_block * length) % rows == 0
    first = first_block * length // rows
    return pl.pallas_call(
        _fourier_kernel,
        grid=(n_blocks * length // rows,),
        in_specs=[pl.BlockSpec((rows, F_W), lambda b: (first + b, 0)),
                  _resident((F_GC, 2 * F_GC), lambda b: (0, 0)),
                  _resident((length, length), lambda b: (0, 0)),
                  _resident((length, length), lambda b: (0, 0))],
        out_specs=pl.BlockSpec((rows, F_W), lambda b: (b, 0)),
        out_shape=jax.ShapeDtypeStruct((n_blocks * length, F_W), BF16),
        compiler_params=pltpu.CompilerParams(dimension_semantics=("arbitrary",),
                                             vmem_limit_bytes=VMEM_LIMIT),
        name="fourier_%d" % length,
    )(u_f, cs, cl, sl)


LOG2E = 1.4426950408889634


def _softmax_pv(scores, values_ext, scale):
    m = None
    for s in scores:
        sm = jnp.max(s, axis=-1, keepdims=True)
        m = sm if m is None else jnp.maximum(m, sm)
    acc = None
    for s, v in zip(scores, values_ext):
        p = jnp.exp2((s - m) * (scale * LOG2E)).astype(BF16)
        pv = _dot(p, v)
        acc = pv if acc is None else acc + pv
    return acc[:, :LANE] / acc[:, LANE:]


def _with_ones(v_pair):
    return jnp.concatenate([v_pair, jnp.ones_like(v_pair)], axis=1)


def _half_masks(rows):
    low = lax.broadcasted_iota(jnp.int32, (rows, LANE), 1) < LANE // 2
    return low, jnp.logical_not(low)


def _head_of_pair(x_pair, mask):
    return jnp.where(mask, x_pair, jnp.zeros_like(x_pair))


def _ctx_attn_kernel(q_ref, k_ref, v_ref, qn_ref, kn_ref, vn_ref, om_ref, on_ref):
    masks = _half_masks(SEQ)
    for pair in range(MLA_HEADS // 2):
        ps = slice(pair * LANE, (pair + 1) * LANE)
        v_ext = _with_ones(v_ref[:, ps])
        o = []
        for half in range(2):
            sl = slice((2 * pair + half) * MLA_HP, (2 * pair + half + 1) * MLA_HP)
            s = _dot_nt(q_ref[:, sl], k_ref[:, sl])
            o.append(_softmax_pv([s], [v_ext], MLA_SCALE))
        om_ref[:, ps] = jnp.where(masks[0], o[0], o[1]).astype(BF16)
    for pair in range(NA_HEADS // 2):
        ps = slice(pair * LANE, (pair + 1) * LANE)
        v_ext = _with_ones(vn_ref[:, ps])
        q_pair = qn_ref[:, ps]
        k_pair = kn_ref[:, ps]
        o = []
        for half in range(2):
            s = _dot_nt(_head_of_pair(q_pair, masks[half]), k_pair)
            o.append(_softmax_pv([s], [v_ext], NA_SCALE))
        on_ref[:, ps] = jnp.where(masks[0], o[0], o[1]).astype(BF16)


def _ctx_attn(q, k, v, qn, kn, vn):
    blk = lambda w: pl.BlockSpec((SEQ, w), lambda b: (b, 0))
    return pl.pallas_call(
        _ctx_attn_kernel,
        grid=(BATCH,),
        in_specs=[blk(MLA_QW), blk(MLA_QW), blk(MLA_HEADS * MLA_V), blk(NA_W), blk(NA_W), blk(NA_W)],
        out_specs=[blk(MLA_HEADS * MLA_V), blk(NA_W)],
        out_shape=[jax.ShapeDtypeStruct((P_TOK, MLA_HEADS * MLA_V), BF16),
                   jax.ShapeDtypeStruct((P_TOK, NA_W), BF16)],
        compiler_params=pltpu.CompilerParams(dimension_semantics=("arbitrary",)),
        name="ctx_attn",
    )(q, k, v, qn, kn, vn)


def _lat_mla_kernel(q_ref, k_ref, v_ref, kc_ref, vc_ref, o_ref):
    low, _ = _half_masks(MLA_QB)
    for pair in range(MLA_HEADS // 2):
        ps = slice(pair * LANE, (pair + 1) * LANE)
        v_ext = [_with_ones(v_ref[:, ps]), _with_ones(vc_ref[:, ps])]
        o = []
        for half in range(2):
            sl = slice((2 * pair + half) * MLA_HP, (2 * pair + half + 1) * MLA_HP)
            q = q_ref[:, sl]
            o.append(_softmax_pv([_dot_nt(q, k_ref[:, sl]), _dot_nt(q, kc_ref[:, sl])], v_ext, MLA_SCALE))
        o_ref[:, ps] = jnp.where(low, o[0], o[1]).astype(BF16)


def _lat_mla(q, k, v, kc, vc, l):
    nq = DEC_SEQ // MLA_QB
    q0 = P_TOK // MLA_QB
    r0 = P_TOK // DEC_SEQ
    return pl.pallas_call(
        _lat_mla_kernel,
        grid=(DEC_BATCH, nq),
        in_specs=[pl.BlockSpec((MLA_QB, MLA_QW), lambda b, j: (q0 + b * nq + j, 0)),
                  pl.BlockSpec((DEC_SEQ, MLA_QW), lambda b, j: (r0 + b, 0)),
                  pl.BlockSpec((DEC_SEQ, MLA_HEADS * MLA_V), lambda b, j: (r0 + b, 0)),
                  pl.BlockSpec((None, None, PAST_LEN, MLA_QW), lambda b, j: (b, l, 0, 0)),
                  pl.BlockSpec((None, None, PAST_LEN, MLA_HEADS * MLA_V), lambda b, j: (b, l, 0, 0))],
        out_specs=pl.BlockSpec((MLA_QB, MLA_HEADS * MLA_V), lambda b, j: (b * nq + j, 0)),
        out_shape=jax.ShapeDtypeStruct((S_TOK, MLA_HEADS * MLA_V), BF16),
        compiler_params=pltpu.CompilerParams(dimension_semantics=("arbitrary", "arbitrary"),
                                             vmem_limit_bytes=VMEM_LIMIT),
        name="lat_mla",
    )(q, k, v, kc, vc)


def _na_window_block(j):
    return jnp.clip(j - 1, 0, NA_BLOCKS - NA_WIN_ROWS // NA_QROWS)


def _lat_na_kernel(q_ref, k_ref, v_ref, kc_ref, vc_ref, bias_ref, o_ref):
    j = pl.program_id(1)
    start = pl.multiple_of(_na_window_block(j) * NA_QB, NA_QB)
    k_win = k_ref[pl.ds(start, NA_WIN), :]
    v_win = v_ref[pl.ds(start, NA_WIN), :]
    masks = _half_masks(NA_QB)
    for pair in range(NA_HEADS // 2):
        ps = slice(pair * LANE, (pair + 1) * LANE)
        k_loc = k_win[:, ps]
        k_ctx = kc_ref[:, ps].astype(BF16)
        v_ext = [_with_ones(v_win[:, ps]), _with_ones(vc_ref[:, ps].astype(BF16))]
        q_pair = q_ref[:, ps]
        o = []
        for half in range(2):
            q = _head_of_pair(q_pair, masks[half])
            s_loc = _dot_nt(q, k_loc) + bias_ref[2 * pair + half]
            o.append(_softmax_pv([s_loc, _dot_nt(q, k_ctx)], v_ext, NA_SCALE))
        o_ref[:, ps] = jnp.where(masks[0], o[0], o[1]).astype(BF16)


NA_BLOCK_KINDS = (0, 1, NA_BLOCKS - 1)
NA_DIAG = GRID_W - 1
NA_N_DR = 2 * NA_KH - 1


def _na_bias_kernel(e_ref, o_ref):
    lane = lax.broadcasted_iota(jnp.int32, (GRID_W, LANE), 1)
    q_col = lax.broadcasted_iota(jnp.int32, (GRID_W, LANE), 0)
    k_col = lane % GRID_W
    col_start = jnp.clip(q_col - NA_KW // 2, 0, GRID_W - NA_KW)
    col_ok = (k_col >= col_start) & (k_col < col_start + NA_KW)
    low_half = lane < GRID_W
    neg = jnp.full((GRID_W, LANE), NEG_INF, F32)
    tiles = {}

    def toeplitz(dr, half):
        if (dr, half) not in tiles:
            row = jnp.broadcast_to(e_ref[dr:dr + 1, :], (GRID_W, LANE))
            shift = (LANE - NA_DIAG + half * GRID_W) % LANE
            tiles[dr, half] = pltpu.roll(row, shift, 1, stride=1, stride_axis=0)
        return tiles[dr, half]

    for kind, blk in enumerate(NA_BLOCK_KINDS):
        first_key_row = NA_QROWS * min(max(blk - 1, 0), NA_BLOCKS - NA_WIN_ROWS // NA_QROWS)
        for qr in range(NA_QROWS):
            q_row = NA_QROWS * blk + qr
            row_start = min(max(q_row - NA_KH // 2, 0), GRID_H - NA_KH)
            for pair in range(NA_WIN_ROWS // 2):
                halves = []
                for half in range(2):
                    k_row = first_key_row + 2 * pair + half
                    inside = row_start <= k_row < row_start + NA_KH
                    halves.append(toeplitz(k_row - q_row + NA_KH - 1, half) if inside else neg)
                tile = jnp.where(low_half, halves[0], halves[1])
                o_ref[kind, qr * GRID_W:(qr + 1) * GRID_W, pair * LANE:(pair + 1) * LANE] = (
                    jnp.where(col_ok, tile, NEG_INF))


def _na_bias_table(rpb):
    left = NA_DIAG - (NA_KW - 1)
    e = jnp.concatenate([jnp.broadcast_to(rpb[..., :1], rpb.shape[:-1] + (left,)), rpb,
                         jnp.broadcast_to(rpb[..., -1:], rpb.shape[:-1] + (LANE - left - rpb.shape[-1],))],
                        axis=-1) * (1.0 / NA_SCALE)
    return pl.pallas_call(
        _na_bias_kernel,
        grid=(DEPTH, NA_HEADS),
        in_specs=[pl.BlockSpec((None, None, NA_N_DR, LANE), lambda l, h: (l, h, 0, 0))],
        out_specs=pl.BlockSpec((None, len(NA_BLOCK_KINDS), None, NA_QB, NA_WIN), lambda l, h: (l, 0, h, 0, 0)),
        out_shape=jax.ShapeDtypeStruct((DEPTH, len(NA_BLOCK_KINDS), NA_HEADS, NA_QB, NA_WIN), F32),
        name="na_bias",
    )(e)


def _lat_na(q, k, v, cache_k, cache_v, bias, l):
    q0 = P_TOK // NA_QB
    r0 = P_TOK // DEC_SEQ
    kind = lambda j: (j > 0).astype(jnp.int32) + (j == NA_BLOCKS - 1).astype(jnp.int32)
    return pl.pallas_call(
        _lat_na_kernel,
        grid=(DEC_BATCH, NA_BLOCKS),
        in_specs=[pl.BlockSpec((NA_QB, NA_W), lambda b, j: (q0 + b * NA_BLOCKS + j, 0)),
                  pl.BlockSpec((DEC_SEQ, NA_W), lambda b, j: (r0 + b, 0)),
                  pl.BlockSpec((DEC_SEQ, NA_W), lambda b, j: (r0 + b, 0)),
                  pl.BlockSpec((None, None, PAST_LEN, NA_W), lambda b, j: (b, l, 0, 0)),
                  pl.BlockSpec((None, None, PAST_LEN, NA_W), lambda b, j: (b, l, 0, 0)),
                  pl.BlockSpec((None, None, NA_HEADS, NA_QB, NA_WIN), lambda b, j: (l, kind(j), 0, 0, 0))],
        out_specs=pl.BlockSpec((NA_QB, NA_W), lambda b, j: (b * NA_BLOCKS + j, 0)),
        out_shape=jax.ShapeDtypeStruct((S_TOK, NA_W), BF16),
        compiler_params=pltpu.CompilerParams(dimension_semantics=("arbitrary", "arbitrary"),
                                             vmem_limit_bytes=VMEM_LIMIT),
        name="lat_na",
    )(q, k, v, cache_k, cache_v, bias)


def _merge_kernel(x_ref, mod_ref, yfc_ref, ymc_ref, ync_ref, yfl_ref, yml_ref, ynl_ref,
                  wg_ref, bg_ref, wf_ref, wm_ref, wn_ref, wo_ref, g_ref, b_ref, o_ref):
    i = pl.program_id(0)
    shift, scale, gate = _mod_rows(mod_ref, i)
    x = x_ref[...]
    h = (_ln(x, 1e-6) * (1.0 + scale) + shift).astype(BF16)
    is_ctx = i < P_TILES
    mix = None
    for n, (yc_ref, yl_ref, w_ref) in enumerate(((yfc_ref, yfl_ref, wf_ref), (ymc_ref, yml_ref, wm_ref),
                                                 (ync_ref, ynl_ref, wn_ref))):
        sl = slice(n * D_MODEL, (n + 1) * D_MODEL)
        g = jax.nn.sigmoid(_dot(h, wg_ref[:, sl]) + bg_ref[:, sl])
        y = jnp.where(is_ctx, yc_ref[...], yl_ref[...])
        t = g * _dot(y, w_ref[...])
        mix = t if mix is None else mix + t
    z = ALPHA * x + gate * _dot(mix.astype(BF16), wo_ref[...])
    o_ref[...] = _ln(z, 1e-5) * g_ref[...] + b_ref[...]


def _merge(x, mod, y_ctx, y_lat, w_gate, b_gate, w_f, w_m, w_n, w_out, ln_g, ln_b, l):
    tile = lambda w: pl.BlockSpec((TM, w), lambda i: (i, 0))
    ctx_tile = lambda w: pl.BlockSpec((TM, w), lambda i: (jnp.minimum(i, P_TILES - 1), 0))
    lat_tile = lambda w: pl.BlockSpec((TM, w), lambda i: (jnp.maximum(i - P_TILES, 0), 0))
    widths = (F_W, MLA_HEADS * MLA_V, NA_W)
    return pl.pallas_call(
        _merge_kernel,
        grid=(N_TILES,),
        in_specs=[tile(D_MODEL),
                  _resident((None, 8, 3 * D_MODEL), lambda i: (l, 0, 1))]
                 + [ctx_tile(w) for w in widths] + [lat_tile(w) for w in widths] + [
                  _resident((None, D_MODEL, 3 * D_MODEL), lambda i: (l, 0, 0)),
                  _resident((None, 1, 3 * D_MODEL), lambda i: (l, 0, 0)),
                  _resident((None, F_W, D_MODEL), lambda i: (l, 0, 0)),
                  _resident((None, MLA_HEADS * MLA_V, D_MODEL), lambda i: (l, 0, 0)),
                  _resident((None, NA_W, D_MODEL), lambda i: (l, 0, 0)),
                  _resident((None, D_MODEL, D_MODEL), lambda i: (l, 0, 0)),
                  _resident((None, 1, D_MODEL), lambda i: (3 * l + 1, 0, 0)),
                  _resident((None, 1, D_MODEL), lambda i: (3 * l + 1, 0, 0))],
        out_specs=tile(D_MODEL),
        out_shape=jax.ShapeDtypeStruct((TOKENS, D_MODEL), F32),
        compiler_params=pltpu.CompilerParams(dimension_semantics=("arbitrary",),
                                             vmem_limit_bytes=VMEM_LIMIT),
        name="merge",
    )(x, mod, *y_ctx, *y_lat, w_gate, b_gate, w_f, w_m, w_n, w_out, ln_g, ln_b)


def _pad_heads(w, n_heads, width):
    lead = w.shape[:-1]
    w = w.reshape(lead + (n_heads, width))
    w = jnp.pad(w, [(0, 0)] * len(lead) + [(0, 0), (0, MLA_HP - width)])
    return w.reshape(lead + (n_heads * MLA_HP,))


def _rope_tables():
    t = jnp.arange(DEC_SEQ, dtype=jnp.int32)
    pos = jnp.stack([t // GRID_W, t % GRID_W], axis=-1).astype(F32)
    half = AXIS_DIM // 2
    inv_freq = ROPE_BASE ** (-jnp.arange(half, dtype=F32) / half)
    ang = pos[:, :, None] * inv_freq
    ang = jnp.concatenate([ang, ang], axis=-1).reshape(DEC_SEQ, MLA_ROPE)
    pad = lambda a, fill: jnp.pad(a, ((0, 0), (ROPE_LANE0, LANE - ROPE_LANE0 - MLA_ROPE)), constant_values=fill)
    cos = jnp.concatenate([jnp.ones((TM, LANE), F32), pad(jnp.cos(ang), 1.0)], axis=0)
    sin = jnp.concatenate([jnp.zeros((TM, LANE), F32), pad(jnp.sin(ang), 0.0)], axis=0)
    return cos, sin


def kernel(x_prompt, x_sample, cache_mla_ckv, cache_mla_krope, cache_na_k, cache_na_v, c, c_ctx, w_ada, b_ada, ffn1_w1, ffn1_w3, ffn1_w2, ffn2_w1, ffn2_w3, ffn2_w2, w_in, mla_q_norm, mla_w_uq, mla_kv_norm, mla_w_ukv, na_rpb, w_branch_f, w_branch_m, w_branch_n, w_gate, b_gate, w_out, ln_g, ln_b):
    bf = lambda w: w.astype(BF16)
    f1 = (ffn1_w1, ffn1_w3, ffn1_w2)
    f2 = (ffn2_w1, ffn2_w3, ffn2_w2)
    w_in_p = bf(jnp.concatenate(
        [w_in[..., :U_KR0],
         jnp.pad(w_in[..., U_KR0:U_KR0 + MLA_ROPE], ((0, 0), (0, 0), (ROPE_LANE0, LANE - ROPE_LANE0 - MLA_ROPE))),
         w_in[..., U_KR0 + MLA_ROPE:]], axis=-1))
    w_uq_p = bf(_pad_heads(mla_w_uq, MLA_HEADS, MLA_NOPE + MLA_ROPE))
    ukv = mla_w_ukv.reshape(DEPTH, MLA_KV_LORA, MLA_HEADS, MLA_NOPE + MLA_V)
    w_uk_p = bf(_pad_heads(ukv[..., :MLA_NOPE].reshape(DEPTH, MLA_KV_LORA, MLA_HEADS * MLA_NOPE), MLA_HEADS, MLA_NOPE))
    w_uv = bf(ukv[..., MLA_NOPE:].reshape(DEPTH, MLA_KV_LORA, MLA_HEADS * MLA_V))
    w_gate_b, w_f, w_m, w_n, w_out_b = bf(w_gate), bf(w_branch_f), bf(w_branch_m), bf(w_branch_n), bf(w_out)
    q_norm = mla_q_norm.reshape(DEPTH, 1, MLA_Q_LORA)
    kv_norm = mla_kv_norm.reshape(DEPTH, 1, MLA_KV_LORA)
    b_gate3 = b_gate.reshape(DEPTH, 1, 3 * D_MODEL)
    g3 = ln_g.reshape(DEPTH * 3, 1, D_MODEL)
    b3 = ln_b.reshape(DEPTH * 3, 1, D_MODEL)
    cos_t, sin_t = _rope_tables()
    dft_ctx = _fourier_tables(SEQ)
    dft_lat = _fourier_tables(DEC_SEQ)
    kr_pad = jnp.pad(cache_mla_krope, ((0, 0), (0, 0), (0, 0), (ROPE_LANE0, LANE - ROPE_LANE0 - MLA_ROPE)))
    cache_k = cache_na_k.reshape(DEC_BATCH, DEPTH, PAST_LEN, NA_W)
    cache_v = cache_na_v.reshape(DEC_BATCH, DEPTH, PAST_LEN, NA_W)

    cvec = jnp.concatenate([c_ctx[None], c, jnp.zeros((8 - 1 - DEC_BATCH, D_MODEL), F32)], axis=0)
    mod = _adaln(cvec, w_ada, b_ada)
    kc, vc = _ctx_kv(cache_mla_ckv, kr_pad, w_uk_p, w_uv)
    na_bias = _na_bias_table(na_rpb)

    x = (x_prompt.reshape(P_TOK, D_MODEL), x_sample.reshape(S_TOK, D_MODEL))
    caches = [jnp.zeros(s, F32) for s in _cache_shapes()]
    for l in range(DEPTH):
        x = _ffn(x, mod, *f1, g3, b3, l, 0)
        u_f, q, k, v, q_n, k_nb, v_nb, *caches = _mixer_in(
            x, mod, cos_t, sin_t, w_in_p, q_norm, w_uq_p, kv_norm, w_uk_p, w_uv, caches, l)
        yf_c = _fourier(u_f, dft_ctx, SEQ, 0, BATCH)
        yf_l = _fourier(u_f, dft_lat, DEC_SEQ, P_TOK // DEC_SEQ, DEC_BATCH)
        ym_c, yn_c = _ctx_attn(q, k, v, q_n, k_nb, v_nb)
        ym_l = _lat_mla(q, k, v, kc, vc, l)
        yn_l = _lat_na(q_n, k_nb, v_nb, cache_k, cache_v, na_bias, l)
        x = _merge(x, mod, (yf_c, ym_c, yn_c), (yf_l, ym_l, yn_l), w_gate_b, b_gate3, w_f, w_m, w_n, w_out_b,
                   g3, b3, l)
        x = _ffn(x, mod, *f2, g3, b3, l, 2, split_out=(l == DEPTH - 1))
    ckv, kr_t, nak_t, nav_t = caches
    per_head = lambda a: jnp.transpose(a.reshape(BATCH, DEPTH, NA_HEADS, NA_HEAD_DIM, SEQ), (0, 1, 4, 2, 3))
    return (x[0].reshape(BATCH, SEQ, D_MODEL), x[1].reshape(DEC_BATCH, DEC_SEQ, D_MODEL),
            ckv, jnp.transpose(kr_t, (0, 1, 3, 2)), per_head(nak_t), per_head(nav_t))
```

```python
import functools

import numpy as np
import jax
import jax.numpy as jnp
from jax import lax
from jax.experimental import pallas as pl
from jax.experimental.pallas import tpu as pltpu

F32 = jnp.float32
BF16 = jnp.bfloat16

D_MODEL = 1024
BATCH = 32
SEQ = 256
DEPTH = 4
DEC_BATCH = 2
DEC_SEQ = 2048
PAST_LEN = 512
GRID_W = 64
GRID_H = DEC_SEQ // GRID_W
D_FF = 2816
F_GROUPS = 4
F_GC = 128
F_W = F_GROUPS * F_GC
MLA_HEADS = 8
MLA_Q_LORA = 384
MLA_KV_LORA = 256
MLA_NOPE = 64
MLA_ROPE = 32
MLA_V = 64
NA_HEADS = 8
NA_HEAD_DIM = 64
NA_KH = 8
NA_KW = 16
NA_W = NA_HEADS * NA_HEAD_DIM
ROPE_BASE = 10000.0
AXIS_DIM = MLA_ROPE // 2
ALPHA = (2.0 * DEPTH) ** 0.25
MLA_SCALE = (MLA_NOPE + MLA_ROPE) ** -0.5
NA_SCALE = NA_HEAD_DIM ** -0.5
NEG_INF = -1e30

LANE = 128
MLA_HP = LANE
MLA_QW = MLA_HEADS * MLA_HP
P_TOK = BATCH * SEQ
S_TOK = DEC_BATCH * DEC_SEQ
TOKENS = P_TOK + S_TOK
TM = 512
SIDE_ROWS = 64
N_TILES = TOKENS // TM
P_TILES = P_TOK // TM
S_TILES_PER_REQ = DEC_SEQ // TM
MXU_TILE = 256
FF_CHUNKS = (0, 4 * MXU_TILE, 8 * MXU_TILE, D_FF)
U_Q0 = F_W
U_KV0 = U_Q0 + MLA_Q_LORA
U_KR0 = U_KV0 + MLA_KV_LORA
U_NA0 = U_KR0 + MLA_ROPE
U_W = U_NA0 + 3 * NA_W
ROPE_LANE0 = MLA_NOPE
NA_QROWS = 4
NA_QB = NA_QROWS * GRID_W
NA_WIN_ROWS = NA_QROWS + NA_KH
NA_WIN = NA_WIN_ROWS * GRID_W
NA_BLOCKS = GRID_H // NA_QROWS
MLA_QB = 256
FOURIER_ROWS = 1024
VMEM_LIMIT = 56 * 1024 * 1024
FFN_VMEM_LIMIT = 60 * 1024 * 1024
STAGE_SLOTS = 4
UP_STAGE_ROWS = 64
DOWN_STAGE_ROWS = D_FF // 16


def _group(i):
    return jnp.where(i < P_TILES, 0, 1 + (i - P_TILES) // S_TILES_PER_REQ)


def _rope_block(i):
    return jnp.where(i < P_TILES, 0, 1 + (i - P_TILES) % S_TILES_PER_REQ)


def _ln(x, eps):
    mu = jnp.mean(x, axis=-1, keepdims=True)
    xc = x - mu
    var = jnp.mean(xc * xc, axis=-1, keepdims=True)
    return xc * lax.rsqrt(var + eps)


def _rms(x, g):
    return x * lax.rsqrt(jnp.mean(x * x, axis=-1, keepdims=True) + 1e-6) * g


def _dot(a, b):
    return jnp.dot(a, b, preferred_element_type=F32)


def _dot_nt(a, b):
    return lax.dot_general(a, b, (((1,), (1,)), ((), ())), preferred_element_type=F32)


def _mod_rows(mod_ref, i):
    m = mod_ref[pl.ds(_group(i), 1), :]
    return m[:, :D_MODEL], m[:, D_MODEL:2 * D_MODEL], m[:, 2 * D_MODEL:]


def _rope(x, cos, sin):
    lane = lax.broadcasted_iota(jnp.int32, x.shape, 1)
    first_half = (lane % AXIS_DIM) < (AXIS_DIM // 2)
    rot = jnp.where(first_half, -pltpu.roll(x, LANE - AXIS_DIM // 2, 1), pltpu.roll(x, AXIS_DIM // 2, 1))
    return x * cos + rot * sin


def _adaln_kernel(c_ref, w_ref, b_ref, o_ref):
    c = c_ref[...]
    s = (c * jax.nn.sigmoid(c)).astype(BF16)
    o_ref[...] = _dot(s, w_ref[...].astype(BF16)) + b_ref[...]


def _adaln(cvec, w_ada, b_ada):
    n_col = 9 * D_MODEL // D_MODEL
    return pl.pallas_call(
        _adaln_kernel,
        grid=(DEPTH, n_col),
        in_specs=[pl.BlockSpec((8, D_MODEL), lambda l, j: (0, 0)),
                  pl.BlockSpec((None, D_MODEL, D_MODEL), lambda l, j: (l, 0, j)),
                  pl.BlockSpec((None, 1, D_MODEL), lambda l, j: (l, 0, j))],
        out_specs=pl.BlockSpec((None, 8, D_MODEL), lambda l, j: (l, 0, j)),
        out_shape=jax.ShapeDtypeStruct((DEPTH, 8, 9 * D_MODEL), F32),
        name="adaln",
    )(cvec, w_ada, b_ada.reshape(DEPTH, 1, 9 * D_MODEL))


def _ffn_kernel(*refs, layer, split_in, split_out):
    n_x = 4 if split_in else 2
    n_o = 2 if split_out else 1
    x_refs = refs[:n_x]
    mod_ref, w1_hbm, w3_hbm, w2_hbm, g_ref, b_ref = refs[n_x:n_x + 6]
    o_refs = refs[n_x + 6:n_x + 6 + n_o]
    h_scr, y_scr, w1_ref, w3_ref, w2_ref, stage_up, stage_down, sem = refs[n_x + 6 + n_o:]
    s = pl.program_id(0)
    cur = s % 2
    nxt = 1 - cur
    pieces = [pl.ds(r, SIDE_ROWS) for r in range(0, TM, SIDE_ROWS)]

    def x_tile(which, tile):
        if not split_in:
            return lambda rows: x_refs[which][rows, :]
        ctx_ref, lat_ref = x_refs[2 * which:2 * which + 2]
        return lambda rows: jnp.where(tile < P_TILES, ctx_ref[rows, :], lat_ref[rows, :])

    x_prev = x_tile(0, s - 1)
    x_next = x_tile(1, s + 1)

    def modulate(x_rows, tile, slot):
        shift, scale, _ = _mod_rows(mod_ref, tile)

        def piece(rows):
            h_scr[slot, rows, :] = (_ln(x_rows(rows), 1e-6) * (1.0 + scale) + shift).astype(BF16)
        return [functools.partial(piece, rows) for rows in pieces]

    def finish():
        _, _, gate = _mod_rows(mod_ref, s - 1)

        def piece(rows):
            z = ALPHA * x_prev(rows) + (0.5 * gate) * y_scr[nxt, rows, :]
            out = _ln(z, 1e-5) * g_ref[...] + b_ref[...]
            if split_out:
                to_ctx = jnp.broadcast_to(s - 1 < P_TILES, out.shape)
                pltpu.store(o_refs[0].at[rows, :], out, mask=to_ctx)
                pltpu.store(o_refs[1].at[rows, :], out, mask=jnp.logical_not(to_ctx))
            else:
                o_refs[0][rows, :] = out
            y_scr[nxt, rows, :] = out
        return [functools.partial(piece, rows) for rows in pieces]

    def load_weights():
        chunks = []
        for src, dst in ((w1_hbm, w1_ref), (w3_hbm, w3_ref)):
            for r in range(0, D_MODEL, UP_STAGE_ROWS):
                rows = pl.ds(r, UP_STAGE_ROWS)
                chunks.append((src.at[layer, rows, :], stage_up, dst.at[rows, :]))
        for r in range(0, D_FF, DOWN_STAGE_ROWS):
            rows = pl.ds(r, DOWN_STAGE_ROWS)
            chunks.append((w2_hbm.at[layer, rows, :], stage_down, w2_ref.at[rows, :]))

        def copy(k):
            src, stage, _ = chunks[k]
            return pltpu.make_async_copy(src, stage.at[k % STAGE_SLOTS], sem.at[k % STAGE_SLOTS])

        for k in range(STAGE_SLOTS - 1):
            copy(k).start()
        for k, (_, stage, dst) in enumerate(chunks):
            if k + STAGE_SLOTS - 1 < len(chunks):
                copy(k + STAGE_SLOTS - 1).start()
            copy(k).wait()
            dst[...] = stage[k % STAGE_SLOTS].astype(BF16)

    def matmuls(side_work):
        side_work = list(side_work)
        chunks = [slice(c0, c1) for c0, c1 in zip(FF_CHUNKS[:-1], FF_CHUNKS[1:])]
        per_dot = -(-len(side_work) // (3 * len(chunks)))

        def dot_with_side(lhs, rhs):
            for _ in range(min(per_dot, len(side_work))):
                side_work.pop(0)()
            return _dot(lhs() if callable(lhs) else lhs, rhs)

        h = lambda: h_scr[cur]

        def gate_up(sl):
            a = dot_with_side(h, w1_ref[:, sl])
            return (a * jax.nn.sigmoid(a) * dot_with_side(h, w3_ref[:, sl])).astype(BF16)

        t_next = gate_up(chunks[0])
        for n, sl in enumerate(chunks):
            t = t_next
            if n + 1 < len(chunks):
                t_next = gate_up(chunks[n + 1])
            yc = dot_with_side(t, w2_ref[sl, :])
            if n == 0:
                y_scr[cur] = yc
            else:
                y_scr[cur] += yc
        for work in side_work:
            work()

    @pl.when(s == 0)
    def _():
        load_weights()
        for work in modulate(x_tile(0, s), s, cur):
            work()
        matmuls(modulate(x_next, s + 1, nxt))

    @pl.when(jnp.logical_and(s > 0, s < N_TILES))
    def _():
        matmuls(finish() + modulate(x_next, s + 1, nxt))

    @pl.when(s == N_TILES)
    def _():
        for work in finish():
            work()


def _resident(shape, index_map):
    return pl.BlockSpec(shape, index_map, pipeline_mode=pl.Buffered(1))


def _ffn(x, mod, w1, w3, w2, ln_g, ln_b, l, sub, split_out=False):
    split_in = isinstance(x, tuple)
    tile = lambda lo, hi, off: pl.BlockSpec((TM, D_MODEL), lambda i: (jnp.clip(i + off, lo, hi) - lo, 0))
    ctx_lat = lambda off: [tile(0, P_TILES - 1, off), tile(P_TILES, N_TILES - 1, off)]
    if split_in:
        x_args = [x[0], x[1]] * 2
        x_specs = ctx_lat(-1) + ctx_lat(1)
    else:
        x_args = [x, x]
        x_specs = [tile(0, N_TILES - 1, -1), tile(0, N_TILES - 1, 1)]
    if split_out:
        out_specs = ctx_lat(-1)
        out_shape = [jax.ShapeDtypeStruct((P_TOK, D_MODEL), F32), jax.ShapeDtypeStruct((S_TOK, D_MODEL), F32)]
    else:
        out_specs = tile(0, N_TILES - 1, -1)
        out_shape = jax.ShapeDtypeStruct((TOKENS, D_MODEL), F32)
    return pl.pallas_call(
        functools.partial(_ffn_kernel, layer=l, split_in=split_in, split_out=split_out),
        grid=(N_TILES + 1,),
        in_specs=x_specs + [
                  _resident((None, 8, 3 * D_MODEL), lambda i: (l, 0, sub)),
                  pl.BlockSpec(memory_space=pl.ANY), pl.BlockSpec(memory_space=pl.ANY),
                  pl.BlockSpec(memory_space=pl.ANY),
                  _resident((None, 1, D_MODEL), lambda i: (3 * l + sub, 0, 0)),
                  _resident((None, 1, D_MODEL), lambda i: (3 * l + sub, 0, 0))],
        out_specs=out_specs,
        out_shape=out_shape,
        scratch_shapes=[pltpu.VMEM((2, TM, D_MODEL), BF16), pltpu.VMEM((2, TM, D_MODEL), F32),
                        pltpu.VMEM((D_MODEL, D_FF), BF16), pltpu.VMEM((D_MODEL, D_FF), BF16),
                        pltpu.VMEM((D_FF, D_MODEL), BF16),
                        pltpu.VMEM((STAGE_SLOTS, UP_STAGE_ROWS, D_FF), F32),
                        pltpu.VMEM((STAGE_SLOTS, DOWN_STAGE_ROWS, D_MODEL), F32),
                        pltpu.SemaphoreType.DMA((STAGE_SLOTS,))],
        compiler_params=pltpu.CompilerParams(dimension_semantics=("arbitrary",),
                                             vmem_limit_bytes=FFN_VMEM_LIMIT),
        name="ffn",
    )(*x_args, mod, w1, w3, w2, ln_g, ln_b)


def _mixer_in_kernel(x_ref, mod_ref, cos_ref, sin_ref, w_in_ref, qn_ref, w_uq_ref, kvn_ref, w_uk_ref, w_uv_ref,
                     ckv_prev, kr_prev, kna_prev, vna_prev,
                     uf_ref, q_ref, k_ref, v_ref, qna_ref, knab_ref, vnab_ref, ckv_ref, kr_ref, kna_ref, vna_ref):
    del ckv_prev, kr_prev, kna_prev, vna_prev
    i = pl.program_id(0)
    shift, scale, _ = _mod_rows(mod_ref, i)
    h = (_ln(x_ref[...], 1e-6) * (1.0 + scale) + shift).astype(BF16)
    cos = cos_ref[...]
    sin = sin_ref[...]

    proj = lambda r0, r1: _dot_nt(h, w_in_ref[r0:r1, :])
    uf_ref[...] = proj(0, U_Q0)

    u_q = proj(U_Q0, U_KV0)
    q = _dot(_rms(u_q, qn_ref[...]).astype(BF16), w_uq_ref[...])
    for hd in range(MLA_HEADS):
        sl = slice(hd * MLA_HP, (hd + 1) * MLA_HP)
        q_ref[:, sl] = _rope(q[:, sl], cos, sin).astype(BF16)

    c_kv = _rms(proj(U_KV0, U_KR0), kvn_ref[...])
    w_kr = jnp.concatenate([jnp.zeros((ROPE_LANE0, D_MODEL), BF16), w_in_ref[U_KR0:U_NA0, :],
                            jnp.zeros((LANE - ROPE_LANE0 - MLA_ROPE, D_MODEL), BF16)], axis=0)
    kr = _dot_nt(h, w_kr)
    k_na = proj(U_NA0 + NA_W, U_NA0 + 2 * NA_W)
    v_na = proj(U_NA0 + 2 * NA_W, U_W)

    @pl.when(i < P_TILES)
    def _():
        kr_t = kr.T[ROPE_LANE0:ROPE_LANE0 + MLA_ROPE, :]
        kna_t = k_na.T
        vna_t = v_na.T
        for b in range(TM // SEQ):
            rows = slice(b * SEQ, (b + 1) * SEQ)
            ckv_ref[b] = c_kv[rows, :]
            kr_ref[b] = kr_t[:, rows]
            kna_ref[b] = kna_t[:, rows]
            vna_ref[b] = vna_t[:, rows]

    c_kv = c_kv.astype(BF16)
    kr = _rope(kr, cos, sin)
    k = _dot(c_kv, w_uk_ref[...])
    for hd in range(MLA_HEADS):
        sl = slice(hd * MLA_HP, (hd + 1) * MLA_HP)
        k_ref[:, sl] = (k[:, sl] + kr).astype(BF16)
    v_ref[...] = _dot(c_kv, w_uv_ref[...]).astype(BF16)

    qna_ref[...] = proj(U_NA0, U_NA0 + NA_W).astype(BF16)
    knab_ref[...] = k_na.astype(BF16)
    vnab_ref[...] = v_na.astype(BF16)


def _cache_shapes():
    return [(BATCH, DEPTH, SEQ, MLA_KV_LORA), (BATCH, DEPTH, MLA_ROPE, SEQ),
            (BATCH, DEPTH, NA_W, SEQ), (BATCH, DEPTH, NA_W, SEQ)]


def _mixer_in(x, mod, cos_t, sin_t, w_in, q_norm, w_uq, kv_norm, w_uk, w_uv, caches, l):
    tile = lambda w: pl.BlockSpec((TM, w), lambda i: (i, 0))
    acts = [(F_W, F32), (MLA_QW, BF16), (MLA_QW, BF16), (MLA_HEADS * MLA_V, BF16), (NA_W, BF16), (NA_W, BF16),
            (NA_W, BF16)]
    cache_spec = lambda s: pl.BlockSpec((TM // SEQ, None) + s[2:],
                                        lambda i: (jnp.minimum(i, P_TILES - 1), l, 0, 0))
    n_in = 10
    return pl.pallas_call(
        _mixer_in_kernel,
        grid=(N_TILES,),
        in_specs=[tile(D_MODEL),
                  _resident((None, 8, 3 * D_MODEL), lambda i: (l, 0, 1)),
                  pl.BlockSpec((TM, LANE), lambda i: (_rope_block(i), 0)),
                  pl.BlockSpec((TM, LANE), lambda i: (_rope_block(i), 0)),
                  _resident((None, U_W, D_MODEL), lambda i: (l, 0, 0)),
                  _resident((None, 1, MLA_Q_LORA), lambda i: (l, 0, 0)),
                  _resident((None, MLA_Q_LORA, MLA_QW), lambda i: (l, 0, 0)),
                  _resident((None, 1, MLA_KV_LORA), lambda i: (l, 0, 0)),
                  _resident((None, MLA_KV_LORA, MLA_QW), lambda i: (l, 0, 0)),
                  _resident((None, MLA_KV_LORA, MLA_HEADS * MLA_V), lambda i: (l, 0, 0))]
                 + [pl.BlockSpec(memory_space=pl.ANY)] * len(caches),
        out_specs=[tile(w) for w, _ in acts] + [cache_spec(s) for s in _cache_shapes()],
        out_shape=[jax.ShapeDtypeStruct((TOKENS, w), dt) for w, dt in acts]
                  + [jax.ShapeDtypeStruct(s, F32) for s in _cache_shapes()],
        input_output_aliases={n_in + n: len(acts) + n for n in range(len(caches))},
        compiler_params=pltpu.CompilerParams(dimension_semantics=("arbitrary",),
                                             vmem_limit_bytes=VMEM_LIMIT),
        name="mixer_in",
    )(x, mod, cos_t, sin_t, w_in, q_norm, w_uq, kv_norm, w_uk, w_uv, *caches)


def _ctx_kv_kernel(ckv_ref, kr_ref, w_uk_ref, w_uv_ref, k_ref, v_ref):
    c = ckv_ref[...].astype(BF16)
    k = _dot(c, w_uk_ref[...])
    kr = kr_ref[...]
    for hd in range(MLA_HEADS):
        sl = slice(hd * MLA_HP, (hd + 1) * MLA_HP)
        k_ref[:, sl] = (k[:, sl] + kr).astype(BF16)
    v_ref[...] = _dot(c, w_uv_ref[...]).astype(BF16)


def _ctx_kv(cache_ckv, cache_kr_pad, w_uk, w_uv):
    return pl.pallas_call(
        _ctx_kv_kernel,
        grid=(DEC_BATCH, DEPTH),
        in_specs=[pl.BlockSpec((None, None, PAST_LEN, MLA_KV_LORA), lambda b, l: (b, l, 0, 0)),
                  pl.BlockSpec((None, None, PAST_LEN, LANE), lambda b, l: (b, l, 0, 0)),
                  pl.BlockSpec((None, MLA_KV_LORA, MLA_QW), lambda b, l: (l, 0, 0)),
                  pl.BlockSpec((None, MLA_KV_LORA, MLA_HEADS * MLA_V), lambda b, l: (l, 0, 0))],
        out_specs=[pl.BlockSpec((None, None, PAST_LEN, MLA_QW), lambda b, l: (b, l, 0, 0)),
                   pl.BlockSpec((None, None, PAST_LEN, MLA_HEADS * MLA_V), lambda b, l: (b, l, 0, 0))],
        out_shape=[jax.ShapeDtypeStruct((DEC_BATCH, DEPTH, PAST_LEN, MLA_QW), BF16),
                   jax.ShapeDtypeStruct((DEC_BATCH, DEPTH, PAST_LEN, MLA_HEADS * MLA_V), BF16)],
        name="ctx_kv",
    )(cache_ckv, cache_kr_pad, w_uk, w_uv)


def _fourier_kernel(x_ref, cs_ref, cl_ref, sl_ref, o_ref):
    length = cl_ref.shape[0]
    x = x_ref[...].astype(BF16)
    xc, xs = [], []
    for g in range(F_GROUPS):
        t = _dot(x[:, g * F_GC:(g + 1) * F_GC], cs_ref[...])
        xc.append(t[:, :F_GC])
        xs.append(t[:, F_GC:])
    xc = jnp.concatenate(xc, axis=1).astype(BF16)
    xs = jnp.concatenate(xs, axis=1).astype(BF16)
    for r in range(0, x_ref.shape[0], length):
        rows = slice(r, r + length)
        o_ref[rows, :] = (_dot(cl_ref[...], xc[rows]) - _dot(sl_ref[...], xs[rows])).astype(BF16)


def _dft_tables(n):
    k = np.arange(n, dtype=np.int64)
    ang = 2.0 * np.pi * ((k[:, None] * k[None, :]) % n).astype(np.float64) / n
    s = n ** -0.5
    return np.cos(ang) * s, np.sin(ang) * s


def _fourier_tables(length):
    cc, sc = _dft_tables(F_GC)
    cl, sl = _dft_tables(length)
    as_bf16 = lambda a: jnp.asarray(a, F32).astype(BF16)
    return as_bf16(np.concatenate([cc, sc], axis=1)), as_bf16(cl), as_bf16(sl)


def _fourier(u_f, tables, length, first_block, n_blocks):
    cs, cl, sl = tables
    rows = max(length, FOURIER_ROWS)
    assert rows % length == 0 and (n_blocks * length) % rows == 0 and (first_block * length) % rows == 0
    first = first_block * length // rows
    return pl.pallas_call(
        _fourier_kernel,
        grid=(n_blocks * length // rows,),
        in_specs=[pl.BlockSpec((rows, F_W), lambda b: (first + b, 0)),
                  _resident((F_GC, 2 * F_GC), lambda b: (0, 0)),
                  _resident((length, length), lambda b: (0, 0)),
                  _resident((length, length), lambda b: (0, 0))],
        out_specs=pl.BlockSpec((rows, F_W), lambda b: (b, 0)),
        out_shape=jax.ShapeDtypeStruct((n_blocks * length, F_W), BF16),
        compiler_params=pltpu.CompilerParams(dimension_semantics=("arbitrary",),
                                             vmem_limit_bytes=VMEM_LIMIT),
        name="fourier_%d" % length,
    )(u_f, cs, cl, sl)


LOG2E = 1.4426950408889634


def _softmax_pv(scores, values_ext, scale):
    m = None
    for s in scores:
        sm = jnp.max(s, axis=-1, keepdims=True)
        m = sm if m is None else jnp.maximum(m, sm)
    acc = None
    for s, v in zip(scores, values_ext):
        p = jnp.exp2((s - m) * (scale * LOG2E)).astype(BF16)
        pv = _dot_nt(p, v.t) if isinstance(v, _KeyMinor) else _dot(p, v)
        acc = pv if acc is None else acc + pv
    return acc[:, :LANE] / acc[:, LANE:]


class _KeyMinor:
    def __init__(self, t):
        self.t = t


def _with_ones(v_pair):
    return jnp.concatenate([v_pair, jnp.ones_like(v_pair)], axis=1)


def _half_masks(rows):
    low = lax.broadcasted_iota(jnp.int32, (rows, LANE), 1) < LANE // 2
    return low, jnp.logical_not(low)


def _head_of_pair(x_pair, mask):
    return jnp.where(mask, x_pair, jnp.zeros_like(x_pair))


def _ctx_attn_kernel(q_ref, k_ref, v_ref, qn_ref, kn_ref, vn_ref, om_ref, on_ref):
    masks = _half_masks(SEQ)
    for pair in range(MLA_HEADS // 2):
        ps = slice(pair * LANE, (pair + 1) * LANE)
        v_ext = _with_ones(v_ref[:, ps])
        o = []
        for half in range(2):
            sl = slice((2 * pair + half) * MLA_HP, (2 * pair + half + 1) * MLA_HP)
            s = _dot_nt(q_ref[:, sl], k_ref[:, sl])
            o.append(_softmax_pv([s], [v_ext], MLA_SCALE))
        om_ref[:, ps] = jnp.where(masks[0], o[0], o[1]).astype(BF16)
    for pair in range(NA_HEADS // 2):
        ps = slice(pair * LANE, (pair + 1) * LANE)
        v_ext = _with_ones(vn_ref[:, ps])
        q_pair = qn_ref[:, ps]
        k_pair = kn_ref[:, ps]
        o = []
        for half in range(2):
            s = _dot_nt(_head_of_pair(q_pair, masks[half]), k_pair)
            o.append(_softmax_pv([s], [v_ext], NA_SCALE))
        on_ref[:, ps] = jnp.where(masks[0], o[0], o[1]).astype(BF16)


def _ctx_attn(q, k, v, qn, kn, vn):
    blk = lambda w: pl.BlockSpec((SEQ, w), lambda b: (b, 0))
    return pl.pallas_call(
        _ctx_attn_kernel,
        grid=(BATCH,),
        in_specs=[blk(MLA_QW), blk(MLA_QW), blk(MLA_HEADS * MLA_V), blk(NA_W), blk(NA_W), blk(NA_W)],
        out_specs=[blk(MLA_HEADS * MLA_V), blk(NA_W)],
        out_shape=[jax.ShapeDtypeStruct((P_TOK, MLA_HEADS * MLA_V), BF16),
                   jax.ShapeDtypeStruct((P_TOK, NA_W), BF16)],
        compiler_params=pltpu.CompilerParams(dimension_semantics=("arbitrary",)),
        name="ctx_attn",
    )(q, k, v, qn, kn, vn)


def _lat_mla_kernel(q_ref, k_ref, v_ref, kc_ref, vc_ref, o_ref):
    low, _ = _half_masks(MLA_QB)
    for pair in range(MLA_HEADS // 2):
        ps = slice(pair * LANE, (pair + 1) * LANE)
        v_ext = [_with_ones(v_ref[:, ps]), _with_ones(vc_ref[:, ps])]
        o = []
        for half in range(2):
            sl = slice((2 * pair + half) * MLA_HP, (2 * pair + half + 1) * MLA_HP)
            q = q_ref[:, sl]
            o.append(_softmax_pv([_dot_nt(q, k_ref[:, sl]), _dot_nt(q, kc_ref[:, sl])], v_ext, MLA_SCALE))
        o_ref[:, ps] = jnp.where(low, o[0], o[1]).astype(BF16)


def _lat_mla(q, k, v, kc, vc, l):
    nq = DEC_SEQ // MLA_QB
    q0 = P_TOK // MLA_QB
    r0 = P_TOK // DEC_SEQ
    return pl.pallas_call(
        _lat_mla_kernel,
        grid=(DEC_BATCH, nq),
        in_specs=[pl.BlockSpec((MLA_QB, MLA_QW), lambda b, j: (q0 + b * nq + j, 0)),
                  pl.BlockSpec((DEC_SEQ, MLA_QW), lambda b, j: (r0 + b, 0)),
                  pl.BlockSpec((DEC_SEQ, MLA_HEADS * MLA_V), lambda b, j: (r0 + b, 0)),
                  pl.BlockSpec((None, None, PAST_LEN, MLA_QW), lambda b, j: (b, l, 0, 0)),
                  pl.BlockSpec((None, None, PAST_LEN, MLA_HEADS * MLA_V), lambda b, j: (b, l, 0, 0))],
        out_specs=pl.BlockSpec((MLA_QB, MLA_HEADS * MLA_V), lambda b, j: (b * nq + j, 0)),
        out_shape=jax.ShapeDtypeStruct((S_TOK, MLA_HEADS * MLA_V), BF16),
        compiler_params=pltpu.CompilerParams(dimension_semantics=("arbitrary", "arbitrary"),
                                             vmem_limit_bytes=VMEM_LIMIT),
        name="lat_mla",
    )(q, k, v, kc, vc)


def _na_window_block(j):
    return jnp.clip(j - 1, 0, NA_BLOCKS - NA_WIN_ROWS // NA_QROWS)


def _lat_na_kernel(q_ref, k_ref, v_ref, kc_ref, vc_ref, bias_ref, o_ref):
    j = pl.program_id(1)
    start = pl.multiple_of(_na_window_block(j) * NA_QB, NA_QB)
    k_win = k_ref[pl.ds(start, NA_WIN), :]
    v_win = v_ref[pl.ds(start, NA_WIN), :]
    masks = _half_masks(NA_QB)
    for pair in range(NA_HEADS // 2):
        ps = slice(pair * LANE, (pair + 1) * LANE)
        k_loc = k_win[:, ps]
        k_ctx_t = kc_ref[ps, :].astype(BF16)
        v_ctx_t = vc_ref[ps, :].astype(BF16)
        v_ext = [_with_ones(v_win[:, ps]), _KeyMinor(jnp.concatenate([v_ctx_t, jnp.ones_like(v_ctx_t)], axis=0))]
        q_pair = q_ref[:, ps]
        o = []
        for half in range(2):
            q = _head_of_pair(q_pair, masks[half])
            s_loc = _dot_nt(q, k_loc) + bias_ref[2 * pair + half]
            o.append(_softmax_pv([s_loc, _dot(q, k_ctx_t)], v_ext, NA_SCALE))
        o_ref[:, ps] = jnp.where(masks[0], o[0], o[1]).astype(BF16)


NA_BLOCK_KINDS = (0, 1, NA_BLOCKS - 1)
NA_DIAG = GRID_W - 1
NA_N_DR = 2 * NA_KH - 1


def _na_bias_kernel(e_ref, o_ref):
    lane = lax.broadcasted_iota(jnp.int32, (GRID_W, LANE), 1)
    q_col = lax.broadcasted_iota(jnp.int32, (GRID_W, LANE), 0)
    k_col = lane % GRID_W
    col_start = jnp.clip(q_col - NA_KW // 2, 0, GRID_W - NA_KW)
    col_ok = (k_col >= col_start) & (k_col < col_start + NA_KW)
    low_half = lane < GRID_W
    neg = jnp.full((GRID_W, LANE), NEG_INF, F32)
    tiles = {}

    def toeplitz(dr, half):
        if (dr, half) not in tiles:
            row = jnp.broadcast_to(e_ref[dr:dr + 1, :], (GRID_W, LANE))
            shift = (LANE - NA_DIAG + half * GRID_W) % LANE
            tiles[dr, half] = pltpu.roll(row, shift, 1, stride=1, stride_axis=0)
        return tiles[dr, half]

    for kind, blk in enumerate(NA_BLOCK_KINDS):
        first_key_row = NA_QROWS * min(max(blk - 1, 0), NA_BLOCKS - NA_WIN_ROWS // NA_QROWS)
        for qr in range(NA_QROWS):
            q_row = NA_QROWS * blk + qr
            row_start = min(max(q_row - NA_KH // 2, 0), GRID_H - NA_KH)
            for pair in range(NA_WIN_ROWS // 2):
                halves = []
                for half in range(2):
                    k_row = first_key_row + 2 * pair + half
                    inside = row_start <= k_row < row_start + NA_KH
                    halves.append(toeplitz(k_row - q_row + NA_KH - 1, half) if inside else neg)
                tile = jnp.where(low_half, halves[0], halves[1])
                o_ref[kind, qr * GRID_W:(qr + 1) * GRID_W, pair * LANE:(pair + 1) * LANE] = (
                    jnp.where(col_ok, tile, NEG_INF))


def _na_bias_table(rpb):
    left = NA_DIAG - (NA_KW - 1)
    e = jnp.concatenate([jnp.broadcast_to(rpb[..., :1], rpb.shape[:-1] + (left,)), rpb,
                         jnp.broadcast_to(rpb[..., -1:], rpb.shape[:-1] + (LANE - left - rpb.shape[-1],))],
                        axis=-1) * (1.0 / NA_SCALE)
    return pl.pallas_call(
        _na_bias_kernel,
        grid=(DEPTH, NA_HEADS),
        in_specs=[pl.BlockSpec((None, None, NA_N_DR, LANE), lambda l, h: (l, h, 0, 0))],
        out_specs=pl.BlockSpec((None, len(NA_BLOCK_KINDS), None, NA_QB, NA_WIN), lambda l, h: (l, 0, h, 0, 0)),
        out_shape=jax.ShapeDtypeStruct((DEPTH, len(NA_BLOCK_KINDS), NA_HEADS, NA_QB, NA_WIN), F32),
        name="na_bias",
    )(e)


def _lat_na(q, k, v, cache_k, cache_v, bias, l):
    q0 = P_TOK // NA_QB
    r0 = P_TOK // DEC_SEQ
    kind = lambda j: (j > 0).astype(jnp.int32) + (j == NA_BLOCKS - 1).astype(jnp.int32)
    return pl.pallas_call(
        _lat_na_kernel,
        grid=(DEC_BATCH, NA_BLOCKS),
        in_specs=[pl.BlockSpec((NA_QB, NA_W), lambda b, j: (q0 + b * NA_BLOCKS + j, 0)),
                  pl.BlockSpec((DEC_SEQ, NA_W), lambda b, j: (r0 + b, 0)),
                  pl.BlockSpec((DEC_SEQ, NA_W), lambda b, j: (r0 + b, 0)),
                  pl.BlockSpec((None, None, PAST_LEN, NA_W), lambda b, j: (b, l, 0, 0)),
                  pl.BlockSpec((None, None, PAST_LEN, NA_W), lambda b, j: (b, l, 0, 0)),
                  pl.BlockSpec((None, None, NA_HEADS, NA_QB, NA_WIN), lambda b, j: (l, kind(j), 0, 0, 0))],
        out_specs=pl.BlockSpec((NA_QB, NA_W), lambda b, j: (b * NA_BLOCKS + j, 0)),
        out_shape=jax.ShapeDtypeStruct((S_TOK, NA_W), BF16),
        compiler_params=pltpu.CompilerParams(dimension_semantics=("arbitrary", "arbitrary"),
                                             vmem_limit_bytes=VMEM_LIMIT),
        name="lat_na",
    )(q, k, v, cache_k, cache_v, bias)


def _merge_kernel(x_ref, mod_ref, yfc_ref, ymc_ref, ync_ref, yfl_ref, yml_ref, ynl_ref,
                  wg_ref, bg_ref, wf_ref, wm_ref, wn_ref, wo_ref, g_ref, b_ref, o_ref):
    i = pl.program_id(0)
    shift, scale, gate = _mod_rows(mod_ref, i)
    x = x_ref[...]
    h = (_ln(x, 1e-6) * (1.0 + scale) + shift).astype(BF16)
    is_ctx = i < P_TILES
    mix = None
    for n, (yc_ref, yl_ref, w_ref) in enumerate(((yfc_ref, yfl_ref, wf_ref), (ymc_ref, yml_ref, wm_ref),
                                                 (ync_ref, ynl_ref, wn_ref))):
        sl = slice(n * D_MODEL, (n + 1) * D_MODEL)
        g = jax.nn.sigmoid(_dot(h, wg_ref[:, sl]) + bg_ref[:, sl])
        y = jnp.where(is_ctx, yc_ref[...], yl_ref[...])
        t = g * _dot(y, w_ref[...])
        mix = t if mix is None else mix + t
    z = ALPHA * x + gate * _dot(mix.astype(BF16), wo_ref[...])
    o_ref[...] = _ln(z, 1e-5) * g_ref[...] + b_ref[...]


def _merge(x, mod, y_ctx, y_lat, w_gate, b_gate, w_f, w_m, w_n, w_out, ln_g, ln_b, l):
    tile = lambda w: pl.BlockSpec((TM, w), lambda i: (i, 0))
    ctx_tile = lambda w: pl.BlockSpec((TM, w), lambda i: (jnp.minimum(i, P_TILES - 1), 0))
    lat_tile = lambda w: pl.BlockSpec((TM, w), lambda i: (jnp.maximum(i - P_TILES, 0), 0))
    widths = (F_W, MLA_HEADS * MLA_V, NA_W)
    return pl.pallas_call(
        _merge_kernel,
        grid=(N_TILES,),
        in_specs=[tile(D_MODEL),
                  _resident((None, 8, 3 * D_MODEL), lambda i: (l, 0, 1))]
                 + [ctx_tile(w) for w in widths] + [lat_tile(w) for w in widths] + [
                  _resident((None, D_MODEL, 3 * D_MODEL), lambda i: (l, 0, 0)),
                  _resident((None, 1, 3 * D_MODEL), lambda i: (l, 0, 0)),
                  _resident((None, F_W, D_MODEL), lambda i: (l, 0, 0)),
                  _resident((None, MLA_HEADS * MLA_V, D_MODEL), lambda i: (l, 0, 0)),
                  _resident((None, NA_W, D_MODEL), lambda i: (l, 0, 0)),
                  _resident((None, D_MODEL, D_MODEL), lambda i: (l, 0, 0)),
                  _resident((None, 1, D_MODEL), lambda i: (3 * l + 1, 0, 0)),
                  _resident((None, 1, D_MODEL), lambda i: (3 * l + 1, 0, 0))],
        out_specs=tile(D_MODEL),
        out_shape=jax.ShapeDtypeStruct((TOKENS, D_MODEL), F32),
        compiler_params=pltpu.CompilerParams(dimension_semantics=("arbitrary",),
                                             vmem_limit_bytes=VMEM_LIMIT),
        name="merge",
    )(x, mod, *y_ctx, *y_lat, w_gate, b_gate, w_f, w_m, w_n, w_out, ln_g, ln_b)


def _pad_heads(w, n_heads, width):
    lead = w.shape[:-1]
    w = w.reshape(lead + (n_heads, width))
    w = jnp.pad(w, [(0, 0)] * len(lead) + [(0, 0), (0, MLA_HP - width)])
    return w.reshape(lead + (n_heads * MLA_HP,))


def _rope_tables():
    t = jnp.arange(DEC_SEQ, dtype=jnp.int32)
    pos = jnp.stack([t // GRID_W, t % GRID_W], axis=-1).astype(F32)
    half = AXIS_DIM // 2
    inv_freq = ROPE_BASE ** (-jnp.arange(half, dtype=F32) / half)
    ang = pos[:, :, None] * inv_freq
    ang = jnp.concatenate([ang, ang], axis=-1).reshape(DEC_SEQ, MLA_ROPE)
    pad = lambda a, fill: jnp.pad(a, ((0, 0), (ROPE_LANE0, LANE - ROPE_LANE0 - MLA_ROPE)), constant_values=fill)
    cos = jnp.concatenate([jnp.ones((TM, LANE), F32), pad(jnp.cos(ang), 1.0)], axis=0)
    sin = jnp.concatenate([jnp.zeros((TM, LANE), F32), pad(jnp.sin(ang), 0.0)], axis=0)
    return cos, sin


def kernel(x_prompt, x_sample, cache_mla_ckv, cache_mla_krope, cache_na_k, cache_na_v, c, c_ctx, w_ada, b_ada, ffn1_w1, ffn1_w3, ffn1_w2, ffn2_w1, ffn2_w3, ffn2_w2, w_in, mla_q_norm, mla_w_uq, mla_kv_norm, mla_w_ukv, na_rpb, w_branch_f, w_branch_m, w_branch_n, w_gate, b_gate, w_out, ln_g, ln_b):
    bf = lambda w: w.astype(BF16)
    f1 = (ffn1_w1, ffn1_w3, ffn1_w2)
    f2 = (ffn2_w1, ffn2_w3, ffn2_w2)
    w_in_p = bf(jnp.swapaxes(w_in, 1, 2))
    w_uq_p = bf(_pad_heads(mla_w_uq, MLA_HEADS, MLA_NOPE + MLA_ROPE))
    ukv = mla_w_ukv.reshape(DEPTH, MLA_KV_LORA, MLA_HEADS, MLA_NOPE + MLA_V)
    w_uk_p = bf(_pad_heads(ukv[..., :MLA_NOPE].reshape(DEPTH, MLA_KV_LORA, MLA_HEADS * MLA_NOPE), MLA_HEADS, MLA_NOPE))
    w_uv = bf(ukv[..., MLA_NOPE:].reshape(DEPTH, MLA_KV_LORA, MLA_HEADS * MLA_V))
    w_gate_b, w_f, w_m, w_n, w_out_b = bf(w_gate), bf(w_branch_f), bf(w_branch_m), bf(w_branch_n), bf(w_out)
    q_norm = mla_q_norm.reshape(DEPTH, 1, MLA_Q_LORA)
    kv_norm = mla_kv_norm.reshape(DEPTH, 1, MLA_KV_LORA)
    b_gate3 = b_gate.reshape(DEPTH, 1, 3 * D_MODEL)
    g3 = ln_g.reshape(DEPTH * 3, 1, D_MODEL)
    b3 = ln_b.reshape(DEPTH * 3, 1, D_MODEL)
    cos_t, sin_t = _rope_tables()
    dft_ctx = _fourier_tables(SEQ)
    dft_lat = _fourier_tables(DEC_SEQ)
    kr_pad = jnp.pad(cache_mla_krope, ((0, 0), (0, 0), (0, 0), (ROPE_LANE0, LANE - ROPE_LANE0 - MLA_ROPE)))
    ch_major = lambda a: jnp.transpose(a, (0, 1, 3, 4, 2)).reshape(DEC_BATCH, DEPTH, NA_W, PAST_LEN)
    cache_k = ch_major(cache_na_k)
    cache_v = ch_major(cache_na_v)

    cvec = jnp.concatenate([c_ctx[None], c, jnp.zeros((8 - 1 - DEC_BATCH, D_MODEL), F32)], axis=0)
    mod = _adaln(cvec, w_ada, b_ada)
    kc, vc = _ctx_kv(cache_mla_ckv, kr_pad, w_uk_p, w_uv)
    na_bias = _na_bias_table(na_rpb)

    x = (x_prompt.reshape(P_TOK, D_MODEL), x_sample.reshape(S_TOK, D_MODEL))
    caches = [jnp.zeros(s, F32) for s in _cache_shapes()]
    for l in range(DEPTH):
        x = _ffn(x, mod, *f1, g3, b3, l, 0)
        u_f, q, k, v, q_n, k_nb, v_nb, *caches = _mixer_in(
            x, mod, cos_t, sin_t, w_in_p, q_norm, w_uq_p, kv_norm, w_uk_p, w_uv, caches, l)
        yf_c = _fourier(u_f, dft_ctx, SEQ, 0, BATCH)
        yf_l = _fourier(u_f, dft_lat, DEC_SEQ, P_TOK // DEC_SEQ, DEC_BATCH)
        ym_c, yn_c = _ctx_attn(q, k, v, q_n, k_nb, v_nb)
        ym_l = _lat_mla(q, k, v, kc, vc, l)
        yn_l = _lat_na(q_n, k_nb, v_nb, cache_k, cache_v, na_bias, l)
        x = _merge(x, mod, (yf_c, ym_c, yn_c), (yf_l, ym_l, yn_l), w_gate_b, b_gate3, w_f, w_m, w_n, w_out_b,
                   g3, b3, l)
        x = _ffn(x, mod, *f2, g3, b3, l, 2, split_out=(l == DEPTH - 1))
    ckv, kr_t, nak_t, nav_t = caches
    per_head = lambda a: jnp.transpose(a.reshape(BATCH, DEPTH, NA_HEADS, NA_HEAD_DIM, SEQ), (0, 1, 4, 2, 3))
    return (x[0].reshape(BATCH, SEQ, D_MODEL), x[1].reshape(DEC_BATCH, DEC_SEQ, D_MODEL),
            ckv, jnp.transpose(kr_t, (0, 1, 3, 2)), per_head(nak_t), per_head(nav_t))
```

```python
import functools

import numpy as np
import jax
import jax.numpy as jnp
from jax import lax
from jax.experimental import pallas as pl
from jax.experimental.pallas import tpu as pltpu

F32 = jnp.float32
BF16 = jnp.bfloat16

D_MODEL = 1024
BATCH = 32
SEQ = 256
DEPTH = 4
DEC_BATCH = 2
DEC_SEQ = 2048
PAST_LEN = 512
GRID_W = 64
GRID_H = DEC_SEQ // GRID_W
D_FF = 2816
F_GROUPS = 4
F_GC = 128
F_W = F_GROUPS * F_GC
MLA_HEADS = 8
MLA_Q_LORA = 384
MLA_KV_LORA = 256
MLA_NOPE = 64
MLA_ROPE = 32
MLA_V = 64
NA_HEADS = 8
NA_HEAD_DIM = 64
NA_KH = 8
NA_KW = 16
NA_W = NA_HEADS * NA_HEAD_DIM
ROPE_BASE = 10000.0
AXIS_DIM = MLA_ROPE // 2
ALPHA = (2.0 * DEPTH) ** 0.25
MLA_SCALE = (MLA_NOPE + MLA_ROPE) ** -0.5
NA_SCALE = NA_HEAD_DIM ** -0.5
NEG_INF = -1e30

LANE = 128
MLA_HP = LANE
MLA_QW = MLA_HEADS * MLA_HP
P_TOK = BATCH * SEQ
S_TOK = DEC_BATCH * DEC_SEQ
TOKENS = P_TOK + S_TOK
TM = 512
SIDE_ROWS = 64
N_TILES = TOKENS // TM
P_TILES = P_TOK // TM
S_TILES_PER_REQ = DEC_SEQ // TM
MXU_TILE = 256
FF_CHUNKS = (0, 4 * MXU_TILE, 8 * MXU_TILE, D_FF)
U_Q0 = F_W
U_KV0 = U_Q0 + MLA_Q_LORA
U_KR0 = U_KV0 + MLA_KV_LORA
U_NA0 = U_KR0 + MLA_ROPE
U_W = U_NA0 + 3 * NA_W
ROPE_LANE0 = MLA_NOPE
NA_QROWS = 4
NA_QB = NA_QROWS * GRID_W
NA_WIN_ROWS = NA_QROWS + NA_KH
NA_WIN = NA_WIN_ROWS * GRID_W
NA_BLOCKS = GRID_H // NA_QROWS
MLA_QB = 1024
CTX_ATTN_SEQS = 4
FOURIER_ROWS = 1024
VMEM_LIMIT = 56 * 1024 * 1024
FFN_VMEM_LIMIT = 60 * 1024 * 1024
STAGE_SLOTS = 4
UP_STAGE_ROWS = 64
DOWN_STAGE_ROWS = D_FF // 16


def _group(i):
    return jnp.where(i < P_TILES, 0, 1 + (i - P_TILES) // S_TILES_PER_REQ)


def _rope_block(i):
    return jnp.where(i < P_TILES, 0, 1 + (i - P_TILES) % S_TILES_PER_REQ)


def _ln(x, eps):
    mu = jnp.mean(x, axis=-1, keepdims=True)
    xc = x - mu
    var = jnp.mean(xc * xc, axis=-1, keepdims=True)
    return xc * lax.rsqrt(var + eps)


def _rms(x, g):
    return x * lax.rsqrt(jnp.mean(x * x, axis=-1, keepdims=True) + 1e-6) * g


def _dot(a, b):
    return jnp.dot(a, b, preferred_element_type=F32)


def _dot_nt(a, b):
    return lax.dot_general(a, b, (((1,), (1,)), ((), ())), preferred_element_type=F32)


def _mod_rows(mod_ref, i):
    m = mod_ref[pl.ds(_group(i), 1), :]
    return m[:, :D_MODEL], m[:, D_MODEL:2 * D_MODEL], m[:, 2 * D_MODEL:]


def _rope(x, cos, sin):
    lane = lax.broadcasted_iota(jnp.int32, x.shape, 1)
    first_half = (lane % AXIS_DIM) < (AXIS_DIM // 2)
    rot = jnp.where(first_half, -pltpu.roll(x, LANE - AXIS_DIM // 2, 1), pltpu.roll(x, AXIS_DIM // 2, 1))
    return x * cos + rot * sin


def _adaln_kernel(c_ref, w_ref, b_ref, o_ref):
    c = c_ref[...]
    s = (c * jax.nn.sigmoid(c)).astype(BF16)
    o_ref[...] = _dot(s, w_ref[...].astype(BF16)) + b_ref[...]


def _adaln(cvec, w_ada, b_ada):
    n_col = 9 * D_MODEL // D_MODEL
    return pl.pallas_call(
        _adaln_kernel,
        grid=(DEPTH, n_col),
        in_specs=[pl.BlockSpec((8, D_MODEL), lambda l, j: (0, 0)),
                  pl.BlockSpec((None, D_MODEL, D_MODEL), lambda l, j: (l, 0, j)),
                  pl.BlockSpec((None, 1, D_MODEL), lambda l, j: (l, 0, j))],
        out_specs=pl.BlockSpec((None, 8, D_MODEL), lambda l, j: (l, 0, j)),
        out_shape=jax.ShapeDtypeStruct((DEPTH, 8, 9 * D_MODEL), F32),
        name="adaln",
    )(cvec, w_ada, b_ada.reshape(DEPTH, 1, 9 * D_MODEL))


def _ffn_kernel(*refs, layer, split_in, split_out):
    n_x = 4 if split_in else 2
    n_o = 2 if split_out else 1
    x_refs = refs[:n_x]
    mod_ref, w1_hbm, w3_hbm, w2_hbm, g_ref, b_ref = refs[n_x:n_x + 6]
    o_refs = refs[n_x + 6:n_x + 6 + n_o]
    h_scr, y_scr, w1_ref, w3_ref, w2_ref, stage_up, stage_down, sem = refs[n_x + 6 + n_o:]
    s = pl.program_id(0)
    cur = s % 2
    nxt = 1 - cur
    pieces = [pl.ds(r, SIDE_ROWS) for r in range(0, TM, SIDE_ROWS)]

    def x_tile(which, tile):
        if not split_in:
            return lambda rows: x_refs[which][rows, :]
        ctx_ref, lat_ref = x_refs[2 * which:2 * which + 2]
        return lambda rows: jnp.where(tile < P_TILES, ctx_ref[rows, :], lat_ref[rows, :])

    x_prev = x_tile(0, s - 1)
    x_next = x_tile(1, s + 1)

    def modulate(x_rows, tile, slot):
        shift, scale, _ = _mod_rows(mod_ref, tile)

        def piece(rows):
            h_scr[slot, rows, :] = (_ln(x_rows(rows), 1e-6) * (1.0 + scale) + shift).astype(BF16)
        return [functools.partial(piece, rows) for rows in pieces]

    def finish():
        _, _, gate = _mod_rows(mod_ref, s - 1)

        def piece(rows):
            z = ALPHA * x_prev(rows) + (0.5 * gate) * y_scr[nxt, rows, :]
            out = _ln(z, 1e-5) * g_ref[...] + b_ref[...]
            if split_out:
                to_ctx = jnp.broadcast_to(s - 1 < P_TILES, out.shape)
                pltpu.store(o_refs[0].at[rows, :], out, mask=to_ctx)
                pltpu.store(o_refs[1].at[rows, :], out, mask=jnp.logical_not(to_ctx))
            else:
                o_refs[0][rows, :] = out
            y_scr[nxt, rows, :] = out
        return [functools.partial(piece, rows) for rows in pieces]

    def load_weights():
        chunks = []
        for src, dst in ((w1_hbm, w1_ref), (w3_hbm, w3_ref)):
            for r in range(0, D_MODEL, UP_STAGE_ROWS):
                rows = pl.ds(r, UP_STAGE_ROWS)
                chunks.append((src.at[layer, rows, :], stage_up, dst.at[rows, :]))
        for r in range(0, D_FF, DOWN_STAGE_ROWS):
            rows = pl.ds(r, DOWN_STAGE_ROWS)
            chunks.append((w2_hbm.at[layer, rows, :], stage_down, w2_ref.at[rows, :]))

        def copy(k):
            src, stage, _ = chunks[k]
            return pltpu.make_async_copy(src, stage.at[k % STAGE_SLOTS], sem.at[k % STAGE_SLOTS])

        for k in range(STAGE_SLOTS - 1):
            copy(k).start()
        for k, (_, stage, dst) in enumerate(chunks):
            if k + STAGE_SLOTS - 1 < len(chunks):
                copy(k + STAGE_SLOTS - 1).start()
            copy(k).wait()
            dst[...] = stage[k % STAGE_SLOTS].astype(BF16)

    def matmuls(side_work):
        side_work = list(side_work)
        chunks = [slice(c0, c1) for c0, c1 in zip(FF_CHUNKS[:-1], FF_CHUNKS[1:])]
        per_dot = -(-len(side_work) // (3 * len(chunks)))

        def dot_with_side(lhs, rhs):
            for _ in range(min(per_dot, len(side_work))):
                side_work.pop(0)()
            return _dot(lhs() if callable(lhs) else lhs, rhs)

        h = lambda: h_scr[cur]

        def gate_up(sl):
            a = dot_with_side(h, w1_ref[:, sl])
            return (a * jax.nn.sigmoid(a) * dot_with_side(h, w3_ref[:, sl])).astype(BF16)

        t_next = gate_up(chunks[0])
        for n, sl in enumerate(chunks):
            t = t_next
            if n + 1 < len(chunks):
                t_next = gate_up(chunks[n + 1])
            yc = dot_with_side(t, w2_ref[sl, :])
            if n == 0:
                y_scr[cur] = yc
            else:
                y_scr[cur] += yc
        for work in side_work:
            work()

    @pl.when(s == 0)
    def _():
        load_weights()
        for work in modulate(x_tile(0, s), s, cur):
            work()
        matmuls(modulate(x_next, s + 1, nxt))

    @pl.when(jnp.logical_and(s > 0, s < N_TILES))
    def _():
        matmuls(finish() + modulate(x_next, s + 1, nxt))

    @pl.when(s == N_TILES)
    def _():
        for work in finish():
            work()


def _resident(shape, index_map):
    return pl.BlockSpec(shape, index_map, pipeline_mode=pl.Buffered(1))


def _ffn(x, mod, w1, w3, w2, ln_g, ln_b, l, sub, split_out=False):
    split_in = isinstance(x, tuple)
    tile = lambda lo, hi, off: pl.BlockSpec((TM, D_MODEL), lambda i: (jnp.clip(i + off, lo, hi) - lo, 0))
    ctx_lat = lambda off: [tile(0, P_TILES - 1, off), tile(P_TILES, N_TILES - 1, off)]
    if split_in:
        x_args = [x[0], x[1]] * 2
        x_specs = ctx_lat(-1) + ctx_lat(1)
    else:
        x_args = [x, x]
        x_specs = [tile(0, N_TILES - 1, -1), tile(0, N_TILES - 1, 1)]
    if split_out:
        out_specs = ctx_lat(-1)
        out_shape = [jax.ShapeDtypeStruct((P_TOK, D_MODEL), F32), jax.ShapeDtypeStruct((S_TOK, D_MODEL), F32)]
    else:
        out_specs = tile(0, N_TILES - 1, -1)
        out_shape = jax.ShapeDtypeStruct((TOKENS, D_MODEL), F32)
    return pl.pallas_call(
        functools.partial(_ffn_kernel, layer=l, split_in=split_in, split_out=split_out),
        grid=(N_TILES + 1,),
        in_specs=x_specs + [
                  _resident((None, 8, 3 * D_MODEL), lambda i: (l, 0, sub)),
                  pl.BlockSpec(memory_space=pl.ANY), pl.BlockSpec(memory_space=pl.ANY),
                  pl.BlockSpec(memory_space=pl.ANY),
                  _resident((None, 1, D_MODEL), lambda i: (3 * l + sub, 0, 0)),
                  _resident((None, 1, D_MODEL), lambda i: (3 * l + sub, 0, 0))],
        out_specs=out_specs,
        out_shape=out_shape,
        scratch_shapes=[pltpu.VMEM((2, TM, D_MODEL), BF16), pltpu.VMEM((2, TM, D_MODEL), F32),
                        pltpu.VMEM((D_MODEL, D_FF), BF16), pltpu.VMEM((D_MODEL, D_FF), BF16),
                        pltpu.VMEM((D_FF, D_MODEL), BF16),
                        pltpu.VMEM((STAGE_SLOTS, UP_STAGE_ROWS, D_FF), F32),
                        pltpu.VMEM((STAGE_SLOTS, DOWN_STAGE_ROWS, D_MODEL), F32),
                        pltpu.SemaphoreType.DMA((STAGE_SLOTS,))],
        compiler_params=pltpu.CompilerParams(dimension_semantics=("arbitrary",),
                                             vmem_limit_bytes=FFN_VMEM_LIMIT),
        name="ffn",
    )(*x_args, mod, w1, w3, w2, ln_g, ln_b)


def _mixer_in_kernel(x_ref, mod_ref, cos_ref, sin_ref, w_in_ref, qn_ref, w_uq_ref, kvn_ref, w_uk_ref, w_uv_ref,
                     ckv_prev, kr_prev, kna_prev, vna_prev,
                     uf_ref, q_ref, k_ref, v_ref, qna_ref, knab_ref, vnab_ref, ckv_ref, kr_ref, kna_ref, vna_ref):
    del ckv_prev, kr_prev, kna_prev, vna_prev
    i = pl.program_id(0)
    shift, scale, _ = _mod_rows(mod_ref, i)
    h = (_ln(x_ref[...], 1e-6) * (1.0 + scale) + shift).astype(BF16)
    cos = cos_ref[...]
    sin = sin_ref[...]

    proj = lambda r0, r1: _dot_nt(h, w_in_ref[r0:r1, :])
    uf_ref[...] = proj(0, U_Q0)

    u_q = proj(U_Q0, U_KV0)
    q = _dot(_rms(u_q, qn_ref[...]).astype(BF16), w_uq_ref[...])
    for hd in range(MLA_HEADS):
        sl = slice(hd * MLA_HP, (hd + 1) * MLA_HP)
        q_ref[:, sl] = _rope(q[:, sl], cos, sin).astype(BF16)

    c_kv = _rms(proj(U_KV0, U_KR0), kvn_ref[...])
    w_kr = jnp.concatenate([jnp.zeros((ROPE_LANE0, D_MODEL), BF16), w_in_ref[U_KR0:U_NA0, :],
                            jnp.zeros((LANE - ROPE_LANE0 - MLA_ROPE, D_MODEL), BF16)], axis=0)
    kr = _dot_nt(h, w_kr)
    k_na = proj(U_NA0 + NA_W, U_NA0 + 2 * NA_W)
    v_na = proj(U_NA0 + 2 * NA_W, U_W)

    @pl.when(i < P_TILES)
    def _():
        kr_t = kr.T[ROPE_LANE0:ROPE_LANE0 + MLA_ROPE, :]
        kna_t = k_na.T
        vna_t = v_na.T
        for b in range(TM // SEQ):
            rows = slice(b * SEQ, (b + 1) * SEQ)
            ckv_ref[b] = c_kv[rows, :]
            kr_ref[b] = kr_t[:, rows]
            kna_ref[b] = kna_t[:, rows]
            vna_ref[b] = vna_t[:, rows]

    c_kv = c_kv.astype(BF16)
    kr = _rope(kr, cos, sin)
    k = _dot(c_kv, w_uk_ref[...])
    for hd in range(MLA_HEADS):
        sl = slice(hd * MLA_HP, (hd + 1) * MLA_HP)
        k_ref[:, sl] = (k[:, sl] + kr).astype(BF16)
    v_ref[...] = _dot(c_kv, w_uv_ref[...]).astype(BF16)

    qna_ref[...] = proj(U_NA0, U_NA0 + NA_W).astype(BF16)
    knab_ref[...] = k_na.astype(BF16)
    vnab_ref[...] = v_na.astype(BF16)


def _cache_shapes():
    return [(BATCH, DEPTH, SEQ, MLA_KV_LORA), (BATCH, DEPTH, MLA_ROPE, SEQ),
            (BATCH, DEPTH, NA_W, SEQ), (BATCH, DEPTH, NA_W, SEQ)]


def _mixer_in(x, mod, cos_t, sin_t, w_in, q_norm, w_uq, kv_norm, w_uk, w_uv, caches, l):
    tile = lambda w: pl.BlockSpec((TM, w), lambda i: (i, 0))
    acts = [(F_W, F32), (MLA_QW, BF16), (MLA_QW, BF16), (MLA_HEADS * MLA_V, BF16), (NA_W, BF16), (NA_W, BF16),
            (NA_W, BF16)]
    cache_spec = lambda s: pl.BlockSpec((TM // SEQ, None) + s[2:],
                                        lambda i: (jnp.minimum(i, P_TILES - 1), l, 0, 0))
    n_in = 10
    return pl.pallas_call(
        _mixer_in_kernel,
        grid=(N_TILES,),
        in_specs=[tile(D_MODEL),
                  _resident((None, 8, 3 * D_MODEL), lambda i: (l, 0, 1)),
                  pl.BlockSpec((TM, LANE), lambda i: (_rope_block(i), 0)),
                  pl.BlockSpec((TM, LANE), lambda i: (_rope_block(i), 0)),
                  _resident((None, U_W, D_MODEL), lambda i: (l, 0, 0)),
                  _resident((None, 1, MLA_Q_LORA), lambda i: (l, 0, 0)),
                  _resident((None, MLA_Q_LORA, MLA_QW), lambda i: (l, 0, 0)),
                  _resident((None, 1, MLA_KV_LORA), lambda i: (l, 0, 0)),
                  _resident((None, MLA_KV_LORA, MLA_QW), lambda i: (l, 0, 0)),
                  _resident((None, MLA_KV_LORA, MLA_HEADS * MLA_V), lambda i: (l, 0, 0))]
                 + [pl.BlockSpec(memory_space=pl.ANY)] * len(caches),
        out_specs=[tile(w) for w, _ in acts] + [cache_spec(s) for s in _cache_shapes()],
        out_shape=[jax.ShapeDtypeStruct((TOKENS, w), dt) for w, dt in acts]
                  + [jax.ShapeDtypeStruct(s, F32) for s in _cache_shapes()],
        input_output_aliases={n_in + n: len(acts) + n for n in range(len(caches))},
        compiler_params=pltpu.CompilerParams(dimension_semantics=("arbitrary",),
                                             vmem_limit_bytes=VMEM_LIMIT),
        name="mixer_in",
    )(x, mod, cos_t, sin_t, w_in, q_norm, w_uq, kv_norm, w_uk, w_uv, *caches)


def _ctx_kv_kernel(ckv_ref, kr_ref, w_uk_ref, w_uv_ref, k_ref, v_ref):
    c = ckv_ref[...].astype(BF16)
    k = _dot(c, w_uk_ref[...])
    kr = kr_ref[...]
    for hd in range(MLA_HEADS):
        sl = slice(hd * MLA_HP, (hd + 1) * MLA_HP)
        k_ref[:, sl] = (k[:, sl] + kr).astype(BF16)
    v_ref[...] = _dot(c, w_uv_ref[...]).astype(BF16)


def _ctx_kv(cache_ckv, cache_kr_pad, w_uk, w_uv):
    return pl.pallas_call(
        _ctx_kv_kernel,
        grid=(DEC_BATCH, DEPTH),
        in_specs=[pl.BlockSpec((None, None, PAST_LEN, MLA_KV_LORA), lambda b, l: (b, l, 0, 0)),
                  pl.BlockSpec((None, None, PAST_LEN, LANE), lambda b, l: (b, l, 0, 0)),
                  pl.BlockSpec((None, MLA_KV_LORA, MLA_QW), lambda b, l: (l, 0, 0)),
                  pl.BlockSpec((None, MLA_KV_LORA, MLA_HEADS * MLA_V), lambda b, l: (l, 0, 0))],
        out_specs=[pl.BlockSpec((None, None, PAST_LEN, MLA_QW), lambda b, l: (b, l, 0, 0)),
                   pl.BlockSpec((None, None, PAST_LEN, MLA_HEADS * MLA_V), lambda b, l: (b, l, 0, 0))],
        out_shape=[jax.ShapeDtypeStruct((DEC_BATCH, DEPTH, PAST_LEN, MLA_QW), BF16),
                   jax.ShapeDtypeStruct((DEC_BATCH, DEPTH, PAST_LEN, MLA_HEADS * MLA_V), BF16)],
        name="ctx_kv",
    )(cache_ckv, cache_kr_pad, w_uk, w_uv)


def _fourier_kernel(x_ref, cs_ref, cl_ref, sl_ref, o_ref):
    length = cl_ref.shape[0]
    x = x_ref[...].astype(BF16)
    xc, xs = [], []
    for g in range(F_GROUPS):
        t = _dot(x[:, g * F_GC:(g + 1) * F_GC], cs_ref[...])
        xc.append(t[:, :F_GC])
        xs.append(t[:, F_GC:])
    xc = jnp.concatenate(xc, axis=1).astype(BF16)
    xs = jnp.concatenate(xs, axis=1).astype(BF16)
    for r in range(0, x_ref.shape[0], length):
        rows = slice(r, r + length)
        o_ref[rows, :] = (_dot(cl_ref[...], xc[rows]) - _dot(sl_ref[...], xs[rows])).astype(BF16)


def _dft_tables(n):
    k = np.arange(n, dtype=np.int64)
    ang = 2.0 * np.pi * ((k[:, None] * k[None, :]) % n).astype(np.float64) / n
    s = n ** -0.5
    return np.cos(ang) * s, np.sin(ang) * s


def _fourier_tables(length):
    cc, sc = _dft_tables(F_GC)
    cl, sl = _dft_tables(length)
    as_bf16 = lambda a: jnp.asarray(a, F32).astype(BF16)
    return as_bf16(np.concatenate([cc, sc], axis=1)), as_bf16(cl), as_bf16(sl)


def _fourier(u_f, tables, length, first_block, n_blocks):
    cs, cl, sl = tables
    rows = max(length, FOURIER_ROWS)
    assert rows % length == 0 and (n_blocks * length) % rows == 0 and (first_block * length) % rows == 0
    first = first_block * length // rows
    return pl.pallas_call(
        _fourier_kernel,
        grid=(n_blocks * length // rows,),
        in_specs=[pl.BlockSpec((rows, F_W), lambda b: (first + b, 0)),
                  _resident((F_GC, 2 * F_GC), lambda b: (0, 0)),
                  _resident((length, length), lambda b: (0, 0)),
                  _resident((length, length), lambda b: (0, 0))],
        out_specs=pl.BlockSpec((rows, F_W), lambda b: (b, 0)),
        out_shape=jax.ShapeDtypeStruct((n_blocks * length, F_W), BF16),
        compiler_params=pltpu.CompilerParams(dimension_semantics=("arbitrary",),
                                             vmem_limit_bytes=VMEM_LIMIT),
        name="fourier_%d" % length,
    )(u_f, cs, cl, sl)


LOG2E = 1.4426950408889634


def _softmax_pv(scores, values_ext, scale):
    m = None
    for s in scores:
        sm = jnp.max(s, axis=-1, keepdims=True)
        m = sm if m is None else jnp.maximum(m, sm)
    acc = None
    for s, v in zip(scores, values_ext):
        p = jnp.exp2((s - m) * (scale * LOG2E)).astype(BF16)
        pv = _dot_nt(p, v.t) if isinstance(v, _KeyMinor) else _dot(p, v)
        acc = pv if acc is None else acc + pv
    return acc[:, :LANE] / acc[:, LANE:]


class _KeyMinor:
    def __init__(self, t):
        self.t = t


def _with_ones(v_pair):
    return jnp.concatenate([v_pair, jnp.ones_like(v_pair)], axis=1)


def _half_masks(rows):
    low = lax.broadcasted_iota(jnp.int32, (rows, LANE), 1) < LANE // 2
    return low, jnp.logical_not(low)


def _head_of_pair(x_pair, mask):
    return jnp.where(mask, x_pair, jnp.zeros_like(x_pair))


def _ctx_attn_kernel(q_ref, k_ref, v_ref, qn_ref, kn_ref, vn_ref, om_ref, on_ref):
    masks = _half_masks(SEQ)
    for r in range(0, CTX_ATTN_SEQS * SEQ, SEQ):
        rows = slice(r, r + SEQ)
        for pair in range(MLA_HEADS // 2):
            ps = slice(pair * LANE, (pair + 1) * LANE)
            v_ext = _with_ones(v_ref[rows, ps])
            o = []
            for half in range(2):
                sl = slice((2 * pair + half) * MLA_HP, (2 * pair + half + 1) * MLA_HP)
                s = _dot_nt(q_ref[rows, sl], k_ref[rows, sl])
                o.append(_softmax_pv([s], [v_ext], MLA_SCALE))
            om_ref[rows, ps] = jnp.where(masks[0], o[0], o[1]).astype(BF16)
        for pair in range(NA_HEADS // 2):
            ps = slice(pair * LANE, (pair + 1) * LANE)
            v_ext = _with_ones(vn_ref[rows, ps])
            q_pair = qn_ref[rows, ps]
            k_pair = kn_ref[rows, ps]
            o = []
            for half in range(2):
                s = _dot_nt(_head_of_pair(q_pair, masks[half]), k_pair)
                o.append(_softmax_pv([s], [v_ext], NA_SCALE))
            on_ref[rows, ps] = jnp.where(masks[0], o[0], o[1]).astype(BF16)


def _ctx_attn(q, k, v, qn, kn, vn):
    blk = lambda w: pl.BlockSpec((CTX_ATTN_SEQS * SEQ, w), lambda b: (b, 0))
    return pl.pallas_call(
        _ctx_attn_kernel,
        grid=(BATCH // CTX_ATTN_SEQS,),
        in_specs=[blk(MLA_QW), blk(MLA_QW), blk(MLA_HEADS * MLA_V), blk(NA_W), blk(NA_W), blk(NA_W)],
        out_specs=[blk(MLA_HEADS * MLA_V), blk(NA_W)],
        out_shape=[jax.ShapeDtypeStruct((P_TOK, MLA_HEADS * MLA_V), BF16),
                   jax.ShapeDtypeStruct((P_TOK, NA_W), BF16)],
        compiler_params=pltpu.CompilerParams(dimension_semantics=("arbitrary",)),
        name="ctx_attn",
    )(q, k, v, qn, kn, vn)


def _lat_mla_kernel(q_ref, k_ref, v_ref, kc_ref, vc_ref, o_ref):
    low, _ = _half_masks(MLA_QB)
    for pair in range(MLA_HEADS // 2):
        ps = slice(pair * LANE, (pair + 1) * LANE)
        v_ext = [_with_ones(v_ref[:, ps]), _with_ones(vc_ref[:, ps])]
        o = []
        for half in range(2):
            sl = slice((2 * pair + half) * MLA_HP, (2 * pair + half + 1) * MLA_HP)
            q = q_ref[:, sl]
            o.append(_softmax_pv([_dot_nt(q, k_ref[:, sl]), _dot_nt(q, kc_ref[:, sl])], v_ext, MLA_SCALE))
        o_ref[:, ps] = jnp.where(low, o[0], o[1]).astype(BF16)


def _lat_mla(q, k, v, kc, vc, l):
    nq = DEC_SEQ // MLA_QB
    q0 = P_TOK // MLA_QB
    r0 = P_TOK // DEC_SEQ
    return pl.pallas_call(
        _lat_mla_kernel,
        grid=(DEC_BATCH, nq),
        in_specs=[pl.BlockSpec((MLA_QB, MLA_QW), lambda b, j: (q0 + b * nq + j, 0)),
                  pl.BlockSpec((DEC_SEQ, MLA_QW), lambda b, j: (r0 + b, 0)),
                  pl.BlockSpec((DEC_SEQ, MLA_HEADS * MLA_V), lambda b, j: (r0 + b, 0)),
                  pl.BlockSpec((None, None, PAST_LEN, MLA_QW), lambda b, j: (b, l, 0, 0)),
                  pl.BlockSpec((None, None, PAST_LEN, MLA_HEADS * MLA_V), lambda b, j: (b, l, 0, 0))],
        out_specs=pl.BlockSpec((MLA_QB, MLA_HEADS * MLA_V), lambda b, j: (b * nq + j, 0)),
        out_shape=jax.ShapeDtypeStruct((S_TOK, MLA_HEADS * MLA_V), BF16),
        compiler_params=pltpu.CompilerParams(dimension_semantics=("arbitrary", "arbitrary"),
                                             vmem_limit_bytes=VMEM_LIMIT),
        name="lat_mla",
    )(q, k, v, kc, vc)


def _na_window_block(j):
    return jnp.clip(j - 1, 0, NA_BLOCKS - NA_WIN_ROWS // NA_QROWS)


def _lat_na_kernel(q_ref, k_ref, v_ref, kc_ref, vc_ref, bias_ref, o_ref):
    j = pl.program_id(1)
    start = pl.multiple_of(_na_window_block(j) * NA_QB, NA_QB)
    k_win = k_ref[pl.ds(start, NA_WIN), :]
    v_win = v_ref[pl.ds(start, NA_WIN), :]
    masks = _half_masks(NA_QB)
    for pair in range(NA_HEADS // 2):
        ps = slice(pair * LANE, (pair + 1) * LANE)
        k_loc = k_win[:, ps]
        k_ctx_t = kc_ref[ps, :].astype(BF16)
        v_ctx_t = vc_ref[ps, :].astype(BF16)
        v_ext = [_with_ones(v_win[:, ps]), _KeyMinor(jnp.concatenate([v_ctx_t, jnp.ones_like(v_ctx_t)], axis=0))]
        q_pair = q_ref[:, ps]
        o = []
        for half in range(2):
            q = _head_of_pair(q_pair, masks[half])
            s_loc = _dot_nt(q, k_loc) + bias_ref[2 * pair + half]
            o.append(_softmax_pv([s_loc, _dot(q, k_ctx_t)], v_ext, NA_SCALE))
        o_ref[:, ps] = jnp.where(masks[0], o[0], o[1]).astype(BF16)


NA_BLOCK_KINDS = (0, 1, NA_BLOCKS - 1)
NA_DIAG = GRID_W - 1
NA_N_DR = 2 * NA_KH - 1


def _na_bias_kernel(e_ref, o_ref):
    lane = lax.broadcasted_iota(jnp.int32, (GRID_W, LANE), 1)
    q_col = lax.broadcasted_iota(jnp.int32, (GRID_W, LANE), 0)
    k_col = lane % GRID_W
    col_start = jnp.clip(q_col - NA_KW // 2, 0, GRID_W - NA_KW)
    col_ok = (k_col >= col_start) & (k_col < col_start + NA_KW)
    low_half = lane < GRID_W
    neg = jnp.full((GRID_W, LANE), NEG_INF, F32)
    tiles = {}

    def toeplitz(dr, half):
        if (dr, half) not in tiles:
            row = jnp.broadcast_to(e_ref[dr:dr + 1, :], (GRID_W, LANE))
            shift = (LANE - NA_DIAG + half * GRID_W) % LANE
            tiles[dr, half] = pltpu.roll(row, shift, 1, stride=1, stride_axis=0)
        return tiles[dr, half]

    for kind, blk in enumerate(NA_BLOCK_KINDS):
        first_key_row = NA_QROWS * min(max(blk - 1, 0), NA_BLOCKS - NA_WIN_ROWS // NA_QROWS)
        for qr in range(NA_QROWS):
            q_row = NA_QROWS * blk + qr
            row_start = min(max(q_row - NA_KH // 2, 0), GRID_H - NA_KH)
            for pair in range(NA_WIN_ROWS // 2):
                halves = []
                for half in range(2):
                    k_row = first_key_row + 2 * pair + half
                    inside = row_start <= k_row < row_start + NA_KH
                    halves.append(toeplitz(k_row - q_row + NA_KH - 1, half) if inside else neg)
                tile = jnp.where(low_half, halves[0], halves[1])
                o_ref[kind, qr * GRID_W:(qr + 1) * GRID_W, pair * LANE:(pair + 1) * LANE] = (
                    jnp.where(col_ok, tile, NEG_INF))


def _na_bias_table(rpb):
    left = NA_DIAG - (NA_KW - 1)
    e = jnp.concatenate([jnp.broadcast_to(rpb[..., :1], rpb.shape[:-1] + (left,)), rpb,
                         jnp.broadcast_to(rpb[..., -1:], rpb.shape[:-1] + (LANE - left - rpb.shape[-1],))],
                        axis=-1) * (1.0 / NA_SCALE)
    return pl.pallas_call(
        _na_bias_kernel,
        grid=(DEPTH, NA_HEADS),
        in_specs=[pl.BlockSpec((None, None, NA_N_DR, LANE), lambda l, h: (l, h, 0, 0))],
        out_specs=pl.BlockSpec((None, len(NA_BLOCK_KINDS), None, NA_QB, NA_WIN), lambda l, h: (l, 0, h, 0, 0)),
        out_shape=jax.ShapeDtypeStruct((DEPTH, len(NA_BLOCK_KINDS), NA_HEADS, NA_QB, NA_WIN), F32),
        name="na_bias",
    )(e)


def _lat_na(q, k, v, cache_k, cache_v, bias, l):
    q0 = P_TOK // NA_QB
    r0 = P_TOK // DEC_SEQ
    kind = lambda j: (j > 0).astype(jnp.int32) + (j == NA_BLOCKS - 1).astype(jnp.int32)
    return pl.pallas_call(
        _lat_na_kernel,
        grid=(DEC_BATCH, NA_BLOCKS),
        in_specs=[pl.BlockSpec((NA_QB, NA_W), lambda b, j: (q0 + b * NA_BLOCKS + j, 0)),
                  pl.BlockSpec((DEC_SEQ, NA_W), lambda b, j: (r0 + b, 0)),
                  pl.BlockSpec((DEC_SEQ, NA_W), lambda b, j: (r0 + b, 0)),
                  pl.BlockSpec((None, None, PAST_LEN, NA_W), lambda b, j: (b, l, 0, 0)),
                  pl.BlockSpec((None, None, PAST_LEN, NA_W), lambda b, j: (b, l, 0, 0)),
                  pl.BlockSpec((None, None, NA_HEADS, NA_QB, NA_WIN), lambda b, j: (l, kind(j), 0, 0, 0))],
        out_specs=pl.BlockSpec((NA_QB, NA_W), lambda b, j: (b * NA_BLOCKS + j, 0)),
        out_shape=jax.ShapeDtypeStruct((S_TOK, NA_W), BF16),
        compiler_params=pltpu.CompilerParams(dimension_semantics=("arbitrary", "arbitrary"),
                                             vmem_limit_bytes=VMEM_LIMIT),
        name="lat_na",
    )(q, k, v, cache_k, cache_v, bias)


def _merge_kernel(x_ref, mod_ref, yfc_ref, ymc_ref, ync_ref, yfl_ref, yml_ref, ynl_ref,
                  wg_ref, bg_ref, wf_ref, wm_ref, wn_ref, wo_ref, g_ref, b_ref, o_ref):
    i = pl.program_id(0)
    shift, scale, gate = _mod_rows(mod_ref, i)
    x = x_ref[...]
    h = (_ln(x, 1e-6) * (1.0 + scale) + shift).astype(BF16)
    is_ctx = i < P_TILES
    mix = None
    for n, (yc_ref, yl_ref, w_ref) in enumerate(((yfc_ref, yfl_ref, wf_ref), (ymc_ref, yml_ref, wm_ref),
                                                 (ync_ref, ynl_ref, wn_ref))):
        sl = slice(n * D_MODEL, (n + 1) * D_MODEL)
        g = jax.nn.sigmoid(_dot(h, wg_ref[:, sl]) + bg_ref[:, sl])
        y = jnp.where(is_ctx, yc_ref[...], yl_ref[...])
        t = g * _dot(y, w_ref[...])
        mix = t if mix is None else mix + t
    z = ALPHA * x + gate * _dot(mix.astype(BF16), wo_ref[...])
    o_ref[...] = _ln(z, 1e-5) * g_ref[...] + b_ref[...]


def _merge(x, mod, y_ctx, y_lat, w_gate, b_gate, w_f, w_m, w_n, w_out, ln_g, ln_b, l):
    tile = lambda w: pl.BlockSpec((TM, w), lambda i: (i, 0))
    ctx_tile = lambda w: pl.BlockSpec((TM, w), lambda i: (jnp.minimum(i, P_TILES - 1), 0))
    lat_tile = lambda w: pl.BlockSpec((TM, w), lambda i: (jnp.maximum(i - P_TILES, 0), 0))
    widths = (F_W, MLA_HEADS * MLA_V, NA_W)
    return pl.pallas_call(
        _merge_kernel,
        grid=(N_TILES,),
        in_specs=[tile(D_MODEL),
                  _resident((None, 8, 3 * D_MODEL), lambda i: (l, 0, 1))]
                 + [ctx_tile(w) for w in widths] + [lat_tile(w) for w in widths] + [
                  _resident((None, D_MODEL, 3 * D_MODEL), lambda i: (l, 0, 0)),
                  _resident((None, 1, 3 * D_MODEL), lambda i: (l, 0, 0)),
                  _resident((None, F_W, D_MODEL), lambda i: (l, 0, 0)),
                  _resident((None, MLA_HEADS * MLA_V, D_MODEL), lambda i: (l, 0, 0)),
                  _resident((None, NA_W, D_MODEL), lambda i: (l, 0, 0)),
                  _resident((None, D_MODEL, D_MODEL), lambda i: (l, 0, 0)),
                  _resident((None, 1, D_MODEL), lambda i: (3 * l + 1, 0, 0)),
                  _resident((None, 1, D_MODEL), lambda i: (3 * l + 1, 0, 0))],
        out_specs=tile(D_MODEL),
        out_shape=jax.ShapeDtypeStruct((TOKENS, D_MODEL), F32),
        compiler_params=pltpu.CompilerParams(dimension_semantics=("arbitrary",),
                                             vmem_limit_bytes=VMEM_LIMIT),
        name="merge",
    )(x, mod, *y_ctx, *y_lat, w_gate, b_gate, w_f, w_m, w_n, w_out, ln_g, ln_b)


def _pad_heads(w, n_heads, width):
    lead = w.shape[:-1]
    w = w.reshape(lead + (n_heads, width))
    w = jnp.pad(w, [(0, 0)] * len(lead) + [(0, 0), (0, MLA_HP - width)])
    return w.reshape(lead + (n_heads * MLA_HP,))


def _rope_tables():
    t = jnp.arange(DEC_SEQ, dtype=jnp.int32)
    pos = jnp.stack([t // GRID_W, t % GRID_W], axis=-1).astype(F32)
    half = AXIS_DIM // 2
    inv_freq = ROPE_BASE ** (-jnp.arange(half, dtype=F32) / half)
    ang = pos[:, :, None] * inv_freq
    ang = jnp.concatenate([ang, ang], axis=-1).reshape(DEC_SEQ, MLA_ROPE)
    pad = lambda a, fill: jnp.pad(a, ((0, 0), (ROPE_LANE0, LANE - ROPE_LANE0 - MLA_ROPE)), constant_values=fill)
    cos = jnp.concatenate([jnp.ones((TM, LANE), F32), pad(jnp.cos(ang), 1.0)], axis=0)
    sin = jnp.concatenate([jnp.zeros((TM, LANE), F32), pad(jnp.sin(ang), 0.0)], axis=0)
    return cos, sin


def kernel(x_prompt, x_sample, cache_mla_ckv, cache_mla_krope, cache_na_k, cache_na_v, c, c_ctx, w_ada, b_ada, ffn1_w1, ffn1_w3, ffn1_w2, ffn2_w1, ffn2_w3, ffn2_w2, w_in, mla_q_norm, mla_w_uq, mla_kv_norm, mla_w_ukv, na_rpb, w_branch_f, w_branch_m, w_branch_n, w_gate, b_gate, w_out, ln_g, ln_b):
    bf = lambda w: w.astype(BF16)
    f1 = (ffn1_w1, ffn1_w3, ffn1_w2)
    f2 = (ffn2_w1, ffn2_w3, ffn2_w2)
    w_in_p = bf(jnp.swapaxes(w_in, 1, 2))
    w_uq_p = bf(_pad_heads(mla_w_uq, MLA_HEADS, MLA_NOPE + MLA_ROPE))
    ukv = mla_w_ukv.reshape(DEPTH, MLA_KV_LORA, MLA_HEADS, MLA_NOPE + MLA_V)
    w_uk_p = bf(_pad_heads(ukv[..., :MLA_NOPE].reshape(DEPTH, MLA_KV_LORA, MLA_HEADS * MLA_NOPE), MLA_HEADS, MLA_NOPE))
    w_uv = bf(ukv[..., MLA_NOPE:].reshape(DEPTH, MLA_KV_LORA, MLA_HEADS * MLA_V))
    w_gate_b, w_f, w_m, w_n, w_out_b = bf(w_gate), bf(w_branch_f), bf(w_branch_m), bf(w_branch_n), bf(w_out)
    q_norm = mla_q_norm.reshape(DEPTH, 1, MLA_Q_LORA)
    kv_norm = mla_kv_norm.reshape(DEPTH, 1, MLA_KV_LORA)
    b_gate3 = b_gate.reshape(DEPTH, 1, 3 * D_MODEL)
    g3 = ln_g.reshape(DEPTH * 3, 1, D_MODEL)
    b3 = ln_b.reshape(DEPTH * 3, 1, D_MODEL)
    cos_t, sin_t = _rope_tables()
    dft_ctx = _fourier_tables(SEQ)
    dft_lat = _fourier_tables(DEC_SEQ)
    kr_pad = jnp.pad(cache_mla_krope, ((0, 0), (0, 0), (0, 0), (ROPE_LANE0, LANE - ROPE_LANE0 - MLA_ROPE)))
    ch_major = lambda a: jnp.transpose(a, (0, 1, 3, 4, 2)).reshape(DEC_BATCH, DEPTH, NA_W, PAST_LEN)
    cache_k = ch_major(cache_na_k)
    cache_v = ch_major(cache_na_v)

    cvec = jnp.concatenate([c_ctx[None], c, jnp.zeros((8 - 1 - DEC_BATCH, D_MODEL), F32)], axis=0)
    mod = _adaln(cvec, w_ada, b_ada)
    kc, vc = _ctx_kv(cache_mla_ckv, kr_pad, w_uk_p, w_uv)
    na_bias = _na_bias_table(na_rpb)

    x = (x_prompt.reshape(P_TOK, D_MODEL), x_sample.reshape(S_TOK, D_MODEL))
    caches = [jnp.zeros(s, F32) for s in _cache_shapes()]
    for l in range(DEPTH):
        x = _ffn(x, mod, *f1, g3, b3, l, 0)
        u_f, q, k, v, q_n, k_nb, v_nb, *caches = _mixer_in(
            x, mod, cos_t, sin_t, w_in_p, q_norm, w_uq_p, kv_norm, w_uk_p, w_uv, caches, l)
        yf_c = _fourier(u_f, dft_ctx, SEQ, 0, BATCH)
        yf_l = _fourier(u_f, dft_lat, DEC_SEQ, P_TOK // DEC_SEQ, DEC_BATCH)
        ym_c, yn_c = _ctx_attn(q, k, v, q_n, k_nb, v_nb)
        ym_l = _lat_mla(q, k, v, kc, vc, l)
        yn_l = _lat_na(q_n, k_nb, v_nb, cache_k, cache_v, na_bias, l)
        x = _merge(x, mod, (yf_c, ym_c, yn_c), (yf_l, ym_l, yn_l), w_gate_b, b_gate3, w_f, w_m, w_n, w_out_b,
                   g3, b3, l)
        x = _ffn(x, mod, *f2, g3, b3, l, 2, split_out=(l == DEPTH - 1))
    ckv, kr_t, nak_t, nav_t = caches
    per_head = lambda a: jnp.transpose(a.reshape(BATCH, DEPTH, NA_HEADS, NA_HEAD_DIM, SEQ), (0, 1, 4, 2, 3))
    return (x[0].reshape(BATCH, SEQ, D_MODEL), x[1].reshape(DEC_BATCH, DEC_SEQ, D_MODEL),
            ckv, jnp.transpose(kr_t, (0, 1, 3, 2)), per_head(nak_t), per_head(nav_t))
```

```python
import functools

import numpy as np
import jax
import jax.numpy as jnp
from jax import lax
from jax.experimental import pallas as pl
from jax.experimental.pallas import tpu as pltpu

F32 = jnp.float32
BF16 = jnp.bfloat16

D_MODEL = 1024
BATCH = 32
SEQ = 256
DEPTH = 4
DEC_BATCH = 2
DEC_SEQ = 2048
PAST_LEN = 512
GRID_W = 64
GRID_H = DEC_SEQ // GRID_W
D_FF = 2816
F_GROUPS = 4
F_GC = 128
F_W = F_GROUPS * F_GC
MLA_HEADS = 8
MLA_Q_LORA = 384
MLA_KV_LORA = 256
MLA_NOPE = 64
MLA_ROPE = 32
MLA_V = 64
NA_HEADS = 8
NA_HEAD_DIM = 64
NA_KH = 8
NA_KW = 16
NA_W = NA_HEADS * NA_HEAD_DIM
ROPE_BASE = 10000.0
AXIS_DIM = MLA_ROPE // 2
ALPHA = (2.0 * DEPTH) ** 0.25
MLA_SCALE = (MLA_NOPE + MLA_ROPE) ** -0.5
NA_SCALE = NA_HEAD_DIM ** -0.5
NEG_INF = -1e30

LANE = 128
MLA_HP = LANE
MLA_QW = MLA_HEADS * MLA_HP
P_TOK = BATCH * SEQ
S_TOK = DEC_BATCH * DEC_SEQ
TOKENS = P_TOK + S_TOK
TM = 512
SIDE_ROWS = 64
N_TILES = TOKENS // TM
P_TILES = P_TOK // TM
S_TILES_PER_REQ = DEC_SEQ // TM
MXU_TILE = 256
FF_CHUNKS = (0, 4 * MXU_TILE, 8 * MXU_TILE, D_FF)
U_Q0 = F_W
U_KV0 = U_Q0 + MLA_Q_LORA
U_KR0 = U_KV0 + MLA_KV_LORA
U_NA0 = U_KR0 + MLA_ROPE
U_W = U_NA0 + 3 * NA_W
ROPE_LANE0 = MLA_NOPE
NA_QROWS = 4
NA_QB = NA_QROWS * GRID_W
NA_WIN_ROWS = NA_QROWS + NA_KH
NA_WIN = NA_WIN_ROWS * GRID_W
NA_BLOCKS = GRID_H // NA_QROWS
MLA_QB = 1024
CTX_ATTN_SEQS = 4
FOURIER_ROWS = 1024
VMEM_LIMIT = 56 * 1024 * 1024
FFN_VMEM_LIMIT = 60 * 1024 * 1024
STAGE_SLOTS = 4
UP_STAGE_ROWS = 64
DOWN_STAGE_ROWS = D_FF // 16


def _group(i):
    return jnp.where(i < P_TILES, 0, 1 + (i - P_TILES) // S_TILES_PER_REQ)


def _rope_block(i):
    return jnp.where(i < P_TILES, 0, 1 + (i - P_TILES) % S_TILES_PER_REQ)


def _ln(x, eps):
    mu = jnp.mean(x, axis=-1, keepdims=True)
    xc = x - mu
    var = jnp.mean(xc * xc, axis=-1, keepdims=True)
    return xc * lax.rsqrt(var + eps)


def _rms(x, g):
    return x * lax.rsqrt(jnp.mean(x * x, axis=-1, keepdims=True) + 1e-6) * g


def _dot(a, b):
    return jnp.dot(a, b, preferred_element_type=F32)


def _dot_nt(a, b):
    return lax.dot_general(a, b, (((1,), (1,)), ((), ())), preferred_element_type=F32)


def _mod_rows(mod_ref, i):
    m = mod_ref[pl.ds(_group(i), 1), :]
    return m[:, :D_MODEL], m[:, D_MODEL:2 * D_MODEL], m[:, 2 * D_MODEL:]


def _rope(x, cos, sin):
    lane = lax.broadcasted_iota(jnp.int32, x.shape, 1)
    first_half = (lane % AXIS_DIM) < (AXIS_DIM // 2)
    rot = jnp.where(first_half, -pltpu.roll(x, LANE - AXIS_DIM // 2, 1), pltpu.roll(x, AXIS_DIM // 2, 1))
    return x * cos + rot * sin


def _adaln_kernel(c_ref, w_ref, b_ref, o_ref):
    c = c_ref[...]
    s = (c * jax.nn.sigmoid(c)).astype(BF16)
    o_ref[...] = _dot(s, w_ref[...].astype(BF16)) + b_ref[...]


def _adaln(cvec, w_ada, b_ada):
    n_col = 9 * D_MODEL // D_MODEL
    return pl.pallas_call(
        _adaln_kernel,
        grid=(DEPTH, n_col),
        in_specs=[pl.BlockSpec((8, D_MODEL), lambda l, j: (0, 0)),
                  pl.BlockSpec((None, D_MODEL, D_MODEL), lambda l, j: (l, 0, j)),
                  pl.BlockSpec((None, 1, D_MODEL), lambda l, j: (l, 0, j))],
        out_specs=pl.BlockSpec((None, 8, D_MODEL), lambda l, j: (l, 0, j)),
        out_shape=jax.ShapeDtypeStruct((DEPTH, 8, 9 * D_MODEL), F32),
        name="adaln",
    )(cvec, w_ada, b_ada.reshape(DEPTH, 1, 9 * D_MODEL))


def _ffn_kernel(*refs, layer, split_in, split_out, emit_h):
    refs = list(refs)
    x_refs = [refs.pop(0) for _ in range(4 if split_in else 2)]
    mod_ref = refs.pop(0)
    mod_next_ref = refs.pop(0) if emit_h else None
    w1_hbm, w3_hbm, w2_hbm, g_ref, b_ref = [refs.pop(0) for _ in range(5)]
    o_refs = [refs.pop(0) for _ in range(2 if split_out else 1)]
    hn_ref = refs.pop(0) if emit_h else None
    h_scr, y_scr, w1_ref, w3_ref, w2_ref, stage_up, stage_down, sem = refs
    s = pl.program_id(0)
    cur = s % 2
    nxt = 1 - cur
    pieces = [pl.ds(r, SIDE_ROWS) for r in range(0, TM, SIDE_ROWS)]

    def x_tile(which, tile):
        if not split_in:
            return lambda rows: x_refs[which][rows, :]
        ctx_ref, lat_ref = x_refs[2 * which:2 * which + 2]
        return lambda rows: jnp.where(tile < P_TILES, ctx_ref[rows, :], lat_ref[rows, :])

    x_prev = x_tile(0, s - 1)
    x_next = x_tile(1, s + 1)

    def modulate(x_rows, tile, slot):
        shift, scale, _ = _mod_rows(mod_ref, tile)

        def piece(rows):
            h_scr[slot, rows, :] = (_ln(x_rows(rows), 1e-6) * (1.0 + scale) + shift).astype(BF16)
        return [functools.partial(piece, rows) for rows in pieces]

    def finish():
        _, _, gate = _mod_rows(mod_ref, s - 1)
        if emit_h:
            shift_n, scale_n, _ = _mod_rows(mod_next_ref, s - 1)

        def piece(rows):
            z = ALPHA * x_prev(rows) + (0.5 * gate) * y_scr[nxt, rows, :]
            out = _ln(z, 1e-5) * g_ref[...] + b_ref[...]
            if split_out:
                to_ctx = jnp.broadcast_to(s - 1 < P_TILES, out.shape)
                pltpu.store(o_refs[0].at[rows, :], out, mask=to_ctx)
                pltpu.store(o_refs[1].at[rows, :], out, mask=jnp.logical_not(to_ctx))
            else:
                o_refs[0][rows, :] = out
            if emit_h:
                hn_ref[rows, :] = (_ln(out, 1e-6) * (1.0 + scale_n) + shift_n).astype(BF16)
            y_scr[nxt, rows, :] = out
        return [functools.partial(piece, rows) for rows in pieces]

    def load_weights():
        chunks = []
        for src, dst in ((w1_hbm, w1_ref), (w3_hbm, w3_ref)):
            for r in range(0, D_MODEL, UP_STAGE_ROWS):
                rows = pl.ds(r, UP_STAGE_ROWS)
                chunks.append((src.at[layer, rows, :], stage_up, dst.at[rows, :]))
        for r in range(0, D_FF, DOWN_STAGE_ROWS):
            rows = pl.ds(r, DOWN_STAGE_ROWS)
            chunks.append((w2_hbm.at[layer, rows, :], stage_down, w2_ref.at[rows, :]))

        def copy(k):
            src, stage, _ = chunks[k]
            return pltpu.make_async_copy(src, stage.at[k % STAGE_SLOTS], sem.at[k % STAGE_SLOTS])

        for k in range(STAGE_SLOTS - 1):
            copy(k).start()
        for k, (_, stage, dst) in enumerate(chunks):
            if k + STAGE_SLOTS - 1 < len(chunks):
                copy(k + STAGE_SLOTS - 1).start()
            copy(k).wait()
            dst[...] = stage[k % STAGE_SLOTS].astype(BF16)

    def matmuls(side_work):
        side_work = list(side_work)
        chunks = [slice(c0, c1) for c0, c1 in zip(FF_CHUNKS[:-1], FF_CHUNKS[1:])]
        per_dot = -(-len(side_work) // (3 * len(chunks)))

        def dot_with_side(lhs, rhs):
            for _ in range(min(per_dot, len(side_work))):
                side_work.pop(0)()
            return _dot(lhs() if callable(lhs) else lhs, rhs)

        h = lambda: h_scr[cur]

        def gate_up(sl):
            a = dot_with_side(h, w1_ref[:, sl])
            return (a * jax.nn.sigmoid(a) * dot_with_side(h, w3_ref[:, sl])).astype(BF16)

        t_next = gate_up(chunks[0])
        for n, sl in enumerate(chunks):
            t = t_next
            if n + 1 < len(chunks):
                t_next = gate_up(chunks[n + 1])
            yc = dot_with_side(t, w2_ref[sl, :])
            if n == 0:
                y_scr[cur] = yc
            else:
                y_scr[cur] += yc
        for work in side_work:
            work()

    @pl.when(s == 0)
    def _():
        load_weights()
        for work in modulate(x_tile(0, s), s, cur):
            work()
        matmuls(modulate(x_next, s + 1, nxt))

    @pl.when(jnp.logical_and(s > 0, s < N_TILES))
    def _():
        matmuls(finish() + modulate(x_next, s + 1, nxt))

    @pl.when(s == N_TILES)
    def _():
        for work in finish():
            work()


def _resident(shape, index_map):
    return pl.BlockSpec(shape, index_map, pipeline_mode=pl.Buffered(1))


def _ffn(x, mod, w1, w3, w2, ln_g, ln_b, l, sub, split_out=False, emit_h=False):
    split_in = isinstance(x, tuple)
    tile = lambda lo, hi, off: pl.BlockSpec((TM, D_MODEL), lambda i: (jnp.clip(i + off, lo, hi) - lo, 0))
    ctx_lat = lambda off: [tile(0, P_TILES - 1, off), tile(P_TILES, N_TILES - 1, off)]
    if split_in:
        x_args = [x[0], x[1]] * 2
        x_specs = ctx_lat(-1) + ctx_lat(1)
    else:
        x_args = [x, x]
        x_specs = [tile(0, N_TILES - 1, -1), tile(0, N_TILES - 1, 1)]
    if split_out:
        out_specs = ctx_lat(-1)
        out_shape = [jax.ShapeDtypeStruct((P_TOK, D_MODEL), F32), jax.ShapeDtypeStruct((S_TOK, D_MODEL), F32)]
    else:
        out_specs = [tile(0, N_TILES - 1, -1)]
        out_shape = [jax.ShapeDtypeStruct((TOKENS, D_MODEL), F32)]
    mod_specs = [_resident((None, 8, 3 * D_MODEL), lambda i: (l, 0, sub))]
    mod_args = [mod]
    if emit_h:
        mod_specs.append(_resident((None, 8, 3 * D_MODEL), lambda i: (l, 0, sub + 1)))
        mod_args.append(mod)
        out_specs = out_specs + [tile(0, N_TILES - 1, -1)]
        out_shape = out_shape + [jax.ShapeDtypeStruct((TOKENS, D_MODEL), BF16)]
    return pl.pallas_call(
        functools.partial(_ffn_kernel, layer=l, split_in=split_in, split_out=split_out, emit_h=emit_h),
        grid=(N_TILES + 1,),
        in_specs=x_specs + mod_specs + [
                  pl.BlockSpec(memory_space=pl.ANY), pl.BlockSpec(memory_space=pl.ANY),
                  pl.BlockSpec(memory_space=pl.ANY),
                  _resident((None, 1, D_MODEL), lambda i: (3 * l + sub, 0, 0)),
                  _resident((None, 1, D_MODEL), lambda i: (3 * l + sub, 0, 0))],
        out_specs=out_specs,
        out_shape=out_shape,
        scratch_shapes=[pltpu.VMEM((2, TM, D_MODEL), BF16), pltpu.VMEM((2, TM, D_MODEL), F32),
                        pltpu.VMEM((D_MODEL, D_FF), BF16), pltpu.VMEM((D_MODEL, D_FF), BF16),
                        pltpu.VMEM((D_FF, D_MODEL), BF16),
                        pltpu.VMEM((STAGE_SLOTS, UP_STAGE_ROWS, D_FF), F32),
                        pltpu.VMEM((STAGE_SLOTS, DOWN_STAGE_ROWS, D_MODEL), F32),
                        pltpu.SemaphoreType.DMA((STAGE_SLOTS,))],
        compiler_params=pltpu.CompilerParams(dimension_semantics=("arbitrary",),
                                             vmem_limit_bytes=FFN_VMEM_LIMIT),
        name="ffn",
    )(*x_args, *mod_args, w1, w3, w2, ln_g, ln_b)


def _mixer_in_kernel(h_ref, cos_ref, sin_ref, w_in_ref, qn_ref, w_uq_ref, kvn_ref, w_uk_ref, w_uv_ref,
                     ckv_prev, kr_prev, kna_prev, vna_prev,
                     uf_ref, q_ref, k_ref, v_ref, qna_ref, knab_ref, vnab_ref, ckv_ref, kr_ref, kna_ref, vna_ref):
    del ckv_prev, kr_prev, kna_prev, vna_prev
    i = pl.program_id(0)
    h = h_ref[...]
    cos = cos_ref[...]
    sin = sin_ref[...]

    proj = lambda r0, r1: _dot_nt(h, w_in_ref[r0:r1, :])
    uf_ref[...] = proj(0, U_Q0)

    u_q = proj(U_Q0, U_KV0)
    q = _dot(_rms(u_q, qn_ref[...]).astype(BF16), w_uq_ref[...])
    for hd in range(MLA_HEADS):
        sl = slice(hd * MLA_HP, (hd + 1) * MLA_HP)
        q_ref[:, sl] = _rope(q[:, sl], cos, sin).astype(BF16)

    c_kv = _rms(proj(U_KV0, U_KR0), kvn_ref[...])
    w_kr = jnp.concatenate([jnp.zeros((ROPE_LANE0, D_MODEL), BF16), w_in_ref[U_KR0:U_NA0, :],
                            jnp.zeros((LANE - ROPE_LANE0 - MLA_ROPE, D_MODEL), BF16)], axis=0)
    kr = _dot_nt(h, w_kr)
    k_na = proj(U_NA0 + NA_W, U_NA0 + 2 * NA_W)
    v_na = proj(U_NA0 + 2 * NA_W, U_W)

    @pl.when(i < P_TILES)
    def _():
        kr_t = kr.T[ROPE_LANE0:ROPE_LANE0 + MLA_ROPE, :]
        kna_t = k_na.T
        vna_t = v_na.T
        for b in range(TM // SEQ):
            rows = slice(b * SEQ, (b + 1) * SEQ)
            ckv_ref[b] = c_kv[rows, :]
            kr_ref[b] = kr_t[:, rows]
            kna_ref[b] = kna_t[:, rows]
            vna_ref[b] = vna_t[:, rows]

    c_kv = c_kv.astype(BF16)
    kr = _rope(kr, cos, sin)
    k = _dot(c_kv, w_uk_ref[...])
    for hd in range(MLA_HEADS):
        sl = slice(hd * MLA_HP, (hd + 1) * MLA_HP)
        k_ref[:, sl] = (k[:, sl] + kr).astype(BF16)
    v_ref[...] = _dot(c_kv, w_uv_ref[...]).astype(BF16)

    qna_ref[...] = proj(U_NA0, U_NA0 + NA_W).astype(BF16)
    knab_ref[...] = k_na.astype(BF16)
    vnab_ref[...] = v_na.astype(BF16)


def _cache_shapes():
    return [(BATCH, DEPTH, SEQ, MLA_KV_LORA), (BATCH, DEPTH, MLA_ROPE, SEQ),
            (BATCH, DEPTH, NA_W, SEQ), (BATCH, DEPTH, NA_W, SEQ)]


def _mixer_in(h, cos_t, sin_t, w_in, q_norm, w_uq, kv_norm, w_uk, w_uv, caches, l):
    tile = lambda w: pl.BlockSpec((TM, w), lambda i: (i, 0))
    acts = [(F_W, F32), (MLA_QW, BF16), (MLA_QW, BF16), (MLA_HEADS * MLA_V, BF16), (NA_W, BF16), (NA_W, BF16),
            (NA_W, BF16)]
    cache_spec = lambda s: pl.BlockSpec((TM // SEQ, None) + s[2:],
                                        lambda i: (jnp.minimum(i, P_TILES - 1), l, 0, 0))
    n_in = 9
    return pl.pallas_call(
        _mixer_in_kernel,
        grid=(N_TILES,),
        in_specs=[tile(D_MODEL),
                  pl.BlockSpec((TM, LANE), lambda i: (_rope_block(i), 0)),
                  pl.BlockSpec((TM, LANE), lambda i: (_rope_block(i), 0)),
                  _resident((None, U_W, D_MODEL), lambda i: (l, 0, 0)),
                  _resident((None, 1, MLA_Q_LORA), lambda i: (l, 0, 0)),
                  _resident((None, MLA_Q_LORA, MLA_QW), lambda i: (l, 0, 0)),
                  _resident((None, 1, MLA_KV_LORA), lambda i: (l, 0, 0)),
                  _resident((None, MLA_KV_LORA, MLA_QW), lambda i: (l, 0, 0)),
                  _resident((None, MLA_KV_LORA, MLA_HEADS * MLA_V), lambda i: (l, 0, 0))]
                 + [pl.BlockSpec(memory_space=pl.ANY)] * len(caches),
        out_specs=[tile(w) for w, _ in acts] + [cache_spec(s) for s in _cache_shapes()],
        out_shape=[jax.ShapeDtypeStruct((TOKENS, w), dt) for w, dt in acts]
                  + [jax.ShapeDtypeStruct(s, F32) for s in _cache_shapes()],
        input_output_aliases={n_in + n: len(acts) + n for n in range(len(caches))},
        compiler_params=pltpu.CompilerParams(dimension_semantics=("arbitrary",),
                                             vmem_limit_bytes=VMEM_LIMIT),
        name="mixer_in",
    )(h, cos_t, sin_t, w_in, q_norm, w_uq, kv_norm, w_uk, w_uv, *caches)


def _ctx_kv_kernel(ckv_ref, kr_ref, w_uk_ref, w_uv_ref, k_ref, v_ref):
    c = ckv_ref[...].astype(BF16)
    k = _dot(c, w_uk_ref[...])
    kr = kr_ref[...]
    for hd in range(MLA_HEADS):
        sl = slice(hd * MLA_HP, (hd + 1) * MLA_HP)
        k_ref[:, sl] = (k[:, sl] + kr).astype(BF16)
    v_ref[...] = _dot(c, w_uv_ref[...]).astype(BF16)


def _ctx_kv(cache_ckv, cache_kr_pad, w_uk, w_uv):
    return pl.pallas_call(
        _ctx_kv_kernel,
        grid=(DEC_BATCH, DEPTH),
        in_specs=[pl.BlockSpec((None, None, PAST_LEN, MLA_KV_LORA), lambda b, l: (b, l, 0, 0)),
                  pl.BlockSpec((None, None, PAST_LEN, LANE), lambda b, l: (b, l, 0, 0)),
                  pl.BlockSpec((None, MLA_KV_LORA, MLA_QW), lambda b, l: (l, 0, 0)),
                  pl.BlockSpec((None, MLA_KV_LORA, MLA_HEADS * MLA_V), lambda b, l: (l, 0, 0))],
        out_specs=[pl.BlockSpec((None, None, PAST_LEN, MLA_QW), lambda b, l: (b, l, 0, 0)),
                   pl.BlockSpec((None, None, PAST_LEN, MLA_HEADS * MLA_V), lambda b, l: (b, l, 0, 0))],
        out_shape=[jax.ShapeDtypeStruct((DEC_BATCH, DEPTH, PAST_LEN, MLA_QW), BF16),
                   jax.ShapeDtypeStruct((DEC_BATCH, DEPTH, PAST_LEN, MLA_HEADS * MLA_V), BF16)],
        name="ctx_kv",
    )(cache_ckv, cache_kr_pad, w_uk, w_uv)


def _fourier_kernel(x_ref, cs_ref, cl_ref, sl_ref, o_ref):
    length = cl_ref.shape[0]
    x = x_ref[...].astype(BF16)
    xc, xs = [], []
    for g in range(F_GROUPS):
        t = _dot(x[:, g * F_GC:(g + 1) * F_GC], cs_ref[...])
        xc.append(t[:, :F_GC])
        xs.append(t[:, F_GC:])
    xc = jnp.concatenate(xc, axis=1).astype(BF16)
    xs = jnp.concatenate(xs, axis=1).astype(BF16)
    for r in range(0, x_ref.shape[0], length):
        rows = slice(r, r + length)
        o_ref[rows, :] = (_dot(cl_ref[...], xc[rows]) - _dot(sl_ref[...], xs[rows])).astype(BF16)


def _dft_tables(n):
    k = np.arange(n, dtype=np.int64)
    ang = 2.0 * np.pi * ((k[:, None] * k[None, :]) % n).astype(np.float64) / n
    s = n ** -0.5
    return np.cos(ang) * s, np.sin(ang) * s


def _fourier_tables(length):
    cc, sc = _dft_tables(F_GC)
    cl, sl = _dft_tables(length)
    as_bf16 = lambda a: jnp.asarray(a, F32).astype(BF16)
    return as_bf16(np.concatenate([cc, sc], axis=1)), as_bf16(cl), as_bf16(sl)


def _fourier(u_f, tables, length, first_block, n_blocks):
    cs, cl, sl = tables
    rows = max(length, FOURIER_ROWS)
    assert rows % length == 0 and (n_blocks * length) % rows == 0 and (first_block * length) % rows == 0
    first = first_block * length // rows
    return pl.pallas_call(
        _fourier_kernel,
        grid=(n_blocks * length // rows,),
        in_specs=[pl.BlockSpec((rows, F_W), lambda b: (first + b, 0)),
                  _resident((F_GC, 2 * F_GC), lambda b: (0, 0)),
                  _resident((length, length), lambda b: (0, 0)),
                  _resident((length, length), lambda b: (0, 0))],
        out_specs=pl.BlockSpec((rows, F_W), lambda b: (b, 0)),
        out_shape=jax.ShapeDtypeStruct((n_blocks * length, F_W), BF16),
        compiler_params=pltpu.CompilerParams(dimension_semantics=("arbitrary",),
                                             vmem_limit_bytes=VMEM_LIMIT),
        name="fourier_%d" % length,
    )(u_f, cs, cl, sl)


LOG2E = 1.4426950408889634


def _softmax_pv(scores, values_ext, scale):
    m = None
    for s in scores:
        sm = jnp.max(s, axis=-1, keepdims=True)
        m = sm if m is None else jnp.maximum(m, sm)
    acc = None
    for s, v in zip(scores, values_ext):
        p = jnp.exp2((s - m) * (scale * LOG2E)).astype(BF16)
        pv = _dot_nt(p, v.t) if isinstance(v, _KeyMinor) else _dot(p, v)
        acc = pv if acc is None else acc + pv
    return acc[:, :LANE] / acc[:, LANE:]


class _KeyMinor:
    def __init__(self, t):
        self.t = t


def _with_ones(v_pair):
    return jnp.concatenate([v_pair, jnp.ones_like(v_pair)], axis=1)


def _half_masks(rows):
    low = lax.broadcasted_iota(jnp.int32, (rows, LANE), 1) < LANE // 2
    return low, jnp.logical_not(low)


def _head_of_pair(x_pair, mask):
    return jnp.where(mask, x_pair, jnp.zeros_like(x_pair))


def _ctx_attn_kernel(q_ref, k_ref, v_ref, qn_ref, kn_ref, vn_ref, om_ref, on_ref):
    masks = _half_masks(SEQ)
    for r in range(0, CTX_ATTN_SEQS * SEQ, SEQ):
        rows = slice(r, r + SEQ)
        for pair in range(MLA_HEADS // 2):
            ps = slice(pair * LANE, (pair + 1) * LANE)
            v_ext = _with_ones(v_ref[rows, ps])
            o = []
            for half in range(2):
                sl = slice((2 * pair + half) * MLA_HP, (2 * pair + half + 1) * MLA_HP)
                s = _dot_nt(q_ref[rows, sl], k_ref[rows, sl])
                o.append(_softmax_pv([s], [v_ext], MLA_SCALE))
            om_ref[rows, ps] = jnp.where(masks[0], o[0], o[1]).astype(BF16)
        for pair in range(NA_HEADS // 2):
            ps = slice(pair * LANE, (pair + 1) * LANE)
            v_ext = _with_ones(vn_ref[rows, ps])
            q_pair = qn_ref[rows, ps]
            k_pair = kn_ref[rows, ps]
            o = []
            for half in range(2):
                s = _dot_nt(_head_of_pair(q_pair, masks[half]), k_pair)
                o.append(_softmax_pv([s], [v_ext], NA_SCALE))
            on_ref[rows, ps] = jnp.where(masks[0], o[0], o[1]).astype(BF16)


def _ctx_attn(q, k, v, qn, kn, vn):
    blk = lambda w: pl.BlockSpec((CTX_ATTN_SEQS * SEQ, w), lambda b: (b, 0))
    return pl.pallas_call(
        _ctx_attn_kernel,
        grid=(BATCH // CTX_ATTN_SEQS,),
        in_specs=[blk(MLA_QW), blk(MLA_QW), blk(MLA_HEADS * MLA_V), blk(NA_W), blk(NA_W), blk(NA_W)],
        out_specs=[blk(MLA_HEADS * MLA_V), blk(NA_W)],
        out_shape=[jax.ShapeDtypeStruct((P_TOK, MLA_HEADS * MLA_V), BF16),
                   jax.ShapeDtypeStruct((P_TOK, NA_W), BF16)],
        compiler_params=pltpu.CompilerParams(dimension_semantics=("arbitrary",)),
        name="ctx_attn",
    )(q, k, v, qn, kn, vn)


def _lat_mla_kernel(q_ref, k_ref, v_ref, kc_ref, vc_ref, o_ref):
    low, _ = _half_masks(MLA_QB)
    for pair in range(MLA_HEADS // 2):
        ps = slice(pair * LANE, (pair + 1) * LANE)
        v_ext = [_with_ones(v_ref[:, ps]), _with_ones(vc_ref[:, ps])]
        o = []
        for half in range(2):
            sl = slice((2 * pair + half) * MLA_HP, (2 * pair + half + 1) * MLA_HP)
            q = q_ref[:, sl]
            o.append(_softmax_pv([_dot_nt(q, k_ref[:, sl]), _dot_nt(q, kc_ref[:, sl])], v_ext, MLA_SCALE))
        o_ref[:, ps] = jnp.where(low, o[0], o[1]).astype(BF16)


def _lat_mla(q, k, v, kc, vc, l):
    nq = DEC_SEQ // MLA_QB
    q0 = P_TOK // MLA_QB
    r0 = P_TOK // DEC_SEQ
    return pl.pallas_call(
        _lat_mla_kernel,
        grid=(DEC_BATCH, nq),
        in_specs=[pl.BlockSpec((MLA_QB, MLA_QW), lambda b, j: (q0 + b * nq + j, 0)),
                  pl.BlockSpec((DEC_SEQ, MLA_QW), lambda b, j: (r0 + b, 0)),
                  pl.BlockSpec((DEC_SEQ, MLA_HEADS * MLA_V), lambda b, j: (r0 + b, 0)),
                  pl.BlockSpec((None, None, PAST_LEN, MLA_QW), lambda b, j: (b, l, 0, 0)),
                  pl.BlockSpec((None, None, PAST_LEN, MLA_HEADS * MLA_V), lambda b, j: (b, l, 0, 0))],
        out_specs=pl.BlockSpec((MLA_QB, MLA_HEADS * MLA_V), lambda b, j: (b * nq + j, 0)),
        out_shape=jax.ShapeDtypeStruct((S_TOK, MLA_HEADS * MLA_V), BF16),
        compiler_params=pltpu.CompilerParams(dimension_semantics=("arbitrary", "arbitrary"),
                                             vmem_limit_bytes=VMEM_LIMIT),
        name="lat_mla",
    )(q, k, v, kc, vc)


def _na_window_block(j):
    return jnp.clip(j - 1, 0, NA_BLOCKS - NA_WIN_ROWS // NA_QROWS)


def _lat_na_kernel(q_ref, k_ref, v_ref, kc_ref, vc_ref, bias_ref, o_ref):
    j = pl.program_id(1)
    start = pl.multiple_of(_na_window_block(j) * NA_QB, NA_QB)
    k_win = k_ref[pl.ds(start, NA_WIN), :]
    v_win = v_ref[pl.ds(start, NA_WIN), :]
    masks = _half_masks(NA_QB)
    for pair in range(NA_HEADS // 2):
        ps = slice(pair * LANE, (pair + 1) * LANE)
        k_loc = k_win[:, ps]
        k_ctx_t = kc_ref[ps, :].astype(BF16)
        v_ctx_t = vc_ref[ps, :].astype(BF16)
        v_ext = [_with_ones(v_win[:, ps]), _KeyMinor(jnp.concatenate([v_ctx_t, jnp.ones_like(v_ctx_t)], axis=0))]
        q_pair = q_ref[:, ps]
        o = []
        for half in range(2):
            q = _head_of_pair(q_pair, masks[half])
            s_loc = _dot_nt(q, k_loc) + bias_ref[2 * pair + half]
            o.append(_softmax_pv([s_loc, _dot(q, k_ctx_t)], v_ext, NA_SCALE))
        o_ref[:, ps] = jnp.where(masks[0], o[0], o[1]).astype(BF16)


NA_BLOCK_KINDS = (0, 1, NA_BLOCKS - 1)
NA_DIAG = GRID_W - 1
NA_N_DR = 2 * NA_KH - 1


def _na_bias_kernel(e_ref, o_ref):
    lane = lax.broadcasted_iota(jnp.int32, (GRID_W, LANE), 1)
    q_col = lax.broadcasted_iota(jnp.int32, (GRID_W, LANE), 0)
    k_col = lane % GRID_W
    col_start = jnp.clip(q_col - NA_KW // 2, 0, GRID_W - NA_KW)
    col_ok = (k_col >= col_start) & (k_col < col_start + NA_KW)
    low_half = lane < GRID_W
    neg = jnp.full((GRID_W, LANE), NEG_INF, F32)
    tiles = {}

    def toeplitz(dr, half):
        if (dr, half) not in tiles:
            row = jnp.broadcast_to(e_ref[dr:dr + 1, :], (GRID_W, LANE))
            shift = (LANE - NA_DIAG + half * GRID_W) % LANE
            tiles[dr, half] = pltpu.roll(row, shift, 1, stride=1, stride_axis=0)
        return tiles[dr, half]

    for kind, blk in enumerate(NA_BLOCK_KINDS):
        first_key_row = NA_QROWS * min(max(blk - 1, 0), NA_BLOCKS - NA_WIN_ROWS // NA_QROWS)
        for qr in range(NA_QROWS):
            q_row = NA_QROWS * blk + qr
            row_start = min(max(q_row - NA_KH // 2, 0), GRID_H - NA_KH)
            for pair in range(NA_WIN_ROWS // 2):
                halves = []
                for half in range(2):
                    k_row = first_key_row + 2 * pair + half
                    inside = row_start <= k_row < row_start + NA_KH
                    halves.append(toeplitz(k_row - q_row + NA_KH - 1, half) if inside else neg)
                tile = jnp.where(low_half, halves[0], halves[1])
                o_ref[kind, qr * GRID_W:(qr + 1) * GRID_W, pair * LANE:(pair + 1) * LANE] = (
                    jnp.where(col_ok, tile, NEG_INF))


def _na_bias_table(rpb):
    left = NA_DIAG - (NA_KW - 1)
    e = jnp.concatenate([jnp.broadcast_to(rpb[..., :1], rpb.shape[:-1] + (left,)), rpb,
                         jnp.broadcast_to(rpb[..., -1:], rpb.shape[:-1] + (LANE - left - rpb.shape[-1],))],
                        axis=-1) * (1.0 / NA_SCALE)
    return pl.pallas_call(
        _na_bias_kernel,
        grid=(DEPTH, NA_HEADS),
        in_specs=[pl.BlockSpec((None, None, NA_N_DR, LANE), lambda l, h: (l, h, 0, 0))],
        out_specs=pl.BlockSpec((None, len(NA_BLOCK_KINDS), None, NA_QB, NA_WIN), lambda l, h: (l, 0, h, 0, 0)),
        out_shape=jax.ShapeDtypeStruct((DEPTH, len(NA_BLOCK_KINDS), NA_HEADS, NA_QB, NA_WIN), F32),
        name="na_bias",
    )(e)


def _lat_na(q, k, v, cache_k, cache_v, bias, l):
    q0 = P_TOK // NA_QB
    r0 = P_TOK // DEC_SEQ
    kind = lambda j: (j > 0).astype(jnp.int32) + (j == NA_BLOCKS - 1).astype(jnp.int32)
    return pl.pallas_call(
        _lat_na_kernel,
        grid=(DEC_BATCH, NA_BLOCKS),
        in_specs=[pl.BlockSpec((NA_QB, NA_W), lambda b, j: (q0 + b * NA_BLOCKS + j, 0)),
                  pl.BlockSpec((DEC_SEQ, NA_W), lambda b, j: (r0 + b, 0)),
                  pl.BlockSpec((DEC_SEQ, NA_W), lambda b, j: (r0 + b, 0)),
                  pl.BlockSpec((None, None, PAST_LEN, NA_W), lambda b, j: (b, l, 0, 0)),
                  pl.BlockSpec((None, None, PAST_LEN, NA_W), lambda b, j: (b, l, 0, 0)),
                  pl.BlockSpec((None, None, NA_HEADS, NA_QB, NA_WIN), lambda b, j: (l, kind(j), 0, 0, 0))],
        out_specs=pl.BlockSpec((NA_QB, NA_W), lambda b, j: (b * NA_BLOCKS + j, 0)),
        out_shape=jax.ShapeDtypeStruct((S_TOK, NA_W), BF16),
        compiler_params=pltpu.CompilerParams(dimension_semantics=("arbitrary", "arbitrary"),
                                             vmem_limit_bytes=VMEM_LIMIT),
        name="lat_na",
    )(q, k, v, cache_k, cache_v, bias)


def _merge_kernel(x_ref, h_ref, mod_ref, yfc_ref, ymc_ref, ync_ref, yfl_ref, yml_ref, ynl_ref,
                  wg_ref, bg_ref, wf_ref, wm_ref, wn_ref, wo_ref, g_ref, b_ref, o_ref):
    i = pl.program_id(0)
    _, _, gate = _mod_rows(mod_ref, i)
    x = x_ref[...]
    h = h_ref[...]
    is_ctx = i < P_TILES
    mix = None
    for n, (yc_ref, yl_ref, w_ref) in enumerate(((yfc_ref, yfl_ref, wf_ref), (ymc_ref, yml_ref, wm_ref),
                                                 (ync_ref, ynl_ref, wn_ref))):
        sl = slice(n * D_MODEL, (n + 1) * D_MODEL)
        g = jax.nn.sigmoid(_dot(h, wg_ref[:, sl]) + bg_ref[:, sl])
        y = jnp.where(is_ctx, yc_ref[...], yl_ref[...])
        t = g * _dot(y, w_ref[...])
        mix = t if mix is None else mix + t
    z = ALPHA * x + gate * _dot(mix.astype(BF16), wo_ref[...])
    o_ref[...] = _ln(z, 1e-5) * g_ref[...] + b_ref[...]


def _merge(x, h, mod, y_ctx, y_lat, w_gate, b_gate, w_f, w_m, w_n, w_out, ln_g, ln_b, l):
    tile = lambda w: pl.BlockSpec((TM, w), lambda i: (i, 0))
    ctx_tile = lambda w: pl.BlockSpec((TM, w), lambda i: (jnp.minimum(i, P_TILES - 1), 0))
    lat_tile = lambda w: pl.BlockSpec((TM, w), lambda i: (jnp.maximum(i - P_TILES, 0), 0))
    widths = (F_W, MLA_HEADS * MLA_V, NA_W)
    return pl.pallas_call(
        _merge_kernel,
        grid=(N_TILES,),
        in_specs=[tile(D_MODEL), tile(D_MODEL),
                  _resident((None, 8, 3 * D_MODEL), lambda i: (l, 0, 1))]
                 + [ctx_tile(w) for w in widths] + [lat_tile(w) for w in widths] + [
                  _resident((None, D_MODEL, 3 * D_MODEL), lambda i: (l, 0, 0)),
                  _resident((None, 1, 3 * D_MODEL), lambda i: (l, 0, 0)),
                  _resident((None, F_W, D_MODEL), lambda i: (l, 0, 0)),
                  _resident((None, MLA_HEADS * MLA_V, D_MODEL), lambda i: (l, 0, 0)),
                  _resident((None, NA_W, D_MODEL), lambda i: (l, 0, 0)),
                  _resident((None, D_MODEL, D_MODEL), lambda i: (l, 0, 0)),
                  _resident((None, 1, D_MODEL), lambda i: (3 * l + 1, 0, 0)),
                  _resident((None, 1, D_MODEL), lambda i: (3 * l + 1, 0, 0))],
        out_specs=tile(D_MODEL),
        out_shape=jax.ShapeDtypeStruct((TOKENS, D_MODEL), F32),
        compiler_params=pltpu.CompilerParams(dimension_semantics=("arbitrary",),
                                             vmem_limit_bytes=VMEM_LIMIT),
        name="merge",
    )(x, h, mod, *y_ctx, *y_lat, w_gate, b_gate, w_f, w_m, w_n, w_out, ln_g, ln_b)


def _pad_heads(w, n_heads, width):
    lead = w.shape[:-1]
    w = w.reshape(lead + (n_heads, width))
    w = jnp.pad(w, [(0, 0)] * len(lead) + [(0, 0), (0, MLA_HP - width)])
    return w.reshape(lead + (n_heads * MLA_HP,))


def _rope_tables():
    t = jnp.arange(DEC_SEQ, dtype=jnp.int32)
    pos = jnp.stack([t // GRID_W, t % GRID_W], axis=-1).astype(F32)
    half = AXIS_DIM // 2
    inv_freq = ROPE_BASE ** (-jnp.arange(half, dtype=F32) / half)
    ang = pos[:, :, None] * inv_freq
    ang = jnp.concatenate([ang, ang], axis=-1).reshape(DEC_SEQ, MLA_ROPE)
    pad = lambda a, fill: jnp.pad(a, ((0, 0), (ROPE_LANE0, LANE - ROPE_LANE0 - MLA_ROPE)), constant_values=fill)
    cos = jnp.concatenate([jnp.ones((TM, LANE), F32), pad(jnp.cos(ang), 1.0)], axis=0)
    sin = jnp.concatenate([jnp.zeros((TM, LANE), F32), pad(jnp.sin(ang), 0.0)], axis=0)
    return cos, sin


def kernel(x_prompt, x_sample, cache_mla_ckv, cache_mla_krope, cache_na_k, cache_na_v, c, c_ctx, w_ada, b_ada, ffn1_w1, ffn1_w3, ffn1_w2, ffn2_w1, ffn2_w3, ffn2_w2, w_in, mla_q_norm, mla_w_uq, mla_kv_norm, mla_w_ukv, na_rpb, w_branch_f, w_branch_m, w_branch_n, w_gate, b_gate, w_out, ln_g, ln_b):
    bf = lambda w: w.astype(BF16)
    f1 = (ffn1_w1, ffn1_w3, ffn1_w2)
    f2 = (ffn2_w1, ffn2_w3, ffn2_w2)
    w_in_p = bf(jnp.swapaxes(w_in, 1, 2))
    w_uq_p = bf(_pad_heads(mla_w_uq, MLA_HEADS, MLA_NOPE + MLA_ROPE))
    ukv = mla_w_ukv.reshape(DEPTH, MLA_KV_LORA, MLA_HEADS, MLA_NOPE + MLA_V)
    w_uk_p = bf(_pad_heads(ukv[..., :MLA_NOPE].reshape(DEPTH, MLA_KV_LORA, MLA_HEADS * MLA_NOPE), MLA_HEADS, MLA_NOPE))
    w_uv = bf(ukv[..., MLA_NOPE:].reshape(DEPTH, MLA_KV_LORA, MLA_HEADS * MLA_V))
    w_gate_b, w_f, w_m, w_n, w_out_b = bf(w_gate), bf(w_branch_f), bf(w_branch_m), bf(w_branch_n), bf(w_out)
    q_norm = mla_q_norm.reshape(DEPTH, 1, MLA_Q_LORA)
    kv_norm = mla_kv_norm.reshape(DEPTH, 1, MLA_KV_LORA)
    b_gate3 = b_gate.reshape(DEPTH, 1, 3 * D_MODEL)
    g3 = ln_g.reshape(DEPTH * 3, 1, D_MODEL)
    b3 = ln_b.reshape(DEPTH * 3, 1, D_MODEL)
    cos_t, sin_t = _rope_tables()
    dft_ctx = _fourier_tables(SEQ)
    dft_lat = _fourier_tables(DEC_SEQ)
    kr_pad = jnp.pad(cache_mla_krope, ((0, 0), (0, 0), (0, 0), (ROPE_LANE0, LANE - ROPE_LANE0 - MLA_ROPE)))
    ch_major = lambda a: jnp.transpose(a, (0, 1, 3, 4, 2)).reshape(DEC_BATCH, DEPTH, NA_W, PAST_LEN)
    cache_k = ch_major(cache_na_k)
    cache_v = ch_major(cache_na_v)

    cvec = jnp.concatenate([c_ctx[None], c, jnp.zeros((8 - 1 - DEC_BATCH, D_MODEL), F32)], axis=0)
    mod = _adaln(cvec, w_ada, b_ada)
    kc, vc = _ctx_kv(cache_mla_ckv, kr_pad, w_uk_p, w_uv)
    na_bias = _na_bias_table(na_rpb)

    x = (x_prompt.reshape(P_TOK, D_MODEL), x_sample.reshape(S_TOK, D_MODEL))
    caches = [jnp.zeros(s, F32) for s in _cache_shapes()]
    for l in range(DEPTH):
        x, h = _ffn(x, mod, *f1, g3, b3, l, 0, emit_h=True)
        u_f, q, k, v, q_n, k_nb, v_nb, *caches = _mixer_in(
            h, cos_t, sin_t, w_in_p, q_norm, w_uq_p, kv_norm, w_uk_p, w_uv, caches, l)
        yf_c = _fourier(u_f, dft_ctx, SEQ, 0, BATCH)
        yf_l = _fourier(u_f, dft_lat, DEC_SEQ, P_TOK // DEC_SEQ, DEC_BATCH)
        ym_c, yn_c = _ctx_attn(q, k, v, q_n, k_nb, v_nb)
        ym_l = _lat_mla(q, k, v, kc, vc, l)
        yn_l = _lat_na(q_n, k_nb, v_nb, cache_k, cache_v, na_bias, l)
        x = _merge(x, h, mod, (yf_c, ym_c, yn_c), (yf_l, ym_l, yn_l), w_gate_b, b_gate3, w_f, w_m, w_n, w_out_b,
                   g3, b3, l)
        x = _ffn(x, mod, *f2, g3, b3, l, 2, split_out=(l == DEPTH - 1))
        x = tuple(x) if len(x) > 1 else x[0]
    ckv, kr_t, nak_t, nav_t = caches
    per_head = lambda a: jnp.transpose(a.reshape(BATCH, DEPTH, NA_HEADS, NA_HEAD_DIM, SEQ), (0, 1, 4, 2, 3))
    return (x[0].reshape(BATCH, SEQ, D_MODEL), x[1].reshape(DEC_BATCH, DEC_SEQ, D_MODEL),
            ckv, jnp.transpose(kr_t, (0, 1, 3, 2)), per_head(nak_t), per_head(nav_t))
```

```python
import functools

import numpy as np
import jax
import jax.numpy as jnp
from jax import lax
from jax.experimental import pallas as pl
from jax.experimental.pallas import tpu as pltpu

F32 = jnp.float32
BF16 = jnp.bfloat16

D_MODEL = 1024
BATCH = 32
SEQ = 256
DEPTH = 4
DEC_BATCH = 2
DEC_SEQ = 2048
PAST_LEN = 512
GRID_W = 64
GRID_H = DEC_SEQ // GRID_W
D_FF = 2816
F_GROUPS = 4
F_GC = 128
F_W = F_GROUPS * F_GC
MLA_HEADS = 8
MLA_Q_LORA = 384
MLA_KV_LORA = 256
MLA_NOPE = 64
MLA_ROPE = 32
MLA_V = 64
NA_HEADS = 8
NA_HEAD_DIM = 64
NA_KH = 8
NA_KW = 16
NA_W = NA_HEADS * NA_HEAD_DIM
ROPE_BASE = 10000.0
AXIS_DIM = MLA_ROPE // 2
ALPHA = (2.0 * DEPTH) ** 0.25
MLA_SCALE = (MLA_NOPE + MLA_ROPE) ** -0.5
NA_SCALE = NA_HEAD_DIM ** -0.5
NEG_INF = -1e30

LANE = 128
MLA_HP = LANE
MLA_QW = MLA_HEADS * MLA_HP
P_TOK = BATCH * SEQ
S_TOK = DEC_BATCH * DEC_SEQ
TOKENS = P_TOK + S_TOK
TM = 512
SIDE_ROWS = 64
N_TILES = TOKENS // TM
P_TILES = P_TOK // TM
S_TILES_PER_REQ = DEC_SEQ // TM
MXU_TILE = 256
FF_CHUNKS = (0, 4 * MXU_TILE, 8 * MXU_TILE, D_FF)
U_Q0 = F_W
U_KV0 = U_Q0 + MLA_Q_LORA
U_KR0 = U_KV0 + MLA_KV_LORA
U_NA0 = U_KR0 + MLA_ROPE
U_W = U_NA0 + 3 * NA_W
ROPE_LANE0 = MLA_NOPE
NA_QROWS = 4
NA_QB = NA_QROWS * GRID_W
NA_WIN_ROWS = NA_QROWS + NA_KH
NA_WIN = NA_WIN_ROWS * GRID_W
NA_BLOCKS = GRID_H // NA_QROWS
MLA_QB = 1024
CTX_ATTN_SEQS = 4
FOURIER_ROWS = 1024
VMEM_LIMIT = 56 * 1024 * 1024
FFN_VMEM_LIMIT = 60 * 1024 * 1024
STAGE_SLOTS = 4
UP_STAGE_ROWS = 64
DOWN_STAGE_ROWS = D_FF // 16


def _group(i):
    return jnp.where(i < P_TILES, 0, 1 + (i - P_TILES) // S_TILES_PER_REQ)


def _rope_block(i):
    return jnp.where(i < P_TILES, 0, 1 + (i - P_TILES) % S_TILES_PER_REQ)


def _ln(x, eps):
    mu = jnp.mean(x, axis=-1, keepdims=True)
    xc = x - mu
    var = jnp.mean(xc * xc, axis=-1, keepdims=True)
    return xc * lax.rsqrt(var + eps)


def _rms(x, g):
    return x * lax.rsqrt(jnp.mean(x * x, axis=-1, keepdims=True) + 1e-6) * g


def _dot(a, b):
    return jnp.dot(a, b, preferred_element_type=F32)


def _dot_nt(a, b):
    return lax.dot_general(a, b, (((1,), (1,)), ((), ())), preferred_element_type=F32)


def _mod_rows(mod_ref, i):
    m = mod_ref[pl.ds(_group(i), 1), :]
    return m[:, :D_MODEL], m[:, D_MODEL:2 * D_MODEL], m[:, 2 * D_MODEL:]


def _rope(x, cos, sin):
    lane = lax.broadcasted_iota(jnp.int32, x.shape, 1)
    first_half = (lane % AXIS_DIM) < (AXIS_DIM // 2)
    rot = jnp.where(first_half, -pltpu.roll(x, LANE - AXIS_DIM // 2, 1), pltpu.roll(x, AXIS_DIM // 2, 1))
    return x * cos + rot * sin


def _adaln_kernel(c_ref, w_ref, b_ref, o_ref):
    c = c_ref[...]
    s = (c * jax.nn.sigmoid(c)).astype(BF16)
    o_ref[...] = _dot(s, w_ref[...].astype(BF16)) + b_ref[...]


def _adaln(cvec, w_ada, b_ada):
    n_col = 9 * D_MODEL // D_MODEL
    return pl.pallas_call(
        _adaln_kernel,
        grid=(DEPTH, n_col),
        in_specs=[pl.BlockSpec((8, D_MODEL), lambda l, j: (0, 0)),
                  pl.BlockSpec((None, D_MODEL, D_MODEL), lambda l, j: (l, 0, j)),
                  pl.BlockSpec((None, 1, D_MODEL), lambda l, j: (l, 0, j))],
        out_specs=pl.BlockSpec((None, 8, D_MODEL), lambda l, j: (l, 0, j)),
        out_shape=jax.ShapeDtypeStruct((DEPTH, 8, 9 * D_MODEL), F32),
        name="adaln",
    )(cvec, w_ada, b_ada.reshape(DEPTH, 1, 9 * D_MODEL))


def _ffn_kernel(*refs, layer, split_in, split_out, emit_h, staged):
    refs = list(refs)
    x_refs = [refs.pop(0) for _ in range(4 if split_in else 2)]
    mod_ref = refs.pop(0)
    mod_next_ref = refs.pop(0) if emit_h else None
    w_in_refs = [refs.pop(0) for _ in range(3)]
    g_ref, b_ref = refs.pop(0), refs.pop(0)
    o_refs = [refs.pop(0) for _ in range(2 if split_out else 1)]
    hn_ref = refs.pop(0) if emit_h else None
    if staged:
        w1_hbm, w3_hbm, w2_hbm = w_in_refs
        h_scr, y_scr, w1_ref, w3_ref, w2_ref, stage_up, stage_down, sem = refs
    else:
        w1_ref, w3_ref, w2_ref = w_in_refs
        h_scr, y_scr = refs
    s = pl.program_id(0)
    cur = s % 2
    nxt = 1 - cur
    pieces = [pl.ds(r, SIDE_ROWS) for r in range(0, TM, SIDE_ROWS)]

    def x_tile(which, tile):
        if not split_in:
            return lambda rows: x_refs[which][rows, :]
        ctx_ref, lat_ref = x_refs[2 * which:2 * which + 2]
        return lambda rows: jnp.where(tile < P_TILES, ctx_ref[rows, :], lat_ref[rows, :])

    x_prev = x_tile(0, s - 1)
    x_next = x_tile(1, s + 1)

    def modulate(x_rows, tile, slot):
        shift, scale, _ = _mod_rows(mod_ref, tile)

        def piece(rows):
            h_scr[slot, rows, :] = (_ln(x_rows(rows), 1e-6) * (1.0 + scale) + shift).astype(BF16)
        return [functools.partial(piece, rows) for rows in pieces]

    def finish():
        _, _, gate = _mod_rows(mod_ref, s - 1)
        if emit_h:
            shift_n, scale_n, _ = _mod_rows(mod_next_ref, s - 1)

        def piece(rows):
            z = ALPHA * x_prev(rows) + (0.5 * gate) * y_scr[nxt, rows, :]
            out = _ln(z, 1e-5) * g_ref[...] + b_ref[...]
            if split_out:
                to_ctx = jnp.broadcast_to(s - 1 < P_TILES, out.shape)
                pltpu.store(o_refs[0].at[rows, :], out, mask=to_ctx)
                pltpu.store(o_refs[1].at[rows, :], out, mask=jnp.logical_not(to_ctx))
            else:
                o_refs[0][rows, :] = out
            if emit_h:
                hn_ref[rows, :] = (_ln(out, 1e-6) * (1.0 + scale_n) + shift_n).astype(BF16)
            y_scr[nxt, rows, :] = out
        return [functools.partial(piece, rows) for rows in pieces]

    def load_weights():
        chunks = []
        for src, dst in ((w1_hbm, w1_ref), (w3_hbm, w3_ref)):
            for r in range(0, D_MODEL, UP_STAGE_ROWS):
                rows = pl.ds(r, UP_STAGE_ROWS)
                chunks.append((src.at[layer, rows, :], stage_up, dst.at[rows, :]))
        for r in range(0, D_FF, DOWN_STAGE_ROWS):
            rows = pl.ds(r, DOWN_STAGE_ROWS)
            chunks.append((w2_hbm.at[layer, rows, :], stage_down, w2_ref.at[rows, :]))

        def copy(k):
            src, stage, _ = chunks[k]
            return pltpu.make_async_copy(src, stage.at[k % STAGE_SLOTS], sem.at[k % STAGE_SLOTS])

        for k in range(STAGE_SLOTS - 1):
            copy(k).start()
        for k, (_, stage, dst) in enumerate(chunks):
            if k + STAGE_SLOTS - 1 < len(chunks):
                copy(k + STAGE_SLOTS - 1).start()
            copy(k).wait()
            dst[...] = stage[k % STAGE_SLOTS].astype(BF16)

    def matmuls(side_work):
        side_work = list(side_work)
        chunks = [slice(c0, c1) for c0, c1 in zip(FF_CHUNKS[:-1], FF_CHUNKS[1:])]
        per_dot = -(-len(side_work) // (3 * len(chunks)))

        def dot_with_side(lhs, rhs):
            for _ in range(min(per_dot, len(side_work))):
                side_work.pop(0)()
            return _dot(lhs() if callable(lhs) else lhs, rhs)

        h = lambda: h_scr[cur]

        def gate_up(sl):
            a = dot_with_side(h, w1_ref[:, sl])
            return (a * jax.nn.sigmoid(a) * dot_with_side(h, w3_ref[:, sl])).astype(BF16)

        t_next = gate_up(chunks[0])
        for n, sl in enumerate(chunks):
            t = t_next
            if n + 1 < len(chunks):
                t_next = gate_up(chunks[n + 1])
            yc = dot_with_side(t, w2_ref[sl, :])
            if n == 0:
                y_scr[cur] = yc
            else:
                y_scr[cur] += yc
        for work in side_work:
            work()

    @pl.when(s == 0)
    def _():
        if staged:
            load_weights()
        for work in modulate(x_tile(0, s), s, cur):
            work()
        matmuls(modulate(x_next, s + 1, nxt))

    @pl.when(jnp.logical_and(s > 0, s < N_TILES))
    def _():
        matmuls(finish() + modulate(x_next, s + 1, nxt))

    @pl.when(s == N_TILES)
    def _():
        for work in finish():
            work()


def _resident(shape, index_map):
    return pl.BlockSpec(shape, index_map, pipeline_mode=pl.Buffered(1))


def _ffn(x, mod, w1, w3, w2, ln_g, ln_b, l, sub, split_out=False, emit_h=False):
    split_in = isinstance(x, tuple)
    tile = lambda lo, hi, off: pl.BlockSpec((TM, D_MODEL), lambda i: (jnp.clip(i + off, lo, hi) - lo, 0))
    ctx_lat = lambda off: [tile(0, P_TILES - 1, off), tile(P_TILES, N_TILES - 1, off)]
    if split_in:
        x_args = [x[0], x[1]] * 2
        x_specs = ctx_lat(-1) + ctx_lat(1)
    else:
        x_args = [x, x]
        x_specs = [tile(0, N_TILES - 1, -1), tile(0, N_TILES - 1, 1)]
    if split_out:
        out_specs = ctx_lat(-1)
        out_shape = [jax.ShapeDtypeStruct((P_TOK, D_MODEL), F32), jax.ShapeDtypeStruct((S_TOK, D_MODEL), F32)]
    else:
        out_specs = [tile(0, N_TILES - 1, -1)]
        out_shape = [jax.ShapeDtypeStruct((TOKENS, D_MODEL), F32)]
    mod_specs = [_resident((None, 8, 3 * D_MODEL), lambda i: (l, 0, sub))]
    mod_args = [mod]
    if emit_h:
        mod_specs.append(_resident((None, 8, 3 * D_MODEL), lambda i: (l, 0, sub + 1)))
        mod_args.append(mod)
        out_specs = out_specs + [tile(0, N_TILES - 1, -1)]
        out_shape = out_shape + [jax.ShapeDtypeStruct((TOKENS, D_MODEL), BF16)]
    staged = w1.dtype == F32
    scratch = [pltpu.VMEM((2, TM, D_MODEL), BF16), pltpu.VMEM((2, TM, D_MODEL), F32)]
    if staged:
        w_specs = [pl.BlockSpec(memory_space=pl.ANY)] * 3
        scratch += [pltpu.VMEM((D_MODEL, D_FF), BF16), pltpu.VMEM((D_MODEL, D_FF), BF16),
                    pltpu.VMEM((D_FF, D_MODEL), BF16),
                    pltpu.VMEM((STAGE_SLOTS, UP_STAGE_ROWS, D_FF), F32),
                    pltpu.VMEM((STAGE_SLOTS, DOWN_STAGE_ROWS, D_MODEL), F32),
                    pltpu.SemaphoreType.DMA((STAGE_SLOTS,))]
    else:
        w_specs = [_resident(w.shape, lambda i: (0, 0)) for w in (w1, w3, w2)]
    return pl.pallas_call(
        functools.partial(_ffn_kernel, layer=l, split_in=split_in, split_out=split_out, emit_h=emit_h,
                          staged=staged),
        grid=(N_TILES + 1,),
        in_specs=x_specs + mod_specs + w_specs + [
                  _resident((None, 1, D_MODEL), lambda i: (3 * l + sub, 0, 0)),
                  _resident((None, 1, D_MODEL), lambda i: (3 * l + sub, 0, 0))],
        out_specs=out_specs,
        out_shape=out_shape,
        scratch_shapes=scratch,
        compiler_params=pltpu.CompilerParams(dimension_semantics=("arbitrary",),
                                             vmem_limit_bytes=FFN_VMEM_LIMIT),
        name="ffn",
    )(*x_args, *mod_args, w1, w3, w2, ln_g, ln_b)


CAST_BLOCKS = N_TILES // 3


def _cast_ahead_specs(layer):
    up_rows, down_rows = D_MODEL // CAST_BLOCKS, D_FF // CAST_BLOCKS
    blk = lambda n: (lambda i: jnp.clip(i - n * CAST_BLOCKS, 0, CAST_BLOCKS - 1))
    shapes = [(up_rows, D_FF, D_MODEL), (up_rows, D_FF, D_MODEL), (down_rows, D_MODEL, D_FF)]
    in_specs = [pl.BlockSpec((None, r, c), lambda i, b=blk(n): (layer, b(i), 0)) for n, (r, c, _) in enumerate(shapes)]
    out_specs = [pl.BlockSpec((r, c), lambda i, b=blk(n): (b(i), 0)) for n, (r, c, _) in enumerate(shapes)]
    out_shape = [jax.ShapeDtypeStruct((total, c), BF16) for r, c, total in shapes]
    return in_specs, out_specs, out_shape


def _cast_ahead(src_refs, dst_refs):
    for n, (src, dst) in enumerate(zip(src_refs, dst_refs)):
        @pl.when(pl.program_id(0) // CAST_BLOCKS == n)
        def _(src=src, dst=dst):
            dst[...] = src[...].astype(BF16)


def _mixer_in_kernel(*refs, cast_ahead):
    refs = list(refs)
    h_ref, cos_ref, sin_ref, w_in_ref, qn_ref, w_uq_ref, kvn_ref, w_uk_ref, w_uv_ref = refs[:9]
    n_cast = 3 if cast_ahead else 0
    outs = refs[13 + n_cast:]
    uf_ref, q_ref, k_ref, v_ref, qna_ref, knab_ref, vnab_ref, ckv_ref, kr_ref, kna_ref, vna_ref = outs[:11]
    _cast_ahead(refs[13:13 + n_cast], outs[11:])
    i = pl.program_id(0)
    h = h_ref[...]
    cos = cos_ref[...]
    sin = sin_ref[...]

    proj = lambda r0, r1: _dot_nt(h, w_in_ref[r0:r1, :])
    uf_ref[...] = proj(0, U_Q0)

    u_q = proj(U_Q0, U_KV0)
    q = _dot(_rms(u_q, qn_ref[...]).astype(BF16), w_uq_ref[...])
    for hd in range(MLA_HEADS):
        sl = slice(hd * MLA_HP, (hd + 1) * MLA_HP)
        q_ref[:, sl] = _rope(q[:, sl], cos, sin).astype(BF16)

    c_kv = _rms(proj(U_KV0, U_KR0), kvn_ref[...])
    w_kr = jnp.concatenate([jnp.zeros((ROPE_LANE0, D_MODEL), BF16), w_in_ref[U_KR0:U_NA0, :],
                            jnp.zeros((LANE - ROPE_LANE0 - MLA_ROPE, D_MODEL), BF16)], axis=0)
    kr = _dot_nt(h, w_kr)
    k_na = proj(U_NA0 + NA_W, U_NA0 + 2 * NA_W)
    v_na = proj(U_NA0 + 2 * NA_W, U_W)

    @pl.when(i < P_TILES)
    def _():
        kr_t = kr.T[ROPE_LANE0:ROPE_LANE0 + MLA_ROPE, :]
        kna_t = k_na.T
        vna_t = v_na.T
        for b in range(TM // SEQ):
            rows = slice(b * SEQ, (b + 1) * SEQ)
            ckv_ref[b] = c_kv[rows, :]
            kr_ref[b] = kr_t[:, rows]
            kna_ref[b] = kna_t[:, rows]
            vna_ref[b] = vna_t[:, rows]

    c_kv = c_kv.astype(BF16)
    kr = _rope(kr, cos, sin)
    k = _dot(c_kv, w_uk_ref[...])
    for hd in range(MLA_HEADS):
        sl = slice(hd * MLA_HP, (hd + 1) * MLA_HP)
        k_ref[:, sl] = (k[:, sl] + kr).astype(BF16)
    v_ref[...] = _dot(c_kv, w_uv_ref[...]).astype(BF16)

    qna_ref[...] = proj(U_NA0, U_NA0 + NA_W).astype(BF16)
    knab_ref[...] = k_na.astype(BF16)
    vnab_ref[...] = v_na.astype(BF16)


def _cache_shapes():
    return [(BATCH, DEPTH, SEQ, MLA_KV_LORA), (BATCH, DEPTH, MLA_ROPE, SEQ),
            (BATCH, DEPTH, NA_W, SEQ), (BATCH, DEPTH, NA_W, SEQ)]


def _mixer_in(h, cos_t, sin_t, w_in, q_norm, w_uq, kv_norm, w_uk, w_uv, caches, l, next_ffn=None):
    tile = lambda w: pl.BlockSpec((TM, w), lambda i: (i, 0))
    acts = [(F_W, F32), (MLA_QW, BF16), (MLA_QW, BF16), (MLA_HEADS * MLA_V, BF16), (NA_W, BF16), (NA_W, BF16),
            (NA_W, BF16)]
    cache_spec = lambda s: pl.BlockSpec((TM // SEQ, None) + s[2:],
                                        lambda i: (jnp.minimum(i, P_TILES - 1), l, 0, 0))
    n_in = 9
    cast_in, cast_out, cast_shape = _cast_ahead_specs(next_ffn[3]) if next_ffn else ([], [], [])
    cast_args = list(next_ffn[:3]) if next_ffn else []
    return pl.pallas_call(
        functools.partial(_mixer_in_kernel, cast_ahead=bool(next_ffn)),
        grid=(N_TILES,),
        in_specs=[tile(D_MODEL),
                  pl.BlockSpec((TM, LANE), lambda i: (_rope_block(i), 0)),
                  pl.BlockSpec((TM, LANE), lambda i: (_rope_block(i), 0)),
                  _resident((None, U_W, D_MODEL), lambda i: (l, 0, 0)),
                  _resident((None, 1, MLA_Q_LORA), lambda i: (l, 0, 0)),
                  _resident((None, MLA_Q_LORA, MLA_QW), lambda i: (l, 0, 0)),
                  _resident((None, 1, MLA_KV_LORA), lambda i: (l, 0, 0)),
                  _resident((None, MLA_KV_LORA, MLA_QW), lambda i: (l, 0, 0)),
                  _resident((None, MLA_KV_LORA, MLA_HEADS * MLA_V), lambda i: (l, 0, 0))]
                 + [pl.BlockSpec(memory_space=pl.ANY)] * len(caches) + cast_in,
        out_specs=[tile(w) for w, _ in acts] + [cache_spec(s) for s in _cache_shapes()] + cast_out,
        out_shape=[jax.ShapeDtypeStruct((TOKENS, w), dt) for w, dt in acts]
                  + [jax.ShapeDtypeStruct(s, F32) for s in _cache_shapes()] + cast_shape,
        input_output_aliases={n_in + n: len(acts) + n for n in range(len(caches))},
        compiler_params=pltpu.CompilerParams(dimension_semantics=("arbitrary",),
                                             vmem_limit_bytes=VMEM_LIMIT),
        name="mixer_in",
    )(h, cos_t, sin_t, w_in, q_norm, w_uq, kv_norm, w_uk, w_uv, *caches, *cast_args)


def _ctx_kv_kernel(ckv_ref, kr_ref, w_uk_ref, w_uv_ref, k_ref, v_ref):
    c = ckv_ref[...].astype(BF16)
    k = _dot(c, w_uk_ref[...])
    kr = kr_ref[...]
    for hd in range(MLA_HEADS):
        sl = slice(hd * MLA_HP, (hd + 1) * MLA_HP)
        k_ref[:, sl] = (k[:, sl] + kr).astype(BF16)
    v_ref[...] = _dot(c, w_uv_ref[...]).astype(BF16)


def _ctx_kv(cache_ckv, cache_kr_pad, w_uk, w_uv):
    return pl.pallas_call(
        _ctx_kv_kernel,
        grid=(DEC_BATCH, DEPTH),
        in_specs=[pl.BlockSpec((None, None, PAST_LEN, MLA_KV_LORA), lambda b, l: (b, l, 0, 0)),
                  pl.BlockSpec((None, None, PAST_LEN, LANE), lambda b, l: (b, l, 0, 0)),
                  pl.BlockSpec((None, MLA_KV_LORA, MLA_QW), lambda b, l: (l, 0, 0)),
                  pl.BlockSpec((None, MLA_KV_LORA, MLA_HEADS * MLA_V), lambda b, l: (l, 0, 0))],
        out_specs=[pl.BlockSpec((None, None, PAST_LEN, MLA_QW), lambda b, l: (b, l, 0, 0)),
                   pl.BlockSpec((None, None, PAST_LEN, MLA_HEADS * MLA_V), lambda b, l: (b, l, 0, 0))],
        out_shape=[jax.ShapeDtypeStruct((DEC_BATCH, DEPTH, PAST_LEN, MLA_QW), BF16),
                   jax.ShapeDtypeStruct((DEC_BATCH, DEPTH, PAST_LEN, MLA_HEADS * MLA_V), BF16)],
        name="ctx_kv",
    )(cache_ckv, cache_kr_pad, w_uk, w_uv)


def _fourier_kernel(x_ref, cs_ref, cl_ref, sl_ref, o_ref):
    length = cl_ref.shape[0]
    x = x_ref[...].astype(BF16)
    xc, xs = [], []
    for g in range(F_GROUPS):
        t = _dot(x[:, g * F_GC:(g + 1) * F_GC], cs_ref[...])
        xc.append(t[:, :F_GC])
        xs.append(t[:, F_GC:])
    xc = jnp.concatenate(xc, axis=1).astype(BF16)
    xs = jnp.concatenate(xs, axis=1).astype(BF16)
    for r in range(0, x_ref.shape[0], length):
        rows = slice(r, r + length)
        o_ref[rows, :] = (_dot(cl_ref[...], xc[rows]) - _dot(sl_ref[...], xs[rows])).astype(BF16)


def _dft_tables(n):
    k = np.arange(n, dtype=np.int64)
    ang = 2.0 * np.pi * ((k[:, None] * k[None, :]) % n).astype(np.float64) / n
    s = n ** -0.5
    return np.cos(ang) * s, np.sin(ang) * s


def _fourier_tables(length):
    cc, sc = _dft_tables(F_GC)
    cl, sl = _dft_tables(length)
    as_bf16 = lambda a: jnp.asarray(a, F32).astype(BF16)
    return as_bf16(np.concatenate([cc, sc], axis=1)), as_bf16(cl), as_bf16(sl)


def _fourier(u_f, tables, length, first_block, n_blocks):
    cs, cl, sl = tables
    rows = max(length, FOURIER_ROWS)
    assert rows % length == 0 and (n_blocks * length) % rows == 0 and (first_block * length) % rows == 0
    first = first_block * length // rows
    return pl.pallas_call(
        _fourier_kernel,
        grid=(n_blocks * length // rows,),
        in_specs=[pl.BlockSpec((rows, F_W), lambda b: (first + b, 0)),
                  _resident((F_GC, 2 * F_GC), lambda b: (0, 0)),
                  _resident((length, length), lambda b: (0, 0)),
                  _resident((length, length), lambda b: (0, 0))],
        out_specs=pl.BlockSpec((rows, F_W), lambda b: (b, 0)),
        out_shape=jax.ShapeDtypeStruct((n_blocks * length, F_W), BF16),
        compiler_params=pltpu.CompilerParams(dimension_semantics=("arbitrary",),
                                             vmem_limit_bytes=VMEM_LIMIT),
        name="fourier_%d" % length,
    )(u_f, cs, cl, sl)


LOG2E = 1.4426950408889634


def _softmax_pv(scores, values_ext, scale):
    m = None
    for s in scores:
        sm = jnp.max(s, axis=-1, keepdims=True)
        m = sm if m is None else jnp.maximum(m, sm)
    acc = None
    for s, v in zip(scores, values_ext):
        p = jnp.exp2((s - m) * (scale * LOG2E)).astype(BF16)
        pv = _dot_nt(p, v.t) if isinstance(v, _KeyMinor) else _dot(p, v)
        acc = pv if acc is None else acc + pv
    return acc[:, :LANE] / acc[:, LANE:]


class _KeyMinor:
    def __init__(self, t):
        self.t = t


def _with_ones(v_pair):
    return jnp.concatenate([v_pair, jnp.ones_like(v_pair)], axis=1)


def _half_masks(rows):
    low = lax.broadcasted_iota(jnp.int32, (rows, LANE), 1) < LANE // 2
    return low, jnp.logical_not(low)


def _head_of_pair(x_pair, mask):
    return jnp.where(mask, x_pair, jnp.zeros_like(x_pair))


def _ctx_attn_kernel(q_ref, k_ref, v_ref, qn_ref, kn_ref, vn_ref, om_ref, on_ref):
    masks = _half_masks(SEQ)
    for r in range(0, CTX_ATTN_SEQS * SEQ, SEQ):
        rows = slice(r, r + SEQ)
        for pair in range(MLA_HEADS // 2):
            ps = slice(pair * LANE, (pair + 1) * LANE)
            v_ext = _with_ones(v_ref[rows, ps])
            o = []
            for half in range(2):
                sl = slice((2 * pair + half) * MLA_HP, (2 * pair + half + 1) * MLA_HP)
                s = _dot_nt(q_ref[rows, sl], k_ref[rows, sl])
                o.append(_softmax_pv([s], [v_ext], MLA_SCALE))
            om_ref[rows, ps] = jnp.where(masks[0], o[0], o[1]).astype(BF16)
        for pair in range(NA_HEADS // 2):
            ps = slice(pair * LANE, (pair + 1) * LANE)
            v_ext = _with_ones(vn_ref[rows, ps])
            q_pair = qn_ref[rows, ps]
            k_pair = kn_ref[rows, ps]
            o = []
            for half in range(2):
                s = _dot_nt(_head_of_pair(q_pair, masks[half]), k_pair)
                o.append(_softmax_pv([s], [v_ext], NA_SCALE))
            on_ref[rows, ps] = jnp.where(masks[0], o[0], o[1]).astype(BF16)


def _ctx_attn(q, k, v, qn, kn, vn):
    blk = lambda w: pl.BlockSpec((CTX_ATTN_SEQS * SEQ, w), lambda b: (b, 0))
    return pl.pallas_call(
        _ctx_attn_kernel,
        grid=(BATCH // CTX_ATTN_SEQS,),
        in_specs=[blk(MLA_QW), blk(MLA_QW), blk(MLA_HEADS * MLA_V), blk(NA_W), blk(NA_W), blk(NA_W)],
        out_specs=[blk(MLA_HEADS * MLA_V), blk(NA_W)],
        out_shape=[jax.ShapeDtypeStruct((P_TOK, MLA_HEADS * MLA_V), BF16),
                   jax.ShapeDtypeStruct((P_TOK, NA_W), BF16)],
        compiler_params=pltpu.CompilerParams(dimension_semantics=("arbitrary",)),
        name="ctx_attn",
    )(q, k, v, qn, kn, vn)


def _lat_mla_kernel(q_ref, k_ref, v_ref, kc_ref, vc_ref, o_ref):
    low, _ = _half_masks(MLA_QB)
    for pair in range(MLA_HEADS // 2):
        ps = slice(pair * LANE, (pair + 1) * LANE)
        v_ext = [_with_ones(v_ref[:, ps]), _with_ones(vc_ref[:, ps])]
        o = []
        for half in range(2):
            sl = slice((2 * pair + half) * MLA_HP, (2 * pair + half + 1) * MLA_HP)
            q = q_ref[:, sl]
            o.append(_softmax_pv([_dot_nt(q, k_ref[:, sl]), _dot_nt(q, kc_ref[:, sl])], v_ext, MLA_SCALE))
        o_ref[:, ps] = jnp.where(low, o[0], o[1]).astype(BF16)


def _lat_mla(q, k, v, kc, vc, l):
    nq = DEC_SEQ // MLA_QB
    q0 = P_TOK // MLA_QB
    r0 = P_TOK // DEC_SEQ
    return pl.pallas_call(
        _lat_mla_kernel,
        grid=(DEC_BATCH, nq),
        in_specs=[pl.BlockSpec((MLA_QB, MLA_QW), lambda b, j: (q0 + b * nq + j, 0)),
                  pl.BlockSpec((DEC_SEQ, MLA_QW), lambda b, j: (r0 + b, 0)),
                  pl.BlockSpec((DEC_SEQ, MLA_HEADS * MLA_V), lambda b, j: (r0 + b, 0)),
                  pl.BlockSpec((None, None, PAST_LEN, MLA_QW), lambda b, j: (b, l, 0, 0)),
                  pl.BlockSpec((None, None, PAST_LEN, MLA_HEADS * MLA_V), lambda b, j: (b, l, 0, 0))],
        out_specs=pl.BlockSpec((MLA_QB, MLA_HEADS * MLA_V), lambda b, j: (b * nq + j, 0)),
        out_shape=jax.ShapeDtypeStruct((S_TOK, MLA_HEADS * MLA_V), BF16),
        compiler_params=pltpu.CompilerParams(dimension_semantics=("arbitrary", "arbitrary"),
                                             vmem_limit_bytes=VMEM_LIMIT),
        name="lat_mla",
    )(q, k, v, kc, vc)


def _na_window_block(j):
    return jnp.clip(j - 1, 0, NA_BLOCKS - NA_WIN_ROWS // NA_QROWS)


def _lat_na_kernel(q_ref, k_ref, v_ref, kc_ref, vc_ref, bias_ref, o_ref):
    j = pl.program_id(1)
    start = pl.multiple_of(_na_window_block(j) * NA_QB, NA_QB)
    k_win = k_ref[pl.ds(start, NA_WIN), :]
    v_win = v_ref[pl.ds(start, NA_WIN), :]
    masks = _half_masks(NA_QB)
    for pair in range(NA_HEADS // 2):
        ps = slice(pair * LANE, (pair + 1) * LANE)
        k_loc = k_win[:, ps]
        k_ctx_t = kc_ref[ps, :].astype(BF16)
        v_ctx_t = vc_ref[ps, :].astype(BF16)
        v_ext = [_with_ones(v_win[:, ps]), _KeyMinor(jnp.concatenate([v_ctx_t, jnp.ones_like(v_ctx_t)], axis=0))]
        q_pair = q_ref[:, ps]
        o = []
        for half in range(2):
            q = _head_of_pair(q_pair, masks[half])
            s_loc = _dot_nt(q, k_loc) + bias_ref[2 * pair + half]
            o.append(_softmax_pv([s_loc, _dot(q, k_ctx_t)], v_ext, NA_SCALE))
        o_ref[:, ps] = jnp.where(masks[0], o[0], o[1]).astype(BF16)


NA_BLOCK_KINDS = (0, 1, NA_BLOCKS - 1)
NA_DIAG = GRID_W - 1
NA_N_DR = 2 * NA_KH - 1


def _na_bias_kernel(e_ref, o_ref):
    lane = lax.broadcasted_iota(jnp.int32, (GRID_W, LANE), 1)
    q_col = lax.broadcasted_iota(jnp.int32, (GRID_W, LANE), 0)
    k_col = lane % GRID_W
    col_start = jnp.clip(q_col - NA_KW // 2, 0, GRID_W - NA_KW)
    col_ok = (k_col >= col_start) & (k_col < col_start + NA_KW)
    low_half = lane < GRID_W
    neg = jnp.full((GRID_W, LANE), NEG_INF, F32)
    tiles = {}

    def toeplitz(dr, half):
        if (dr, half) not in tiles:
            row = jnp.broadcast_to(e_ref[dr:dr + 1, :], (GRID_W, LANE))
            shift = (LANE - NA_DIAG + half * GRID_W) % LANE
            tiles[dr, half] = pltpu.roll(row, shift, 1, stride=1, stride_axis=0)
        return tiles[dr, half]

    for kind, blk in enumerate(NA_BLOCK_KINDS):
        first_key_row = NA_QROWS * min(max(blk - 1, 0), NA_BLOCKS - NA_WIN_ROWS // NA_QROWS)
        for qr in range(NA_QROWS):
            q_row = NA_QROWS * blk + qr
            row_start = min(max(q_row - NA_KH // 2, 0), GRID_H - NA_KH)
            for pair in range(NA_WIN_ROWS // 2):
                halves = []
                for half in range(2):
                    k_row = first_key_row + 2 * pair + half
                    inside = row_start <= k_row < row_start + NA_KH
                    halves.append(toeplitz(k_row - q_row + NA_KH - 1, half) if inside else neg)
                tile = jnp.where(low_half, halves[0], halves[1])
                o_ref[kind, qr * GRID_W:(qr + 1) * GRID_W, pair * LANE:(pair + 1) * LANE] = (
                    jnp.where(col_ok, tile, NEG_INF))


def _na_bias_table(rpb):
    left = NA_DIAG - (NA_KW - 1)
    e = jnp.concatenate([jnp.broadcast_to(rpb[..., :1], rpb.shape[:-1] + (left,)), rpb,
                         jnp.broadcast_to(rpb[..., -1:], rpb.shape[:-1] + (LANE - left - rpb.shape[-1],))],
                        axis=-1) * (1.0 / NA_SCALE)
    return pl.pallas_call(
        _na_bias_kernel,
        grid=(DEPTH, NA_HEADS),
        in_specs=[pl.BlockSpec((None, None, NA_N_DR, LANE), lambda l, h: (l, h, 0, 0))],
        out_specs=pl.BlockSpec((None, len(NA_BLOCK_KINDS), None, NA_QB, NA_WIN), lambda l, h: (l, 0, h, 0, 0)),
        out_shape=jax.ShapeDtypeStruct((DEPTH, len(NA_BLOCK_KINDS), NA_HEADS, NA_QB, NA_WIN), F32),
        name="na_bias",
    )(e)


def _lat_na(q, k, v, cache_k, cache_v, bias, l):
    q0 = P_TOK // NA_QB
    r0 = P_TOK // DEC_SEQ
    kind = lambda j: (j > 0).astype(jnp.int32) + (j == NA_BLOCKS - 1).astype(jnp.int32)
    return pl.pallas_call(
        _lat_na_kernel,
        grid=(DEC_BATCH, NA_BLOCKS),
        in_specs=[pl.BlockSpec((NA_QB, NA_W), lambda b, j: (q0 + b * NA_BLOCKS + j, 0)),
                  pl.BlockSpec((DEC_SEQ, NA_W), lambda b, j: (r0 + b, 0)),
                  pl.BlockSpec((DEC_SEQ, NA_W), lambda b, j: (r0 + b, 0)),
                  pl.BlockSpec((None, None, PAST_LEN, NA_W), lambda b, j: (b, l, 0, 0)),
                  pl.BlockSpec((None, None, PAST_LEN, NA_W), lambda b, j: (b, l, 0, 0)),
                  pl.BlockSpec((None, None, NA_HEADS, NA_QB, NA_WIN), lambda b, j: (l, kind(j), 0, 0, 0))],
        out_specs=pl.BlockSpec((NA_QB, NA_W), lambda b, j: (b * NA_BLOCKS + j, 0)),
        out_shape=jax.ShapeDtypeStruct((S_TOK, NA_W), BF16),
        compiler_params=pltpu.CompilerParams(dimension_semantics=("arbitrary", "arbitrary"),
                                             vmem_limit_bytes=VMEM_LIMIT),
        name="lat_na",
    )(q, k, v, cache_k, cache_v, bias)


def _merge_kernel(x_ref, h_ref, mod_ref, yfc_ref, ymc_ref, ync_ref, yfl_ref, yml_ref, ynl_ref,
                  wg_ref, bg_ref, wf_ref, wm_ref, wn_ref, wo_ref, g_ref, b_ref, w1_ref, w3_ref, w2_ref,
                  o_ref, w1b_ref, w3b_ref, w2b_ref):
    _cast_ahead((w1_ref, w3_ref, w2_ref), (w1b_ref, w3b_ref, w2b_ref))
    i = pl.program_id(0)
    _, _, gate = _mod_rows(mod_ref, i)
    x = x_ref[...]
    h = h_ref[...]
    is_ctx = i < P_TILES
    mix = None
    for n, (yc_ref, yl_ref, w_ref) in enumerate(((yfc_ref, yfl_ref, wf_ref), (ymc_ref, yml_ref, wm_ref),
                                                 (ync_ref, ynl_ref, wn_ref))):
        sl = slice(n * D_MODEL, (n + 1) * D_MODEL)
        g = jax.nn.sigmoid(_dot(h, wg_ref[:, sl]) + bg_ref[:, sl])
        y = jnp.where(is_ctx, yc_ref[...], yl_ref[...])
        t = g * _dot(y, w_ref[...])
        mix = t if mix is None else mix + t
    z = ALPHA * x + gate * _dot(mix.astype(BF16), wo_ref[...])
    o_ref[...] = _ln(z, 1e-5) * g_ref[...] + b_ref[...]


def _merge(x, h, mod, y_ctx, y_lat, w_gate, b_gate, w_f, w_m, w_n, w_out, ln_g, ln_b, ffn_w, l):
    cast_in, cast_out, cast_shape = _cast_ahead_specs(l)
    tile = lambda w: pl.BlockSpec((TM, w), lambda i: (i, 0))
    ctx_tile = lambda w: pl.BlockSpec((TM, w), lambda i: (jnp.minimum(i, P_TILES - 1), 0))
    lat_tile = lambda w: pl.BlockSpec((TM, w), lambda i: (jnp.maximum(i - P_TILES, 0), 0))
    widths = (F_W, MLA_HEADS * MLA_V, NA_W)
    return pl.pallas_call(
        _merge_kernel,
        grid=(N_TILES,),
        in_specs=[tile(D_MODEL), tile(D_MODEL),
                  _resident((None, 8, 3 * D_MODEL), lambda i: (l, 0, 1))]
                 + [ctx_tile(w) for w in widths] + [lat_tile(w) for w in widths] + [
                  _resident((None, D_MODEL, 3 * D_MODEL), lambda i: (l, 0, 0)),
                  _resident((None, 1, 3 * D_MODEL), lambda i: (l, 0, 0)),
                  _resident((None, F_W, D_MODEL), lambda i: (l, 0, 0)),
                  _resident((None, MLA_HEADS * MLA_V, D_MODEL), lambda i: (l, 0, 0)),
                  _resident((None, NA_W, D_MODEL), lambda i: (l, 0, 0)),
                  _resident((None, D_MODEL, D_MODEL), lambda i: (l, 0, 0)),
                  _resident((None, 1, D_MODEL), lambda i: (3 * l + 1, 0, 0)),
                  _resident((None, 1, D_MODEL), lambda i: (3 * l + 1, 0, 0))] + cast_in,
        out_specs=[tile(D_MODEL)] + cast_out,
        out_shape=[jax.ShapeDtypeStruct((TOKENS, D_MODEL), F32)] + cast_shape,
        compiler_params=pltpu.CompilerParams(dimension_semantics=("arbitrary",),
                                             vmem_limit_bytes=VMEM_LIMIT),
        name="merge",
    )(x, h, mod, *y_ctx, *y_lat, w_gate, b_gate, w_f, w_m, w_n, w_out, ln_g, ln_b, *ffn_w)


def _pad_heads(w, n_heads, width):
    lead = w.shape[:-1]
    w = w.reshape(lead + (n_heads, width))
    w = jnp.pad(w, [(0, 0)] * len(lead) + [(0, 0), (0, MLA_HP - width)])
    return w.reshape(lead + (n_heads * MLA_HP,))


def _rope_tables():
    t = jnp.arange(DEC_SEQ, dtype=jnp.int32)
    pos = jnp.stack([t // GRID_W, t % GRID_W], axis=-1).astype(F32)
    half = AXIS_DIM // 2
    inv_freq = ROPE_BASE ** (-jnp.arange(half, dtype=F32) / half)
    ang = pos[:, :, None] * inv_freq
    ang = jnp.concatenate([ang, ang], axis=-1).reshape(DEC_SEQ, MLA_ROPE)
    pad = lambda a, fill: jnp.pad(a, ((0, 0), (ROPE_LANE0, LANE - ROPE_LANE0 - MLA_ROPE)), constant_values=fill)
    cos = jnp.concatenate([jnp.ones((TM, LANE), F32), pad(jnp.cos(ang), 1.0)], axis=0)
    sin = jnp.concatenate([jnp.zeros((TM, LANE), F32), pad(jnp.sin(ang), 0.0)], axis=0)
    return cos, sin


def kernel(x_prompt, x_sample, cache_mla_ckv, cache_mla_krope, cache_na_k, cache_na_v, c, c_ctx, w_ada, b_ada, ffn1_w1, ffn1_w3, ffn1_w2, ffn2_w1, ffn2_w3, ffn2_w2, w_in, mla_q_norm, mla_w_uq, mla_kv_norm, mla_w_ukv, na_rpb, w_branch_f, w_branch_m, w_branch_n, w_gate, b_gate, w_out, ln_g, ln_b):
    bf = lambda w: w.astype(BF16)
    f1 = (ffn1_w1, ffn1_w3, ffn1_w2)
    f2 = (ffn2_w1, ffn2_w3, ffn2_w2)
    w_in_p = bf(jnp.swapaxes(w_in, 1, 2))
    w_uq_p = bf(_pad_heads(mla_w_uq, MLA_HEADS, MLA_NOPE + MLA_ROPE))
    ukv = mla_w_ukv.reshape(DEPTH, MLA_KV_LORA, MLA_HEADS, MLA_NOPE + MLA_V)
    w_uk_p = bf(_pad_heads(ukv[..., :MLA_NOPE].reshape(DEPTH, MLA_KV_LORA, MLA_HEADS * MLA_NOPE), MLA_HEADS, MLA_NOPE))
    w_uv = bf(ukv[..., MLA_NOPE:].reshape(DEPTH, MLA_KV_LORA, MLA_HEADS * MLA_V))
    w_gate_b, w_f, w_m, w_n, w_out_b = bf(w_gate), bf(w_branch_f), bf(w_branch_m), bf(w_branch_n), bf(w_out)
    q_norm = mla_q_norm.reshape(DEPTH, 1, MLA_Q_LORA)
    kv_norm = mla_kv_norm.reshape(DEPTH, 1, MLA_KV_LORA)
    b_gate3 = b_gate.reshape(DEPTH, 1, 3 * D_MODEL)
    g3 = ln_g.reshape(DEPTH * 3, 1, D_MODEL)
    b3 = ln_b.reshape(DEPTH * 3, 1, D_MODEL)
    cos_t, sin_t = _rope_tables()
    dft_ctx = _fourier_tables(SEQ)
    dft_lat = _fourier_tables(DEC_SEQ)
    kr_pad = jnp.pad(cache_mla_krope, ((0, 0), (0, 0), (0, 0), (ROPE_LANE0, LANE - ROPE_LANE0 - MLA_ROPE)))
    ch_major = lambda a: jnp.transpose(a, (0, 1, 3, 4, 2)).reshape(DEC_BATCH, DEPTH, NA_W, PAST_LEN)
    cache_k = ch_major(cache_na_k)
    cache_v = ch_major(cache_na_v)

    cvec = jnp.concatenate([c_ctx[None], c, jnp.zeros((8 - 1 - DEC_BATCH, D_MODEL), F32)], axis=0)
    mod = _adaln(cvec, w_ada, b_ada)
    kc, vc = _ctx_kv(cache_mla_ckv, kr_pad, w_uk_p, w_uv)
    na_bias = _na_bias_table(na_rpb)

    x = (x_prompt.reshape(P_TOK, D_MODEL), x_sample.reshape(S_TOK, D_MODEL))
    caches = [jnp.zeros(s, F32) for s in _cache_shapes()]
    f1_l = f1
    for l in range(DEPTH):
        x, h = _ffn(x, mod, *f1_l, g3, b3, l, 0, emit_h=True)
        u_f, q, k, v, q_n, k_nb, v_nb, *rest = _mixer_in(
            h, cos_t, sin_t, w_in_p, q_norm, w_uq_p, kv_norm, w_uk_p, w_uv, caches, l,
            next_ffn=(*f1, l + 1) if l + 1 < DEPTH else None)
        caches, f1_l = rest[:4], rest[4:]
        yf_c = _fourier(u_f, dft_ctx, SEQ, 0, BATCH)
        yf_l = _fourier(u_f, dft_lat, DEC_SEQ, P_TOK // DEC_SEQ, DEC_BATCH)
        ym_c, yn_c = _ctx_attn(q, k, v, q_n, k_nb, v_nb)
        ym_l = _lat_mla(q, k, v, kc, vc, l)
        yn_l = _lat_na(q_n, k_nb, v_nb, cache_k, cache_v, na_bias, l)
        x, *f2_l = _merge(x, h, mod, (yf_c, ym_c, yn_c), (yf_l, ym_l, yn_l), w_gate_b, b_gate3, w_f, w_m, w_n,
                          w_out_b, g3, b3, f2, l)
        x = _ffn(x, mod, *f2_l, g3, b3, l, 2, split_out=(l == DEPTH - 1))
        x = tuple(x) if len(x) > 1 else x[0]
    ckv, kr_t, nak_t, nav_t = caches
    per_head = lambda a: jnp.transpose(a.reshape(BATCH, DEPTH, NA_HEADS, NA_HEAD_DIM, SEQ), (0, 1, 4, 2, 3))
    return (x[0].reshape(BATCH, SEQ, D_MODEL), x[1].reshape(DEC_BATCH, DEC_SEQ, D_MODEL),
            ckv, jnp.transpose(kr_t, (0, 1, 3, 2)), per_head(nak_t), per_head(nav_t))
```

```python
import functools

import numpy as np
import jax
import jax.numpy as jnp
from jax import lax
from jax.experimental import pallas as pl
from jax.experimental.pallas import tpu as pltpu

F32 = jnp.float32
BF16 = jnp.bfloat16

D_MODEL = 1024
BATCH = 32
SEQ = 256
DEPTH = 4
DEC_BATCH = 2
DEC_SEQ = 2048
PAST_LEN = 512
GRID_W = 64
GRID_H = DEC_SEQ // GRID_W
D_FF = 2816
F_GROUPS = 4
F_GC = 128
F_W = F_GROUPS * F_GC
MLA_HEADS = 8
MLA_Q_LORA = 384
MLA_KV_LORA = 256
MLA_NOPE = 64
MLA_ROPE = 32
MLA_V = 64
NA_HEADS = 8
NA_HEAD_DIM = 64
NA_KH = 8
NA_KW = 16
NA_W = NA_HEADS * NA_HEAD_DIM
ROPE_BASE = 10000.0
AXIS_DIM = MLA_ROPE // 2
ALPHA = (2.0 * DEPTH) ** 0.25
MLA_SCALE = (MLA_NOPE + MLA_ROPE) ** -0.5
NA_SCALE = NA_HEAD_DIM ** -0.5
NEG_INF = -1e30

LANE = 128
MLA_HP = LANE
MLA_QW = MLA_HEADS * MLA_HP
P_TOK = BATCH * SEQ
S_TOK = DEC_BATCH * DEC_SEQ
TOKENS = P_TOK + S_TOK
TM = 512
SIDE_ROWS = 64
N_TILES = TOKENS // TM
P_TILES = P_TOK // TM
S_TILES_PER_REQ = DEC_SEQ // TM
MXU_TILE = 256
FF_CHUNKS = (0, 4 * MXU_TILE, 8 * MXU_TILE, D_FF)
U_Q0 = F_W
U_KV0 = U_Q0 + MLA_Q_LORA
U_KR0 = U_KV0 + MLA_KV_LORA
U_NA0 = U_KR0 + MLA_ROPE
U_W = U_NA0 + 3 * NA_W
ROPE_LANE0 = MLA_NOPE
NA_QROWS = 4
NA_QB = NA_QROWS * GRID_W
NA_WIN_ROWS = NA_QROWS + NA_KH
NA_WIN = NA_WIN_ROWS * GRID_W
NA_BLOCKS = GRID_H // NA_QROWS
MLA_QB = 1024
CTX_ATTN_SEQS = 4
FOURIER_ROWS = 1024
VMEM_LIMIT = 56 * 1024 * 1024
FFN_VMEM_LIMIT = 60 * 1024 * 1024
STAGE_SLOTS = 4
UP_STAGE_ROWS = 64
DOWN_STAGE_ROWS = D_FF // 16


def _group(i):
    return jnp.where(i < P_TILES, 0, 1 + (i - P_TILES) // S_TILES_PER_REQ)


def _rope_block(i):
    return jnp.where(i < P_TILES, 0, 1 + (i - P_TILES) % S_TILES_PER_REQ)


def _ln(x, eps):
    mu = jnp.mean(x, axis=-1, keepdims=True)
    xc = x - mu
    var = jnp.mean(xc * xc, axis=-1, keepdims=True)
    return xc * lax.rsqrt(var + eps)


def _rms(x, g):
    return x * lax.rsqrt(jnp.mean(x * x, axis=-1, keepdims=True) + 1e-6) * g


def _dot(a, b):
    return jnp.dot(a, b, preferred_element_type=F32)


def _dot_nt(a, b):
    return lax.dot_general(a, b, (((1,), (1,)), ((), ())), preferred_element_type=F32)


def _mod_rows(mod_ref, i):
    m = mod_ref[pl.ds(_group(i), 1), :]
    return m[:, :D_MODEL], m[:, D_MODEL:2 * D_MODEL], m[:, 2 * D_MODEL:]


def _rope(x, cos, sin):
    lane = lax.broadcasted_iota(jnp.int32, x.shape, 1)
    first_half = (lane % AXIS_DIM) < (AXIS_DIM // 2)
    rot = jnp.where(first_half, -pltpu.roll(x, LANE - AXIS_DIM // 2, 1), pltpu.roll(x, AXIS_DIM // 2, 1))
    return x * cos + rot * sin


N_MOD = 9


def _adaln_kernel(c_ref, w_ref, b_ref, o_ref):
    c = c_ref[...]
    s = (c * jax.nn.sigmoid(c)).astype(BF16)
    o_ref[...] = _dot(s, w_ref[...].astype(BF16)) + b_ref[...]


def _adaln(cvec, w_ada, b_ada):
    return pl.pallas_call(
        _adaln_kernel,
        grid=(DEPTH, N_MOD),
        in_specs=[pl.BlockSpec((8, D_MODEL), lambda l, j: (0, 0)),
                  pl.BlockSpec((None, D_MODEL, D_MODEL), lambda l, j: (l, 0, j)),
                  pl.BlockSpec((None, 1, D_MODEL), lambda l, j: (l, 0, j))],
        out_specs=pl.BlockSpec((None, 8, D_MODEL), lambda l, j: (l, 0, j)),
        out_shape=jax.ShapeDtypeStruct((DEPTH, 8, 9 * D_MODEL), F32),
        name="adaln",
    )(cvec, w_ada, b_ada.reshape(DEPTH, 1, 9 * D_MODEL))


def _ffn_kernel(*refs, layer, split_in, split_out, emit_h, staged):
    refs = list(refs)
    x_refs = [refs.pop(0) for _ in range(4 if split_in else 2)]
    mod_ref = refs.pop(0)
    mod_next_ref = refs.pop(0) if emit_h else None
    w_in_refs = [refs.pop(0) for _ in range(3)]
    g_ref, b_ref = refs.pop(0), refs.pop(0)
    o_refs = [refs.pop(0) for _ in range(2 if split_out else 1)]
    hn_ref = refs.pop(0) if emit_h else None
    if staged:
        w1_hbm, w3_hbm, w2_hbm = w_in_refs
        h_scr, y_scr, w1_ref, w3_ref, w2_ref, stage_up, stage_down, sem = refs
    else:
        w1_ref, w3_ref, w2_ref = w_in_refs
        h_scr, y_scr = refs
    s = pl.program_id(0)
    cur = s % 2
    nxt = 1 - cur
    pieces = [pl.ds(r, SIDE_ROWS) for r in range(0, TM, SIDE_ROWS)]

    def x_tile(which, tile):
        if not split_in:
            return lambda rows: x_refs[which][rows, :]
        ctx_ref, lat_ref = x_refs[2 * which:2 * which + 2]
        return lambda rows: jnp.where(tile < P_TILES, ctx_ref[rows, :], lat_ref[rows, :])

    x_prev = x_tile(0, s - 1)
    x_next = x_tile(1, s + 1)

    def modulate(x_rows, tile, slot):
        shift, scale, _ = _mod_rows(mod_ref, tile)

        def piece(rows):
            h_scr[slot, rows, :] = (_ln(x_rows(rows), 1e-6) * (1.0 + scale) + shift).astype(BF16)
        return [functools.partial(piece, rows) for rows in pieces]

    def finish():
        _, _, gate = _mod_rows(mod_ref, s - 1)
        if emit_h:
            shift_n, scale_n, _ = _mod_rows(mod_next_ref, s - 1)

        def piece(rows):
            z = ALPHA * x_prev(rows) + (0.5 * gate) * y_scr[nxt, rows, :]
            out = _ln(z, 1e-5) * g_ref[...] + b_ref[...]
            if split_out:
                to_ctx = jnp.broadcast_to(s - 1 < P_TILES, out.shape)
                pltpu.store(o_refs[0].at[rows, :], out, mask=to_ctx)
                pltpu.store(o_refs[1].at[rows, :], out, mask=jnp.logical_not(to_ctx))
            else:
                o_refs[0][rows, :] = out
            if emit_h:
                hn_ref[rows, :] = (_ln(out, 1e-6) * (1.0 + scale_n) + shift_n).astype(BF16)
            y_scr[nxt, rows, :] = out
        return [functools.partial(piece, rows) for rows in pieces]

    def load_weights():
        chunks = []
        for src, dst in ((w1_hbm, w1_ref), (w3_hbm, w3_ref)):
            for r in range(0, D_MODEL, UP_STAGE_ROWS):
                rows = pl.ds(r, UP_STAGE_ROWS)
                chunks.append((src.at[layer, rows, :], stage_up, dst.at[rows, :]))
        for r in range(0, D_FF, DOWN_STAGE_ROWS):
            rows = pl.ds(r, DOWN_STAGE_ROWS)
            chunks.append((w2_hbm.at[layer, rows, :], stage_down, w2_ref.at[rows, :]))

        def copy(k):
            src, stage, _ = chunks[k]
            return pltpu.make_async_copy(src, stage.at[k % STAGE_SLOTS], sem.at[k % STAGE_SLOTS])

        for k in range(STAGE_SLOTS - 1):
            copy(k).start()
        for k, (_, stage, dst) in enumerate(chunks):
            if k + STAGE_SLOTS - 1 < len(chunks):
                copy(k + STAGE_SLOTS - 1).start()
            copy(k).wait()
            dst[...] = stage[k % STAGE_SLOTS].astype(BF16)

    def matmuls(side_work):
        side_work = list(side_work)
        chunks = [slice(c0, c1) for c0, c1 in zip(FF_CHUNKS[:-1], FF_CHUNKS[1:])]
        per_dot = -(-len(side_work) // (3 * len(chunks)))

        def dot_with_side(lhs, rhs):
            for _ in range(min(per_dot, len(side_work))):
                side_work.pop(0)()
            return _dot(lhs() if callable(lhs) else lhs, rhs)

        h = lambda: h_scr[cur]

        def gate_up(sl):
            a = dot_with_side(h, w1_ref[:, sl])
            return (a * jax.nn.sigmoid(a) * dot_with_side(h, w3_ref[:, sl])).astype(BF16)

        t_next = gate_up(chunks[0])
        for n, sl in enumerate(chunks):
            t = t_next
            if n + 1 < len(chunks):
                t_next = gate_up(chunks[n + 1])
            yc = dot_with_side(t, w2_ref[sl, :])
            if n == 0:
                y_scr[cur] = yc
            else:
                y_scr[cur] += yc
        for work in side_work:
            work()

    @pl.when(s == 0)
    def _():
        if staged:
            load_weights()
        for work in modulate(x_tile(0, s), s, cur):
            work()
        matmuls(modulate(x_next, s + 1, nxt))

    @pl.when(jnp.logical_and(s > 0, s < N_TILES))
    def _():
        matmuls(finish() + modulate(x_next, s + 1, nxt))

    @pl.when(s == N_TILES)
    def _():
        for work in finish():
            work()


def _resident(shape, index_map):
    return pl.BlockSpec(shape, index_map, pipeline_mode=pl.Buffered(1))


def _ffn(x, mod, w1, w3, w2, ln_g, ln_b, l, sub, split_out=False, emit_h=False):
    split_in = isinstance(x, tuple)
    tile = lambda lo, hi, off: pl.BlockSpec((TM, D_MODEL), lambda i: (jnp.clip(i + off, lo, hi) - lo, 0))
    ctx_lat = lambda off: [tile(0, P_TILES - 1, off), tile(P_TILES, N_TILES - 1, off)]
    if split_in:
        x_args = [x[0], x[1]] * 2
        x_specs = ctx_lat(-1) + ctx_lat(1)
    else:
        x_args = [x, x]
        x_specs = [tile(0, N_TILES - 1, -1), tile(0, N_TILES - 1, 1)]
    if split_out:
        out_specs = ctx_lat(-1)
        out_shape = [jax.ShapeDtypeStruct((P_TOK, D_MODEL), F32), jax.ShapeDtypeStruct((S_TOK, D_MODEL), F32)]
    else:
        out_specs = [tile(0, N_TILES - 1, -1)]
        out_shape = [jax.ShapeDtypeStruct((TOKENS, D_MODEL), F32)]
    mod_specs = [_resident((None, 8, 3 * D_MODEL), lambda i: (l, 0, sub))]
    mod_args = [mod]
    if emit_h:
        mod_specs.append(_resident((None, 8, 3 * D_MODEL), lambda i: (l, 0, sub + 1)))
        mod_args.append(mod)
        out_specs = out_specs + [tile(0, N_TILES - 1, -1)]
        out_shape = out_shape + [jax.ShapeDtypeStruct((TOKENS, D_MODEL), BF16)]
    staged = w1.dtype == F32
    scratch = [pltpu.VMEM((2, TM, D_MODEL), BF16), pltpu.VMEM((2, TM, D_MODEL), F32)]
    if staged:
        w_specs = [pl.BlockSpec(memory_space=pl.ANY)] * 3
        scratch += [pltpu.VMEM((D_MODEL, D_FF), BF16), pltpu.VMEM((D_MODEL, D_FF), BF16),
                    pltpu.VMEM((D_FF, D_MODEL), BF16),
                    pltpu.VMEM((STAGE_SLOTS, UP_STAGE_ROWS, D_FF), F32),
                    pltpu.VMEM((STAGE_SLOTS, DOWN_STAGE_ROWS, D_MODEL), F32),
                    pltpu.SemaphoreType.DMA((STAGE_SLOTS,))]
    else:
        w_specs = [_resident(w.shape, lambda i: (0, 0)) for w in (w1, w3, w2)]
    return pl.pallas_call(
        functools.partial(_ffn_kernel, layer=l, split_in=split_in, split_out=split_out, emit_h=emit_h,
                          staged=staged),
        grid=(N_TILES + 1,),
        in_specs=x_specs + mod_specs + w_specs + [
                  _resident((None, 1, D_MODEL), lambda i: (3 * l + sub, 0, 0)),
                  _resident((None, 1, D_MODEL), lambda i: (3 * l + sub, 0, 0))],
        out_specs=out_specs,
        out_shape=out_shape,
        scratch_shapes=scratch,
        compiler_params=pltpu.CompilerParams(dimension_semantics=("arbitrary",),
                                             vmem_limit_bytes=FFN_VMEM_LIMIT),
        name="ffn",
    )(*x_args, *mod_args, w1, w3, w2, ln_g, ln_b)


CAST_BLOCKS = N_TILES // 3


def _cast_ahead_specs(layer):
    up_rows, down_rows = D_MODEL // CAST_BLOCKS, D_FF // CAST_BLOCKS
    blk = lambda n: (lambda i: jnp.clip(i - n * CAST_BLOCKS, 0, CAST_BLOCKS - 1))
    shapes = [(up_rows, D_FF, D_MODEL), (up_rows, D_FF, D_MODEL), (down_rows, D_MODEL, D_FF)]
    in_specs = [pl.BlockSpec((None, r, c), lambda i, b=blk(n): (layer, b(i), 0)) for n, (r, c, _) in enumerate(shapes)]
    out_specs = [pl.BlockSpec((r, c), lambda i, b=blk(n): (b(i), 0)) for n, (r, c, _) in enumerate(shapes)]
    out_shape = [jax.ShapeDtypeStruct((total, c), BF16) for r, c, total in shapes]
    return in_specs, out_specs, out_shape


def _cast_ahead(src_refs, dst_refs):
    for n, (src, dst) in enumerate(zip(src_refs, dst_refs)):
        @pl.when(pl.program_id(0) // CAST_BLOCKS == n)
        def _(src=src, dst=dst):
            dst[...] = src[...].astype(BF16)


def _mixer_in_kernel(*refs, cast_ahead):
    refs = list(refs)
    h_ref, cos_ref, sin_ref, w_in_ref, qn_ref, w_uq_ref, kvn_ref, w_uk_ref, w_uv_ref = refs[:9]
    n_cast = 3 if cast_ahead else 0
    outs = refs[13 + n_cast:]
    uf_ref, q_ref, k_ref, v_ref, qna_ref, knab_ref, vnab_ref, ckv_ref, kr_ref, kna_ref, vna_ref = outs[:11]
    _cast_ahead(refs[13:13 + n_cast], outs[11:])
    i = pl.program_id(0)
    h = h_ref[...]
    cos = cos_ref[...]
    sin = sin_ref[...]

    proj = lambda r0, r1: _dot_nt(h, w_in_ref[r0:r1, :])
    uf_ref[...] = proj(0, U_Q0)

    u_q = proj(U_Q0, U_KV0)
    q = _dot(_rms(u_q, qn_ref[...]).astype(BF16), w_uq_ref[...])
    for hd in range(MLA_HEADS):
        sl = slice(hd * MLA_HP, (hd + 1) * MLA_HP)
        q_ref[:, sl] = _rope(q[:, sl], cos, sin).astype(BF16)

    c_kv = _rms(proj(U_KV0, U_KR0), kvn_ref[...])
    w_kr = jnp.concatenate([jnp.zeros((ROPE_LANE0, D_MODEL), BF16), w_in_ref[U_KR0:U_NA0, :],
                            jnp.zeros((LANE - ROPE_LANE0 - MLA_ROPE, D_MODEL), BF16)], axis=0)
    kr = _dot_nt(h, w_kr)
    k_na = proj(U_NA0 + NA_W, U_NA0 + 2 * NA_W)
    v_na = proj(U_NA0 + 2 * NA_W, U_W)

    @pl.when(i < P_TILES)
    def _():
        kr_t = kr.T[ROPE_LANE0:ROPE_LANE0 + MLA_ROPE, :]
        kna_t = k_na.T
        vna_t = v_na.T
        for b in range(TM // SEQ):
            rows = slice(b * SEQ, (b + 1) * SEQ)
            ckv_ref[b] = c_kv[rows, :]
            kr_ref[b] = kr_t[:, rows]
            kna_ref[b] = kna_t[:, rows]
            vna_ref[b] = vna_t[:, rows]

    c_kv = c_kv.astype(BF16)
    kr = _rope(kr, cos, sin)
    k = _dot(c_kv, w_uk_ref[...])
    for hd in range(MLA_HEADS):
        sl = slice(hd * MLA_HP, (hd + 1) * MLA_HP)
        k_ref[:, sl] = (k[:, sl] + kr).astype(BF16)
    v_ref[...] = _dot(c_kv, w_uv_ref[...]).astype(BF16)

    qna_ref[...] = proj(U_NA0, U_NA0 + NA_W).astype(BF16)
    knab_ref[...] = k_na.astype(BF16)
    vnab_ref[...] = v_na.astype(BF16)


def _cache_shapes():
    return [(BATCH, DEPTH, SEQ, MLA_KV_LORA), (BATCH, DEPTH, MLA_ROPE, SEQ),
            (BATCH, DEPTH, NA_W, SEQ), (BATCH, DEPTH, NA_W, SEQ)]


def _mixer_in(h, cos_t, sin_t, w_in, q_norm, w_uq, kv_norm, w_uk, w_uv, caches, l, next_ffn=None):
    tile = lambda w: pl.BlockSpec((TM, w), lambda i: (i, 0))
    acts = [(F_W, F32), (MLA_QW, BF16), (MLA_QW, BF16), (MLA_HEADS * MLA_V, BF16), (NA_W, BF16), (NA_W, BF16),
            (NA_W, BF16)]
    cache_spec = lambda s: pl.BlockSpec((TM // SEQ, None) + s[2:],
                                        lambda i: (jnp.minimum(i, P_TILES - 1), l, 0, 0))
    n_in = 9
    cast_in, cast_out, cast_shape = _cast_ahead_specs(next_ffn[3]) if next_ffn else ([], [], [])
    cast_args = list(next_ffn[:3]) if next_ffn else []
    return pl.pallas_call(
        functools.partial(_mixer_in_kernel, cast_ahead=bool(next_ffn)),
        grid=(N_TILES,),
        in_specs=[tile(D_MODEL),
                  pl.BlockSpec((TM, LANE), lambda i: (_rope_block(i), 0)),
                  pl.BlockSpec((TM, LANE), lambda i: (_rope_block(i), 0)),
                  _resident((None, U_W, D_MODEL), lambda i: (l, 0, 0)),
                  _resident((None, 1, MLA_Q_LORA), lambda i: (l, 0, 0)),
                  _resident((None, MLA_Q_LORA, MLA_QW), lambda i: (l, 0, 0)),
                  _resident((None, 1, MLA_KV_LORA), lambda i: (l, 0, 0)),
                  _resident((None, MLA_KV_LORA, MLA_QW), lambda i: (l, 0, 0)),
                  _resident((None, MLA_KV_LORA, MLA_HEADS * MLA_V), lambda i: (l, 0, 0))]
                 + [pl.BlockSpec(memory_space=pl.ANY)] * len(caches) + cast_in,
        out_specs=[tile(w) for w, _ in acts] + [cache_spec(s) for s in _cache_shapes()] + cast_out,
        out_shape=[jax.ShapeDtypeStruct((TOKENS, w), dt) for w, dt in acts]
                  + [jax.ShapeDtypeStruct(s, F32) for s in _cache_shapes()] + cast_shape,
        input_output_aliases={n_in + n: len(acts) + n for n in range(len(caches))},
        compiler_params=pltpu.CompilerParams(dimension_semantics=("arbitrary",),
                                             vmem_limit_bytes=VMEM_LIMIT),
        name="mixer_in",
    )(h, cos_t, sin_t, w_in, q_norm, w_uq, kv_norm, w_uk, w_uv, *caches, *cast_args)


def _ctx_kv_kernel(ckv_ref, kr_ref, w_uk_ref, w_uv_ref, k_ref, v_ref):
    c = ckv_ref[...].astype(BF16)
    k = _dot(c, w_uk_ref[...])
    kr = kr_ref[...]
    for hd in range(MLA_HEADS):
        sl = slice(hd * MLA_HP, (hd + 1) * MLA_HP)
        k_ref[:, sl] = (k[:, sl] + kr).astype(BF16)
    v_ref[...] = _dot(c, w_uv_ref[...]).astype(BF16)


def _ctx_kv(cache_ckv, cache_kr_pad, w_uk, w_uv):
    return pl.pallas_call(
        _ctx_kv_kernel,
        grid=(DEC_BATCH, DEPTH),
        in_specs=[pl.BlockSpec((None, None, PAST_LEN, MLA_KV_LORA), lambda b, l: (b, l, 0, 0)),
                  pl.BlockSpec((None, None, PAST_LEN, LANE), lambda b, l: (b, l, 0, 0)),
                  pl.BlockSpec((None, MLA_KV_LORA, MLA_QW), lambda b, l: (l, 0, 0)),
                  pl.BlockSpec((None, MLA_KV_LORA, MLA_HEADS * MLA_V), lambda b, l: (l, 0, 0))],
        out_specs=[pl.BlockSpec((None, None, PAST_LEN, MLA_QW), lambda b, l: (b, l, 0, 0)),
                   pl.BlockSpec((None, None, PAST_LEN, MLA_HEADS * MLA_V), lambda b, l: (b, l, 0, 0))],
        out_shape=[jax.ShapeDtypeStruct((DEC_BATCH, DEPTH, PAST_LEN, MLA_QW), BF16),
                   jax.ShapeDtypeStruct((DEC_BATCH, DEPTH, PAST_LEN, MLA_HEADS * MLA_V), BF16)],
        name="ctx_kv",
    )(cache_ckv, cache_kr_pad, w_uk, w_uv)


def _fourier_kernel(x_ref, cs_ref, cl_ref, sl_ref, o_ref):
    length = cl_ref.shape[0]
    x = x_ref[...].astype(BF16)
    xc, xs = [], []
    for g in range(F_GROUPS):
        t = _dot(x[:, g * F_GC:(g + 1) * F_GC], cs_ref[...])
        xc.append(t[:, :F_GC])
        xs.append(t[:, F_GC:])
    xc = jnp.concatenate(xc, axis=1).astype(BF16)
    xs = jnp.concatenate(xs, axis=1).astype(BF16)
    for r in range(0, x_ref.shape[0], length):
        rows = slice(r, r + length)
        o_ref[rows, :] = (_dot(cl_ref[...], xc[rows]) - _dot(sl_ref[...], xs[rows])).astype(BF16)


def _dft_tables(n):
    k = np.arange(n, dtype=np.int64)
    ang = 2.0 * np.pi * ((k[:, None] * k[None, :]) % n).astype(np.float64) / n
    s = n ** -0.5
    return np.cos(ang) * s, np.sin(ang) * s


def _fourier_tables(length):
    cc, sc = _dft_tables(F_GC)
    cl, sl = _dft_tables(length)
    as_bf16 = lambda a: jnp.asarray(a, F32).astype(BF16)
    return as_bf16(np.concatenate([cc, sc], axis=1)), as_bf16(cl), as_bf16(sl)


def _fourier(u_f, tables, length, first_block, n_blocks):
    cs, cl, sl = tables
    rows = max(length, FOURIER_ROWS)
    assert rows % length == 0 and (n_blocks * length) % rows == 0 and (first_block * length) % rows == 0
    first = first_block * length // rows
    return pl.pallas_call(
        _fourier_kernel,
        grid=(n_blocks * length // rows,),
        in_specs=[pl.BlockSpec((rows, F_W), lambda b: (first + b, 0)),
                  _resident((F_GC, 2 * F_GC), lambda b: (0, 0)),
                  _resident((length, length), lambda b: (0, 0)),
                  _resident((length, length), lambda b: (0, 0))],
        out_specs=pl.BlockSpec((rows, F_W), lambda b: (b, 0)),
        out_shape=jax.ShapeDtypeStruct((n_blocks * length, F_W), BF16),
        compiler_params=pltpu.CompilerParams(dimension_semantics=("arbitrary",),
                                             vmem_limit_bytes=VMEM_LIMIT),
        name="fourier_%d" % length,
    )(u_f, cs, cl, sl)


LOG2E = 1.4426950408889634


def _softmax_pv(scores, values_ext, scale):
    m = None
    for s in scores:
        sm = jnp.max(s, axis=-1, keepdims=True)
        m = sm if m is None else jnp.maximum(m, sm)
    acc = None
    for s, v in zip(scores, values_ext):
        p = jnp.exp2((s - m) * (scale * LOG2E)).astype(BF16)
        pv = _dot_nt(p, v.t) if isinstance(v, _KeyMinor) else _dot(p, v)
        acc = pv if acc is None else acc + pv
    return acc[:, :LANE] / acc[:, LANE:]


class _KeyMinor:
    def __init__(self, t):
        self.t = t


def _with_ones(v_pair):
    return jnp.concatenate([v_pair, jnp.ones_like(v_pair)], axis=1)


def _half_masks(rows):
    low = lax.broadcasted_iota(jnp.int32, (rows, LANE), 1) < LANE // 2
    return low, jnp.logical_not(low)


def _head_of_pair(x_pair, mask):
    return jnp.where(mask, x_pair, jnp.zeros_like(x_pair))


def _ctx_attn_kernel(q_ref, k_ref, v_ref, qn_ref, kn_ref, vn_ref, om_ref, on_ref):
    masks = _half_masks(SEQ)
    for r in range(0, CTX_ATTN_SEQS * SEQ, SEQ):
        rows = slice(r, r + SEQ)
        for pair in range(MLA_HEADS // 2):
            ps = slice(pair * LANE, (pair + 1) * LANE)
            v_ext = _with_ones(v_ref[rows, ps])
            o = []
            for half in range(2):
                sl = slice((2 * pair + half) * MLA_HP, (2 * pair + half + 1) * MLA_HP)
                s = _dot_nt(q_ref[rows, sl], k_ref[rows, sl])
                o.append(_softmax_pv([s], [v_ext], MLA_SCALE))
            om_ref[rows, ps] = jnp.where(masks[0], o[0], o[1]).astype(BF16)
        for pair in range(NA_HEADS // 2):
            ps = slice(pair * LANE, (pair + 1) * LANE)
            v_ext = _with_ones(vn_ref[rows, ps])
            q_pair = qn_ref[rows, ps]
            k_pair = kn_ref[rows, ps]
            o = []
            for half in range(2):
                s = _dot_nt(_head_of_pair(q_pair, masks[half]), k_pair)
                o.append(_softmax_pv([s], [v_ext], NA_SCALE))
            on_ref[rows, ps] = jnp.where(masks[0], o[0], o[1]).astype(BF16)


def _ctx_attn(q, k, v, qn, kn, vn):
    blk = lambda w: pl.BlockSpec((CTX_ATTN_SEQS * SEQ, w), lambda b: (b, 0))
    return pl.pallas_call(
        _ctx_attn_kernel,
        grid=(BATCH // CTX_ATTN_SEQS,),
        in_specs=[blk(MLA_QW), blk(MLA_QW), blk(MLA_HEADS * MLA_V), blk(NA_W), blk(NA_W), blk(NA_W)],
        out_specs=[blk(MLA_HEADS * MLA_V), blk(NA_W)],
        out_shape=[jax.ShapeDtypeStruct((P_TOK, MLA_HEADS * MLA_V), BF16),
                   jax.ShapeDtypeStruct((P_TOK, NA_W), BF16)],
        compiler_params=pltpu.CompilerParams(dimension_semantics=("arbitrary",)),
        name="ctx_attn",
    )(q, k, v, qn, kn, vn)


def _lat_mla_kernel(q_ref, k_ref, v_ref, kc_ref, vc_ref, o_ref):
    low, _ = _half_masks(MLA_QB)
    for pair in range(MLA_HEADS // 2):
        ps = slice(pair * LANE, (pair + 1) * LANE)
        v_ext = [_with_ones(v_ref[:, ps]), _with_ones(vc_ref[:, ps])]
        o = []
        for half in range(2):
            sl = slice((2 * pair + half) * MLA_HP, (2 * pair + half + 1) * MLA_HP)
            q = q_ref[:, sl]
            o.append(_softmax_pv([_dot_nt(q, k_ref[:, sl]), _dot_nt(q, kc_ref[:, sl])], v_ext, MLA_SCALE))
        o_ref[:, ps] = jnp.where(low, o[0], o[1]).astype(BF16)


def _lat_mla(q, k, v, kc, vc, l):
    nq = DEC_SEQ // MLA_QB
    q0 = P_TOK // MLA_QB
    r0 = P_TOK // DEC_SEQ
    return pl.pallas_call(
        _lat_mla_kernel,
        grid=(DEC_BATCH, nq),
        in_specs=[pl.BlockSpec((MLA_QB, MLA_QW), lambda b, j: (q0 + b * nq + j, 0)),
                  pl.BlockSpec((DEC_SEQ, MLA_QW), lambda b, j: (r0 + b, 0)),
                  pl.BlockSpec((DEC_SEQ, MLA_HEADS * MLA_V), lambda b, j: (r0 + b, 0)),
                  pl.BlockSpec((None, None, PAST_LEN, MLA_QW), lambda b, j: (b, l, 0, 0)),
                  pl.BlockSpec((None, None, PAST_LEN, MLA_HEADS * MLA_V), lambda b, j: (b, l, 0, 0))],
        out_specs=pl.BlockSpec((MLA_QB, MLA_HEADS * MLA_V), lambda b, j: (b * nq + j, 0)),
        out_shape=jax.ShapeDtypeStruct((S_TOK, MLA_HEADS * MLA_V), BF16),
        compiler_params=pltpu.CompilerParams(dimension_semantics=("arbitrary", "arbitrary"),
                                             vmem_limit_bytes=VMEM_LIMIT),
        name="lat_mla",
    )(q, k, v, kc, vc)


def _na_window_block(j):
    return jnp.clip(j - 1, 0, NA_BLOCKS - NA_WIN_ROWS // NA_QROWS)


def _lat_na_kernel(q_ref, k_ref, v_ref, kc_ref, vc_ref, bias_ref, o_ref):
    j = pl.program_id(1)
    start = pl.multiple_of(_na_window_block(j) * NA_QB, NA_QB)
    k_win = k_ref[pl.ds(start, NA_WIN), :]
    v_win = v_ref[pl.ds(start, NA_WIN), :]
    masks = _half_masks(NA_QB)
    for pair in range(NA_HEADS // 2):
        ps = slice(pair * LANE, (pair + 1) * LANE)
        k_loc = k_win[:, ps]
        k_ctx_t = kc_ref[ps, :].astype(BF16)
        v_ctx_t = vc_ref[ps, :].astype(BF16)
        v_ext = [_with_ones(v_win[:, ps]), _KeyMinor(jnp.concatenate([v_ctx_t, jnp.ones_like(v_ctx_t)], axis=0))]
        q_pair = q_ref[:, ps]
        o = []
        for half in range(2):
            q = _head_of_pair(q_pair, masks[half])
            s_loc = _dot_nt(q, k_loc) + bias_ref[2 * pair + half]
            o.append(_softmax_pv([s_loc, _dot(q, k_ctx_t)], v_ext, NA_SCALE))
        o_ref[:, ps] = jnp.where(masks[0], o[0], o[1]).astype(BF16)


NA_BLOCK_KINDS = (0, 1, NA_BLOCKS - 1)
NA_DIAG = GRID_W - 1
NA_N_DR = 2 * NA_KH - 1


def _na_bias_kernel(e_ref, o_ref):
    lane = lax.broadcasted_iota(jnp.int32, (GRID_W, LANE), 1)
    q_col = lax.broadcasted_iota(jnp.int32, (GRID_W, LANE), 0)
    k_col = lane % GRID_W
    col_start = jnp.clip(q_col - NA_KW // 2, 0, GRID_W - NA_KW)
    col_ok = (k_col >= col_start) & (k_col < col_start + NA_KW)
    low_half = lane < GRID_W
    neg = jnp.full((GRID_W, LANE), NEG_INF, F32)
    tiles = {}

    def toeplitz(dr, half):
        if (dr, half) not in tiles:
            row = jnp.broadcast_to(e_ref[dr:dr + 1, :], (GRID_W, LANE))
            shift = (LANE - NA_DIAG + half * GRID_W) % LANE
            tiles[dr, half] = pltpu.roll(row, shift, 1, stride=1, stride_axis=0)
        return tiles[dr, half]

    for kind, blk in enumerate(NA_BLOCK_KINDS):
        first_key_row = NA_QROWS * min(max(blk - 1, 0), NA_BLOCKS - NA_WIN_ROWS // NA_QROWS)
        for qr in range(NA_QROWS):
            q_row = NA_QROWS * blk + qr
            row_start = min(max(q_row - NA_KH // 2, 0), GRID_H - NA_KH)
            for pair in range(NA_WIN_ROWS // 2):
                halves = []
                for half in range(2):
                    k_row = first_key_row + 2 * pair + half
                    inside = row_start <= k_row < row_start + NA_KH
                    halves.append(toeplitz(k_row - q_row + NA_KH - 1, half) if inside else neg)
                tile = jnp.where(low_half, halves[0], halves[1])
                o_ref[kind, qr * GRID_W:(qr + 1) * GRID_W, pair * LANE:(pair + 1) * LANE] = (
                    jnp.where(col_ok, tile, NEG_INF))


def _na_bias_table(rpb):
    left = NA_DIAG - (NA_KW - 1)
    e = jnp.concatenate([jnp.broadcast_to(rpb[..., :1], rpb.shape[:-1] + (left,)), rpb,
                         jnp.broadcast_to(rpb[..., -1:], rpb.shape[:-1] + (LANE - left - rpb.shape[-1],))],
                        axis=-1) * (1.0 / NA_SCALE)
    return pl.pallas_call(
        _na_bias_kernel,
        grid=(DEPTH, NA_HEADS),
        in_specs=[pl.BlockSpec((None, None, NA_N_DR, LANE), lambda l, h: (l, h, 0, 0))],
        out_specs=pl.BlockSpec((None, len(NA_BLOCK_KINDS), None, NA_QB, NA_WIN), lambda l, h: (l, 0, h, 0, 0)),
        out_shape=jax.ShapeDtypeStruct((DEPTH, len(NA_BLOCK_KINDS), NA_HEADS, NA_QB, NA_WIN), F32),
        name="na_bias",
    )(e)


def _lat_na(q, k, v, cache_k, cache_v, bias, l):
    q0 = P_TOK // NA_QB
    r0 = P_TOK // DEC_SEQ
    kind = lambda j: (j > 0).astype(jnp.int32) + (j == NA_BLOCKS - 1).astype(jnp.int32)
    return pl.pallas_call(
        _lat_na_kernel,
        grid=(DEC_BATCH, NA_BLOCKS),
        in_specs=[pl.BlockSpec((NA_QB, NA_W), lambda b, j: (q0 + b * NA_BLOCKS + j, 0)),
                  pl.BlockSpec((DEC_SEQ, NA_W), lambda b, j: (r0 + b, 0)),
                  pl.BlockSpec((DEC_SEQ, NA_W), lambda b, j: (r0 + b, 0)),
                  pl.BlockSpec((None, None, PAST_LEN, NA_W), lambda b, j: (b, l, 0, 0)),
                  pl.BlockSpec((None, None, PAST_LEN, NA_W), lambda b, j: (b, l, 0, 0)),
                  pl.BlockSpec((None, None, NA_HEADS, NA_QB, NA_WIN), lambda b, j: (l, kind(j), 0, 0, 0))],
        out_specs=pl.BlockSpec((NA_QB, NA_W), lambda b, j: (b * NA_BLOCKS + j, 0)),
        out_shape=jax.ShapeDtypeStruct((S_TOK, NA_W), BF16),
        compiler_params=pltpu.CompilerParams(dimension_semantics=("arbitrary", "arbitrary"),
                                             vmem_limit_bytes=VMEM_LIMIT),
        name="lat_na",
    )(q, k, v, cache_k, cache_v, bias)


def _merge_kernel(x_ref, h_ref, mod_ref, yfc_ref, ymc_ref, ync_ref, yfl_ref, yml_ref, ynl_ref,
                  wg_ref, bg_ref, wf_ref, wm_ref, wn_ref, wo_ref, g_ref, b_ref, w1_ref, w3_ref, w2_ref,
                  o_ref, w1b_ref, w3b_ref, w2b_ref):
    _cast_ahead((w1_ref, w3_ref, w2_ref), (w1b_ref, w3b_ref, w2b_ref))
    i = pl.program_id(0)
    _, _, gate = _mod_rows(mod_ref, i)
    x = x_ref[...]
    h = h_ref[...]
    is_ctx = i < P_TILES
    mix = None
    for n, (yc_ref, yl_ref, w_ref) in enumerate(((yfc_ref, yfl_ref, wf_ref), (ymc_ref, yml_ref, wm_ref),
                                                 (ync_ref, ynl_ref, wn_ref))):
        sl = slice(n * D_MODEL, (n + 1) * D_MODEL)
        g = jax.nn.sigmoid(_dot(h, wg_ref[:, sl]) + bg_ref[:, sl])
        y = jnp.where(is_ctx, yc_ref[...], yl_ref[...])
        t = g * _dot(y, w_ref[...])
        mix = t if mix is None else mix + t
    z = ALPHA * x + gate * _dot(mix.astype(BF16), wo_ref[...])
    o_ref[...] = _ln(z, 1e-5) * g_ref[...] + b_ref[...]


def _merge(x, h, mod, y_ctx, y_lat, w_gate, b_gate, w_f, w_m, w_n, w_out, ln_g, ln_b, ffn_w, l):
    cast_in, cast_out, cast_shape = _cast_ahead_specs(l)
    tile = lambda w: pl.BlockSpec((TM, w), lambda i: (i, 0))
    ctx_tile = lambda w: pl.BlockSpec((TM, w), lambda i: (jnp.minimum(i, P_TILES - 1), 0))
    lat_tile = lambda w: pl.BlockSpec((TM, w), lambda i: (jnp.maximum(i - P_TILES, 0), 0))
    widths = (F_W, MLA_HEADS * MLA_V, NA_W)
    return pl.pallas_call(
        _merge_kernel,
        grid=(N_TILES,),
        in_specs=[tile(D_MODEL), tile(D_MODEL),
                  _resident((None, 8, 3 * D_MODEL), lambda i: (l, 0, 1))]
                 + [ctx_tile(w) for w in widths] + [lat_tile(w) for w in widths] + [
                  _resident((None, D_MODEL, 3 * D_MODEL), lambda i: (l, 0, 0)),
                  _resident((None, 1, 3 * D_MODEL), lambda i: (l, 0, 0)),
                  _resident((None, F_W, D_MODEL), lambda i: (l, 0, 0)),
                  _resident((None, MLA_HEADS * MLA_V, D_MODEL), lambda i: (l, 0, 0)),
                  _resident((None, NA_W, D_MODEL), lambda i: (l, 0, 0)),
                  _resident((None, D_MODEL, D_MODEL), lambda i: (l, 0, 0)),
                  _resident((None, 1, D_MODEL), lambda i: (3 * l + 1, 0, 0)),
                  _resident((None, 1, D_MODEL), lambda i: (3 * l + 1, 0, 0))] + cast_in,
        out_specs=[tile(D_MODEL)] + cast_out,
        out_shape=[jax.ShapeDtypeStruct((TOKENS, D_MODEL), F32)] + cast_shape,
        compiler_params=pltpu.CompilerParams(dimension_semantics=("arbitrary",),
                                             vmem_limit_bytes=VMEM_LIMIT),
        name="merge",
    )(x, h, mod, *y_ctx, *y_lat, w_gate, b_gate, w_f, w_m, w_n, w_out, ln_g, ln_b, *ffn_w)


def _pad_heads(w, n_heads, width):
    lead = w.shape[:-1]
    w = w.reshape(lead + (n_heads, width))
    w = jnp.pad(w, [(0, 0)] * len(lead) + [(0, 0), (0, MLA_HP - width)])
    return w.reshape(lead + (n_heads * MLA_HP,))


def _rope_tables():
    t = jnp.arange(DEC_SEQ, dtype=jnp.int32)
    pos = jnp.stack([t // GRID_W, t % GRID_W], axis=-1).astype(F32)
    half = AXIS_DIM // 2
    inv_freq = ROPE_BASE ** (-jnp.arange(half, dtype=F32) / half)
    ang = pos[:, :, None] * inv_freq
    ang = jnp.concatenate([ang, ang], axis=-1).reshape(DEC_SEQ, MLA_ROPE)
    pad = lambda a, fill: jnp.pad(a, ((0, 0), (ROPE_LANE0, LANE - ROPE_LANE0 - MLA_ROPE)), constant_values=fill)
    cos = jnp.concatenate([jnp.ones((TM, LANE), F32), pad(jnp.cos(ang), 1.0)], axis=0)
    sin = jnp.concatenate([jnp.zeros((TM, LANE), F32), pad(jnp.sin(ang), 0.0)], axis=0)
    return cos, sin


def kernel(x_prompt, x_sample, cache_mla_ckv, cache_mla_krope, cache_na_k, cache_na_v, c, c_ctx, w_ada, b_ada, ffn1_w1, ffn1_w3, ffn1_w2, ffn2_w1, ffn2_w3, ffn2_w2, w_in, mla_q_norm, mla_w_uq, mla_kv_norm, mla_w_ukv, na_rpb, w_branch_f, w_branch_m, w_branch_n, w_gate, b_gate, w_out, ln_g, ln_b):
    bf = lambda w: w.astype(BF16)
    f1 = (ffn1_w1, ffn1_w3, ffn1_w2)
    f2 = (ffn2_w1, ffn2_w3, ffn2_w2)
    w_in_p = bf(jnp.swapaxes(w_in, 1, 2))
    w_uq_p = bf(_pad_heads(mla_w_uq, MLA_HEADS, MLA_NOPE + MLA_ROPE))
    ukv = mla_w_ukv.reshape(DEPTH, MLA_KV_LORA, MLA_HEADS, MLA_NOPE + MLA_V)
    w_uk_p = bf(_pad_heads(ukv[..., :MLA_NOPE].reshape(DEPTH, MLA_KV_LORA, MLA_HEADS * MLA_NOPE), MLA_HEADS, MLA_NOPE))
    w_uv = bf(ukv[..., MLA_NOPE:].reshape(DEPTH, MLA_KV_LORA, MLA_HEADS * MLA_V))
    w_gate_b, w_f, w_m, w_n, w_out_b = bf(w_gate), bf(w_branch_f), bf(w_branch_m), bf(w_branch_n), bf(w_out)
    q_norm = mla_q_norm.reshape(DEPTH, 1, MLA_Q_LORA)
    kv_norm = mla_kv_norm.reshape(DEPTH, 1, MLA_KV_LORA)
    b_gate3 = b_gate.reshape(DEPTH, 1, 3 * D_MODEL)
    g3 = ln_g.reshape(DEPTH * 3, 1, D_MODEL)
    b3 = ln_b.reshape(DEPTH * 3, 1, D_MODEL)
    cos_t, sin_t = _rope_tables()
    dft_ctx = _fourier_tables(SEQ)
    dft_lat = _fourier_tables(DEC_SEQ)
    kr_pad = jnp.pad(cache_mla_krope, ((0, 0), (0, 0), (0, 0), (ROPE_LANE0, LANE - ROPE_LANE0 - MLA_ROPE)))
    ch_major = lambda a: jnp.transpose(a, (0, 1, 3, 4, 2)).reshape(DEC_BATCH, DEPTH, NA_W, PAST_LEN)
    cache_k = ch_major(cache_na_k)
    cache_v = ch_major(cache_na_v)

    cvec = jnp.concatenate([c_ctx[None], c, jnp.zeros((8 - 1 - DEC_BATCH, D_MODEL), F32)], axis=0)
    mod = _adaln(cvec, w_ada, b_ada)
    kc, vc = _ctx_kv(cache_mla_ckv, kr_pad, w_uk_p, w_uv)
    na_bias = _na_bias_table(na_rpb)

    x = (x_prompt.reshape(P_TOK, D_MODEL), x_sample.reshape(S_TOK, D_MODEL))
    caches = [jnp.zeros(s, F32) for s in _cache_shapes()]
    f1_l = f1
    for l in range(DEPTH):
        x, h = _ffn(x, mod, *f1_l, g3, b3, l, 0, emit_h=True)
        u_f, q, k, v, q_n, k_nb, v_nb, *rest = _mixer_in(
            h, cos_t, sin_t, w_in_p, q_norm, w_uq_p, kv_norm, w_uk_p, w_uv, caches, l,
            next_ffn=(*f1, l + 1) if l + 1 < DEPTH else None)
        caches, f1_l = rest[:4], rest[4:]
        yf_c = _fourier(u_f, dft_ctx, SEQ, 0, BATCH)
        yf_l = _fourier(u_f, dft_lat, DEC_SEQ, P_TOK // DEC_SEQ, DEC_BATCH)
        ym_c, yn_c = _ctx_attn(q, k, v, q_n, k_nb, v_nb)
        ym_l = _lat_mla(q, k, v, kc, vc, l)
        yn_l = _lat_na(q_n, k_nb, v_nb, cache_k, cache_v, na_bias, l)
        x, *f2_l = _merge(x, h, mod, (yf_c, ym_c, yn_c), (yf_l, ym_l, yn_l), w_gate_b, b_gate3, w_f, w_m, w_n,
                          w_out_b, g3, b3, f2, l)
        x = _ffn(x, mod, *f2_l, g3, b3, l, 2, split_out=(l == DEPTH - 1))
        x = tuple(x) if len(x) > 1 else x[0]
    ckv, kr_t, nak_t, nav_t = caches
    per_head = lambda a: jnp.transpose(a.reshape(BATCH, DEPTH, NA_HEADS, NA_HEAD_DIM, SEQ), (0, 1, 4, 2, 3))
    return (x[0].reshape(BATCH, SEQ, D_MODEL), x[1].reshape(DEC_BATCH, DEC_SEQ, D_MODEL),
            ckv, jnp.transpose(kr_t, (0, 1, 3, 2)), per_head(nak_t), per_head(nav_t))
```

```python
import functools

import numpy as np
import jax
import jax.numpy as jnp
from jax import lax
from jax.experimental import pallas as pl
from jax.experimental.pallas import tpu as pltpu

F32 = jnp.float32
BF16 = jnp.bfloat16

D_MODEL = 1024
BATCH = 32
SEQ = 256
DEPTH = 4
DEC_BATCH = 2
DEC_SEQ = 2048
PAST_LEN = 512
GRID_W = 64
GRID_H = DEC_SEQ // GRID_W
D_FF = 2816
F_GROUPS = 4
F_GC = 128
F_W = F_GROUPS * F_GC
MLA_HEADS = 8
MLA_Q_LORA = 384
MLA_KV_LORA = 256
MLA_NOPE = 64
MLA_ROPE = 32
MLA_V = 64
NA_HEADS = 8
NA_HEAD_DIM = 64
NA_KH = 8
NA_KW = 16
NA_W = NA_HEADS * NA_HEAD_DIM
ROPE_BASE = 10000.0
AXIS_DIM = MLA_ROPE // 2
ALPHA = (2.0 * DEPTH) ** 0.25
MLA_SCALE = (MLA_NOPE + MLA_ROPE) ** -0.5
NA_SCALE = NA_HEAD_DIM ** -0.5
NEG_INF = -1e30

LANE = 128
MLA_HP = LANE
MLA_QW = MLA_HEADS * MLA_HP
P_TOK = BATCH * SEQ
S_TOK = DEC_BATCH * DEC_SEQ
TOKENS = P_TOK + S_TOK
TM = 512
SIDE_ROWS = 64
N_TILES = TOKENS // TM
P_TILES = P_TOK // TM
S_TILES_PER_REQ = DEC_SEQ // TM
MXU_TILE = 256
FF_CHUNKS = (0, 4 * MXU_TILE, 8 * MXU_TILE, D_FF)
U_Q0 = F_W
U_KV0 = U_Q0 + MLA_Q_LORA
U_KR0 = U_KV0 + MLA_KV_LORA
U_NA0 = U_KR0 + MLA_ROPE
U_W = U_NA0 + 3 * NA_W
ROPE_LANE0 = MLA_NOPE
NA_QROWS = 4
NA_QB = NA_QROWS * GRID_W
NA_WIN_ROWS = NA_QROWS + NA_KH
NA_WIN = NA_WIN_ROWS * GRID_W
NA_BLOCKS = GRID_H // NA_QROWS
MLA_QB = 1024
CTX_ATTN_SEQS = 4
FOURIER_ROWS = 1024
VMEM_LIMIT = 56 * 1024 * 1024
FFN_VMEM_LIMIT = 60 * 1024 * 1024
STAGE_SLOTS = 4
UP_STAGE_ROWS = 64
DOWN_STAGE_ROWS = D_FF // 16


def _group(i):
    return jnp.where(i < P_TILES, 0, 1 + (i - P_TILES) // S_TILES_PER_REQ)


def _rope_block(i):
    return jnp.where(i < P_TILES, 0, 1 + (i - P_TILES) % S_TILES_PER_REQ)


def _ln(x, eps):
    mu = jnp.mean(x, axis=-1, keepdims=True)
    xc = x - mu
    var = jnp.mean(xc * xc, axis=-1, keepdims=True)
    return xc * lax.rsqrt(var + eps)


def _rms(x, g):
    return x * lax.rsqrt(jnp.mean(x * x, axis=-1, keepdims=True) + 1e-6) * g


def _dot(a, b):
    return jnp.dot(a, b, preferred_element_type=F32)


def _dot_nt(a, b):
    return lax.dot_general(a, b, (((1,), (1,)), ((), ())), preferred_element_type=F32)


def _mod_rows(mod_ref, i):
    m = mod_ref[pl.ds(_group(i), 1), :]
    return m[:, :D_MODEL], m[:, D_MODEL:2 * D_MODEL], m[:, 2 * D_MODEL:]


def _rope(x, cos, sin):
    lane = lax.broadcasted_iota(jnp.int32, x.shape, 1)
    first_half = (lane % AXIS_DIM) < (AXIS_DIM // 2)
    rot = jnp.where(first_half, -pltpu.roll(x, LANE - AXIS_DIM // 2, 1), pltpu.roll(x, AXIS_DIM // 2, 1))
    return x * cos + rot * sin


N_MOD = 9


def _adaln_kernel(c_ref, w_ref, b_ref, o_ref):
    c = c_ref[...]
    s = (c * jax.nn.sigmoid(c)).astype(BF16)
    o_ref[...] = _dot(s, w_ref[...].astype(BF16)) + b_ref[...]


def _adaln(cvec, w_ada, b_ada):
    return pl.pallas_call(
        _adaln_kernel,
        grid=(DEPTH, N_MOD),
        in_specs=[pl.BlockSpec((8, D_MODEL), lambda l, j: (0, 0)),
                  pl.BlockSpec((None, D_MODEL, D_MODEL), lambda l, j: (l, 0, j)),
                  pl.BlockSpec((None, 1, D_MODEL), lambda l, j: (l, 0, j))],
        out_specs=pl.BlockSpec((None, 8, D_MODEL), lambda l, j: (l, 0, j)),
        out_shape=jax.ShapeDtypeStruct((DEPTH, 8, 9 * D_MODEL), F32),
        name="adaln",
    )(cvec, w_ada, b_ada.reshape(DEPTH, 1, 9 * D_MODEL))


def _ffn_kernel(*refs, layer, split_in, split_out, emit_h, staged):
    refs = list(refs)
    x_refs = [refs.pop(0) for _ in range(4 if split_in else 2)]
    mod_ref = refs.pop(0)
    mod_next_ref = refs.pop(0) if emit_h else None
    w_in_refs = [refs.pop(0) for _ in range(3)]
    g_ref, b_ref = refs.pop(0), refs.pop(0)
    o_refs = [refs.pop(0) for _ in range(2 if split_out else 1)]
    hn_ref = refs.pop(0) if emit_h else None
    if staged:
        w1_hbm, w3_hbm, w2_hbm = w_in_refs
        h_scr, y_scr, w1_ref, w3_ref, w2_ref, stage_up, stage_down, sem = refs
    else:
        w1_ref, w3_ref, w2_ref = w_in_refs
        h_scr, y_scr = refs
    s = pl.program_id(0)
    cur = s % 2
    nxt = 1 - cur
    pieces = [pl.ds(r, SIDE_ROWS) for r in range(0, TM, SIDE_ROWS)]

    def x_tile(which, tile):
        if not split_in:
            return lambda rows: x_refs[which][rows, :]
        ctx_ref, lat_ref = x_refs[2 * which:2 * which + 2]
        return lambda rows: jnp.where(tile < P_TILES, ctx_ref[rows, :], lat_ref[rows, :])

    x_prev = x_tile(0, s - 1)
    x_next = x_tile(1, s + 1)

    def modulate(x_rows, tile, slot):
        shift, scale, _ = _mod_rows(mod_ref, tile)

        def piece(rows):
            h_scr[slot, rows, :] = (_ln(x_rows(rows), 1e-6) * (1.0 + scale) + shift).astype(BF16)
        return [functools.partial(piece, rows) for rows in pieces]

    def finish():
        _, _, gate = _mod_rows(mod_ref, s - 1)
        if emit_h:
            shift_n, scale_n, _ = _mod_rows(mod_next_ref, s - 1)

        def piece(rows):
            z = ALPHA * x_prev(rows) + (0.5 * gate) * y_scr[nxt, rows, :]
            out = _ln(z, 1e-5) * g_ref[...] + b_ref[...]
            if split_out:
                to_ctx = jnp.broadcast_to(s - 1 < P_TILES, out.shape)
                pltpu.store(o_refs[0].at[rows, :], out, mask=to_ctx)
                pltpu.store(o_refs[1].at[rows, :], out, mask=jnp.logical_not(to_ctx))
            else:
                o_refs[0][rows, :] = out
            if emit_h:
                hn_ref[rows, :] = (_ln(out, 1e-6) * (1.0 + scale_n) + shift_n).astype(BF16)
            y_scr[nxt, rows, :] = out
        return [functools.partial(piece, rows) for rows in pieces]

    def load_weights():
        chunks = []
        for src, dst in ((w1_hbm, w1_ref), (w3_hbm, w3_ref)):
            for r in range(0, D_MODEL, UP_STAGE_ROWS):
                rows = pl.ds(r, UP_STAGE_ROWS)
                chunks.append((src.at[layer, rows, :], stage_up, dst.at[rows, :]))
        for r in range(0, D_FF, DOWN_STAGE_ROWS):
            rows = pl.ds(r, DOWN_STAGE_ROWS)
            chunks.append((w2_hbm.at[layer, rows, :], stage_down, w2_ref.at[rows, :]))

        def copy(k):
            src, stage, _ = chunks[k]
            return pltpu.make_async_copy(src, stage.at[k % STAGE_SLOTS], sem.at[k % STAGE_SLOTS])

        for k in range(STAGE_SLOTS - 1):
            copy(k).start()
        for k, (_, stage, dst) in enumerate(chunks):
            if k + STAGE_SLOTS - 1 < len(chunks):
                copy(k + STAGE_SLOTS - 1).start()
            copy(k).wait()
            dst[...] = stage[k % STAGE_SLOTS].astype(BF16)

    def matmuls(side_work):
        side_work = list(side_work)
        chunks = [slice(c0, c1) for c0, c1 in zip(FF_CHUNKS[:-1], FF_CHUNKS[1:])]
        per_dot = -(-len(side_work) // (3 * len(chunks)))

        def dot_with_side(lhs, rhs):
            for _ in range(min(per_dot, len(side_work))):
                side_work.pop(0)()
            return _dot(lhs() if callable(lhs) else lhs, rhs)

        h = lambda: h_scr[cur]

        def gate_up(sl):
            a = dot_with_side(h, w1_ref[:, sl])
            return (a * jax.nn.sigmoid(a) * dot_with_side(h, w3_ref[:, sl])).astype(BF16)

        t_next = gate_up(chunks[0])
        for n, sl in enumerate(chunks):
            t = t_next
            if n + 1 < len(chunks):
                t_next = gate_up(chunks[n + 1])
            yc = dot_with_side(t, w2_ref[sl, :])
            if n == 0:
                y_scr[cur] = yc
            else:
                y_scr[cur] += yc
        for work in side_work:
            work()

    @pl.when(s == 0)
    def _():
        if staged:
            load_weights()
        for work in modulate(x_tile(0, s), s, cur):
            work()
        matmuls(modulate(x_next, s + 1, nxt))

    @pl.when(jnp.logical_and(s > 0, s < N_TILES))
    def _():
        matmuls(finish() + modulate(x_next, s + 1, nxt))

    @pl.when(s == N_TILES)
    def _():
        for work in finish():
            work()


def _resident(shape, index_map):
    return pl.BlockSpec(shape, index_map, pipeline_mode=pl.Buffered(1))


def _ffn(x, mod, w1, w3, w2, ln_g, ln_b, l, sub, split_out=False, emit_h=False):
    split_in = isinstance(x, tuple)
    tile = lambda lo, hi, off: pl.BlockSpec((TM, D_MODEL), lambda i: (jnp.clip(i + off, lo, hi) - lo, 0))
    ctx_lat = lambda off: [tile(0, P_TILES - 1, off), tile(P_TILES, N_TILES - 1, off)]
    if split_in:
        x_args = [x[0], x[1]] * 2
        x_specs = ctx_lat(-1) + ctx_lat(1)
    else:
        x_args = [x, x]
        x_specs = [tile(0, N_TILES - 1, -1), tile(0, N_TILES - 1, 1)]
    if split_out:
        out_specs = ctx_lat(-1)
        out_shape = [jax.ShapeDtypeStruct((P_TOK, D_MODEL), F32), jax.ShapeDtypeStruct((S_TOK, D_MODEL), F32)]
    else:
        out_specs = [tile(0, N_TILES - 1, -1)]
        out_shape = [jax.ShapeDtypeStruct((TOKENS, D_MODEL), F32)]
    mod_specs = [_resident((None, 8, 3 * D_MODEL), lambda i: (l, 0, sub))]
    mod_args = [mod]
    if emit_h:
        mod_specs.append(_resident((None, 8, 3 * D_MODEL), lambda i: (l, 0, sub + 1)))
        mod_args.append(mod)
        out_specs = out_specs + [tile(0, N_TILES - 1, -1)]
        out_shape = out_shape + [jax.ShapeDtypeStruct((TOKENS, D_MODEL), BF16)]
    staged = w1.dtype == F32
    scratch = [pltpu.VMEM((2, TM, D_MODEL), BF16), pltpu.VMEM((2, TM, D_MODEL), F32)]
    if staged:
        w_specs = [pl.BlockSpec(memory_space=pl.ANY)] * 3
        scratch += [pltpu.VMEM((D_MODEL, D_FF), BF16), pltpu.VMEM((D_MODEL, D_FF), BF16),
                    pltpu.VMEM((D_FF, D_MODEL), BF16),
                    pltpu.VMEM((STAGE_SLOTS, UP_STAGE_ROWS, D_FF), F32),
                    pltpu.VMEM((STAGE_SLOTS, DOWN_STAGE_ROWS, D_MODEL), F32),
                    pltpu.SemaphoreType.DMA((STAGE_SLOTS,))]
    else:
        w_specs = [_resident(w.shape, lambda i: (0, 0)) for w in (w1, w3, w2)]
    return pl.pallas_call(
        functools.partial(_ffn_kernel, layer=l, split_in=split_in, split_out=split_out, emit_h=emit_h,
                          staged=staged),
        grid=(N_TILES + 1,),
        in_specs=x_specs + mod_specs + w_specs + [
                  _resident((None, 1, D_MODEL), lambda i: (3 * l + sub, 0, 0)),
                  _resident((None, 1, D_MODEL), lambda i: (3 * l + sub, 0, 0))],
        out_specs=out_specs,
        out_shape=out_shape,
        scratch_shapes=scratch,
        compiler_params=pltpu.CompilerParams(dimension_semantics=("arbitrary",),
                                             vmem_limit_bytes=FFN_VMEM_LIMIT),
        name="ffn",
    )(*x_args, *mod_args, w1, w3, w2, ln_g, ln_b)


CAST_BLOCKS = N_TILES // 3


def _cast_ahead_specs(layer):
    up_rows, down_rows = D_MODEL // CAST_BLOCKS, D_FF // CAST_BLOCKS
    blk = lambda n: (lambda i: jnp.clip(i - n * CAST_BLOCKS, 0, CAST_BLOCKS - 1))
    shapes = [(up_rows, D_FF, D_MODEL), (up_rows, D_FF, D_MODEL), (down_rows, D_MODEL, D_FF)]
    in_specs = [pl.BlockSpec((None, r, c), lambda i, b=blk(n): (layer, b(i), 0)) for n, (r, c, _) in enumerate(shapes)]
    out_specs = [pl.BlockSpec((r, c), lambda i, b=blk(n): (b(i), 0)) for n, (r, c, _) in enumerate(shapes)]
    out_shape = [jax.ShapeDtypeStruct((total, c), BF16) for r, c, total in shapes]
    return in_specs, out_specs, out_shape


def _cast_ahead(src_refs, dst_refs):
    for n, (src, dst) in enumerate(zip(src_refs, dst_refs)):
        @pl.when(pl.program_id(0) // CAST_BLOCKS == n)
        def _(src=src, dst=dst):
            dst[...] = src[...].astype(BF16)


def _mixer_in_kernel(*refs, cast_ahead):
    refs = list(refs)
    h_ref, cos_ref, sin_ref, w_in_ref, qn_ref, w_uq_ref, kvn_ref, w_uk_ref, w_uv_ref = refs[:9]
    n_cast = 3 if cast_ahead else 0
    outs = refs[13 + n_cast:]
    uf_ref, q_ref, k_ref, v_ref, qna_ref, knab_ref, vnab_ref, ckv_ref, kr_ref, kna_ref, vna_ref = outs[:11]
    _cast_ahead(refs[13:13 + n_cast], outs[11:])
    i = pl.program_id(0)
    h = h_ref[...]
    cos = cos_ref[...]
    sin = sin_ref[...]

    proj = lambda r0, r1: _dot_nt(h, w_in_ref[r0:r1, :])
    uf_ref[...] = proj(0, U_Q0)

    u_q = proj(U_Q0, U_KV0)
    q = _dot(_rms(u_q, qn_ref[...]).astype(BF16), w_uq_ref[...])
    for hd in range(MLA_HEADS):
        sl = slice(hd * MLA_HP, (hd + 1) * MLA_HP)
        q_ref[:, sl] = _rope(q[:, sl], cos, sin).astype(BF16)

    c_kv = _rms(proj(U_KV0, U_KR0), kvn_ref[...])
    w_kr = jnp.concatenate([jnp.zeros((ROPE_LANE0, D_MODEL), BF16), w_in_ref[U_KR0:U_NA0, :],
                            jnp.zeros((LANE - ROPE_LANE0 - MLA_ROPE, D_MODEL), BF16)], axis=0)
    kr = _dot_nt(h, w_kr)
    k_na = proj(U_NA0 + NA_W, U_NA0 + 2 * NA_W)
    v_na = proj(U_NA0 + 2 * NA_W, U_W)

    @pl.when(i < P_TILES)
    def _():
        kr_t = kr.T[ROPE_LANE0:ROPE_LANE0 + MLA_ROPE, :]
        kna_t = k_na.T
        vna_t = v_na.T
        for b in range(TM // SEQ):
            rows = slice(b * SEQ, (b + 1) * SEQ)
            ckv_ref[b] = c_kv[rows, :]
            kr_ref[b] = kr_t[:, rows]
            kna_ref[b] = kna_t[:, rows]
            vna_ref[b] = vna_t[:, rows]

    c_kv = c_kv.astype(BF16)
    kr = _rope(kr, cos, sin)
    k = _dot(c_kv, w_uk_ref[...])
    for hd in range(MLA_HEADS):
        sl = slice(hd * MLA_HP, (hd + 1) * MLA_HP)
        k_ref[:, sl] = (k[:, sl] + kr).astype(BF16)
    v_ref[...] = _dot(c_kv, w_uv_ref[...]).astype(BF16)

    qna_ref[...] = proj(U_NA0, U_NA0 + NA_W).astype(BF16)
    knab_ref[...] = k_na.astype(BF16)
    vnab_ref[...] = v_na.astype(BF16)


def _cache_shapes():
    return [(BATCH, DEPTH, SEQ, MLA_KV_LORA), (BATCH, DEPTH, MLA_ROPE, SEQ),
            (BATCH, DEPTH, NA_W, SEQ), (BATCH, DEPTH, NA_W, SEQ)]


def _mixer_in(h, cos_t, sin_t, w_in, q_norm, w_uq, kv_norm, w_uk, w_uv, caches, l, next_ffn=None):
    tile = lambda w: pl.BlockSpec((TM, w), lambda i: (i, 0))
    acts = [(F_W, F32), (MLA_QW, BF16), (MLA_QW, BF16), (MLA_HEADS * MLA_V, BF16), (NA_W, BF16), (NA_W, BF16),
            (NA_W, BF16)]
    cache_spec = lambda s: pl.BlockSpec((TM // SEQ, None) + s[2:],
                                        lambda i: (jnp.minimum(i, P_TILES - 1), l, 0, 0))
    n_in = 9
    cast_in, cast_out, cast_shape = _cast_ahead_specs(next_ffn[3]) if next_ffn else ([], [], [])
    cast_args = list(next_ffn[:3]) if next_ffn else []
    return pl.pallas_call(
        functools.partial(_mixer_in_kernel, cast_ahead=bool(next_ffn)),
        grid=(N_TILES,),
        in_specs=[tile(D_MODEL),
                  pl.BlockSpec((TM, LANE), lambda i: (_rope_block(i), 0)),
                  pl.BlockSpec((TM, LANE), lambda i: (_rope_block(i), 0)),
                  _resident((None, U_W, D_MODEL), lambda i: (l, 0, 0)),
                  _resident((None, 1, MLA_Q_LORA), lambda i: (l, 0, 0)),
                  _resident((None, MLA_Q_LORA, MLA_QW), lambda i: (l, 0, 0)),
                  _resident((None, 1, MLA_KV_LORA), lambda i: (l, 0, 0)),
                  _resident((None, MLA_KV_LORA, MLA_QW), lambda i: (l, 0, 0)),
                  _resident((None, MLA_KV_LORA, MLA_HEADS * MLA_V), lambda i: (l, 0, 0))]
                 + [pl.BlockSpec(memory_space=pl.ANY)] * len(caches) + cast_in,
        out_specs=[tile(w) for w, _ in acts] + [cache_spec(s) for s in _cache_shapes()] + cast_out,
        out_shape=[jax.ShapeDtypeStruct((TOKENS, w), dt) for w, dt in acts]
                  + [jax.ShapeDtypeStruct(s, F32) for s in _cache_shapes()] + cast_shape,
        input_output_aliases={n_in + n: len(acts) + n for n in range(len(caches))},
        compiler_params=pltpu.CompilerParams(dimension_semantics=("arbitrary",),
                                             vmem_limit_bytes=VMEM_LIMIT),
        name="mixer_in",
    )(h, cos_t, sin_t, w_in, q_norm, w_uq, kv_norm, w_uk, w_uv, *caches, *cast_args)


def _ctx_kv_kernel(ckv_ref, kr_ref, w_uk_ref, w_uv_ref, k_ref, v_ref):
    c = ckv_ref[...].astype(BF16)
    k = _dot(c, w_uk_ref[...])
    kr = kr_ref[...]
    for hd in range(MLA_HEADS):
        sl = slice(hd * MLA_HP, (hd + 1) * MLA_HP)
        k_ref[:, sl] = (k[:, sl] + kr).astype(BF16)
    v_ref[...] = _dot(c, w_uv_ref[...]).astype(BF16)


def _ctx_kv(cache_ckv, cache_kr_pad, w_uk, w_uv):
    return pl.pallas_call(
        _ctx_kv_kernel,
        grid=(DEC_BATCH, DEPTH),
        in_specs=[pl.BlockSpec((None, None, PAST_LEN, MLA_KV_LORA), lambda b, l: (b, l, 0, 0)),
                  pl.BlockSpec((None, None, PAST_LEN, LANE), lambda b, l: (b, l, 0, 0)),
                  pl.BlockSpec((None, MLA_KV_LORA, MLA_QW), lambda b, l: (l, 0, 0)),
                  pl.BlockSpec((None, MLA_KV_LORA, MLA_HEADS * MLA_V), lambda b, l: (l, 0, 0))],
        out_specs=[pl.BlockSpec((None, None, PAST_LEN, MLA_QW), lambda b, l: (b, l, 0, 0)),
                   pl.BlockSpec((None, None, PAST_LEN, MLA_HEADS * MLA_V), lambda b, l: (b, l, 0, 0))],
        out_shape=[jax.ShapeDtypeStruct((DEC_BATCH, DEPTH, PAST_LEN, MLA_QW), BF16),
                   jax.ShapeDtypeStruct((DEC_BATCH, DEPTH, PAST_LEN, MLA_HEADS * MLA_V), BF16)],
        name="ctx_kv",
    )(cache_ckv, cache_kr_pad, w_uk, w_uv)


def _fourier_kernel(x_ref, cs_ref, cl_ref, sl_ref, o_ref):
    length = cl_ref.shape[0]
    x = x_ref[...].astype(BF16)
    xc, xs = [], []
    for g in range(F_GROUPS):
        t = _dot(x[:, g * F_GC:(g + 1) * F_GC], cs_ref[...])
        xc.append(t[:, :F_GC])
        xs.append(t[:, F_GC:])
    xc = jnp.concatenate(xc, axis=1).astype(BF16)
    xs = jnp.concatenate(xs, axis=1).astype(BF16)
    for r in range(0, x_ref.shape[0], length):
        rows = slice(r, r + length)
        o_ref[rows, :] = (_dot(cl_ref[...], xc[rows]) - _dot(sl_ref[...], xs[rows])).astype(BF16)


def _dft_tables(n):
    k = np.arange(n, dtype=np.int64)
    ang = 2.0 * np.pi * ((k[:, None] * k[None, :]) % n).astype(np.float64) / n
    s = n ** -0.5
    return np.cos(ang) * s, np.sin(ang) * s


def _fourier_tables(length):
    cc, sc = _dft_tables(F_GC)
    cl, sl = _dft_tables(length)
    as_bf16 = lambda a: jnp.asarray(a, F32).astype(BF16)
    return as_bf16(np.concatenate([cc, sc], axis=1)), as_bf16(cl), as_bf16(sl)


def _fourier(u_f, tables, length, first_block, n_blocks):
    cs, cl, sl = tables
    rows = max(length, FOURIER_ROWS)
    assert rows % length == 0 and (n_blocks * length) % rows == 0 and (first_block * length) % rows == 0
    first = first_block * length // rows
    return pl.pallas_call(
        _fourier_kernel,
        grid=(n_blocks * length // rows,),
        in_specs=[pl.BlockSpec((rows, F_W), lambda b: (first + b, 0)),
                  _resident((F_GC, 2 * F_GC), lambda b: (0, 0)),
                  _resident((length, length), lambda b: (0, 0)),
                  _resident((length, length), lambda b: (0, 0))],
        out_specs=pl.BlockSpec((rows, F_W), lambda b: (b, 0)),
        out_shape=jax.ShapeDtypeStruct((n_blocks * length, F_W), BF16),
        compiler_params=pltpu.CompilerParams(dimension_semantics=("arbitrary",),
                                             vmem_limit_bytes=VMEM_LIMIT),
        name="fourier_%d" % length,
    )(u_f, cs, cl, sl)


LOG2E = 1.4426950408889634


def _softmax_pv(scores, values_ext, scale):
    m = None
    for s in scores:
        sm = jnp.max(s, axis=-1, keepdims=True)
        m = sm if m is None else jnp.maximum(m, sm)
    acc = None
    for s, v in zip(scores, values_ext):
        p = jnp.exp2((s - m) * (scale * LOG2E)).astype(BF16)
        pv = _dot_nt(p, v.t) if isinstance(v, _KeyMinor) else _dot(p, v)
        acc = pv if acc is None else acc + pv
    return acc[:, :LANE] / acc[:, LANE:]


class _KeyMinor:
    def __init__(self, t):
        self.t = t


def _with_ones(v_pair):
    return jnp.concatenate([v_pair, jnp.ones_like(v_pair)], axis=1)


def _half_masks(rows):
    low = lax.broadcasted_iota(jnp.int32, (rows, LANE), 1) < LANE // 2
    return low, jnp.logical_not(low)


def _head_of_pair(x_pair, mask):
    return jnp.where(mask, x_pair, jnp.zeros_like(x_pair))


def _ctx_attn_kernel(q_ref, k_ref, v_ref, qn_ref, kn_ref, vn_ref, om_ref, on_ref):
    masks = _half_masks(SEQ)
    for r in range(0, CTX_ATTN_SEQS * SEQ, SEQ):
        rows = slice(r, r + SEQ)
        for pair in range(MLA_HEADS // 2):
            ps = slice(pair * LANE, (pair + 1) * LANE)
            v_ext = _with_ones(v_ref[rows, ps])
            o = []
            for half in range(2):
                sl = slice((2 * pair + half) * MLA_HP, (2 * pair + half + 1) * MLA_HP)
                s = _dot_nt(q_ref[rows, sl], k_ref[rows, sl])
                o.append(_softmax_pv([s], [v_ext], MLA_SCALE))
            om_ref[rows, ps] = jnp.where(masks[0], o[0], o[1]).astype(BF16)
        for pair in range(NA_HEADS // 2):
            ps = slice(pair * LANE, (pair + 1) * LANE)
            v_ext = _with_ones(vn_ref[rows, ps])
            q_pair = qn_ref[rows, ps]
            k_pair = kn_ref[rows, ps]
            o = []
            for half in range(2):
                s = _dot_nt(_head_of_pair(q_pair, masks[half]), k_pair)
                o.append(_softmax_pv([s], [v_ext], NA_SCALE))
            on_ref[rows, ps] = jnp.where(masks[0], o[0], o[1]).astype(BF16)


def _ctx_attn(q, k, v, qn, kn, vn):
    blk = lambda w: pl.BlockSpec((CTX_ATTN_SEQS * SEQ, w), lambda b: (b, 0))
    return pl.pallas_call(
        _ctx_attn_kernel,
        grid=(BATCH // CTX_ATTN_SEQS,),
        in_specs=[blk(MLA_QW), blk(MLA_QW), blk(MLA_HEADS * MLA_V), blk(NA_W), blk(NA_W), blk(NA_W)],
        out_specs=[blk(MLA_HEADS * MLA_V), blk(NA_W)],
        out_shape=[jax.ShapeDtypeStruct((P_TOK, MLA_HEADS * MLA_V), BF16),
                   jax.ShapeDtypeStruct((P_TOK, NA_W), BF16)],
        compiler_params=pltpu.CompilerParams(dimension_semantics=("arbitrary",)),
        name="ctx_attn",
    )(q, k, v, qn, kn, vn)


def _lat_mla_kernel(q_ref, k_ref, v_ref, kc_ref, vc_ref, *refs):
    n_w = (len(refs) - 1) // 2
    o_ref = refs[n_w]
    for src, dst in zip(refs[:n_w], refs[n_w + 1:]):
        dst[...] = src[...].astype(BF16)
    low, _ = _half_masks(MLA_QB)
    for pair in range(MLA_HEADS // 2):
        ps = slice(pair * LANE, (pair + 1) * LANE)
        v_ext = [_with_ones(v_ref[:, ps]), _with_ones(vc_ref[:, ps])]
        o = []
        for half in range(2):
            sl = slice((2 * pair + half) * MLA_HP, (2 * pair + half + 1) * MLA_HP)
            q = q_ref[:, sl]
            o.append(_softmax_pv([_dot_nt(q, k_ref[:, sl]), _dot_nt(q, kc_ref[:, sl])], v_ext, MLA_SCALE))
        o_ref[:, ps] = jnp.where(low, o[0], o[1]).astype(BF16)


def _lat_mla(q, k, v, kc, vc, merge_w, l):
    nq = DEC_SEQ // MLA_QB
    q0 = P_TOK // MLA_QB
    r0 = P_TOK // DEC_SEQ
    n_steps = DEC_BATCH * nq
    w_in_specs = [pl.BlockSpec((None, w.shape[1] // n_steps, w.shape[2]), lambda b, j: (l, b * nq + j, 0))
                  for w in merge_w]
    w_out_specs = [pl.BlockSpec((w.shape[1] // n_steps, w.shape[2]), lambda b, j: (b * nq + j, 0)) for w in merge_w]
    return pl.pallas_call(
        _lat_mla_kernel,
        grid=(DEC_BATCH, nq),
        in_specs=[pl.BlockSpec((MLA_QB, MLA_QW), lambda b, j: (q0 + b * nq + j, 0)),
                  pl.BlockSpec((DEC_SEQ, MLA_QW), lambda b, j: (r0 + b, 0)),
                  pl.BlockSpec((DEC_SEQ, MLA_HEADS * MLA_V), lambda b, j: (r0 + b, 0)),
                  pl.BlockSpec((None, None, PAST_LEN, MLA_QW), lambda b, j: (b, l, 0, 0)),
                  pl.BlockSpec((None, None, PAST_LEN, MLA_HEADS * MLA_V), lambda b, j: (b, l, 0, 0))] + w_in_specs,
        out_specs=[pl.BlockSpec((MLA_QB, MLA_HEADS * MLA_V), lambda b, j: (b * nq + j, 0))] + w_out_specs,
        out_shape=[jax.ShapeDtypeStruct((S_TOK, MLA_HEADS * MLA_V), BF16)]
                  + [jax.ShapeDtypeStruct(w.shape[1:], BF16) for w in merge_w],
        compiler_params=pltpu.CompilerParams(dimension_semantics=("arbitrary", "arbitrary"),
                                             vmem_limit_bytes=VMEM_LIMIT),
        name="lat_mla",
    )(q, k, v, kc, vc, *merge_w)


def _na_window_block(j):
    return jnp.clip(j - 1, 0, NA_BLOCKS - NA_WIN_ROWS // NA_QROWS)


def _lat_na_kernel(q_ref, k_ref, v_ref, kc_ref, vc_ref, bias_ref, o_ref):
    j = pl.program_id(1)
    start = pl.multiple_of(_na_window_block(j) * NA_QB, NA_QB)
    k_win = k_ref[pl.ds(start, NA_WIN), :]
    v_win = v_ref[pl.ds(start, NA_WIN), :]
    masks = _half_masks(NA_QB)
    for pair in range(NA_HEADS // 2):
        ps = slice(pair * LANE, (pair + 1) * LANE)
        k_loc = k_win[:, ps]
        k_ctx_t = kc_ref[ps, :].astype(BF16)
        v_ctx_t = vc_ref[ps, :].astype(BF16)
        v_ext = [_with_ones(v_win[:, ps]), _KeyMinor(jnp.concatenate([v_ctx_t, jnp.ones_like(v_ctx_t)], axis=0))]
        q_pair = q_ref[:, ps]
        o = []
        for half in range(2):
            q = _head_of_pair(q_pair, masks[half])
            s_loc = _dot_nt(q, k_loc) + bias_ref[2 * pair + half]
            o.append(_softmax_pv([s_loc, _dot(q, k_ctx_t)], v_ext, NA_SCALE))
        o_ref[:, ps] = jnp.where(masks[0], o[0], o[1]).astype(BF16)


NA_BLOCK_KINDS = (0, 1, NA_BLOCKS - 1)
NA_DIAG = GRID_W - 1
NA_N_DR = 2 * NA_KH - 1


def _na_bias_kernel(e_ref, o_ref):
    lane = lax.broadcasted_iota(jnp.int32, (GRID_W, LANE), 1)
    q_col = lax.broadcasted_iota(jnp.int32, (GRID_W, LANE), 0)
    k_col = lane % GRID_W
    col_start = jnp.clip(q_col - NA_KW // 2, 0, GRID_W - NA_KW)
    col_ok = (k_col >= col_start) & (k_col < col_start + NA_KW)
    low_half = lane < GRID_W
    neg = jnp.full((GRID_W, LANE), NEG_INF, F32)
    tiles = {}

    def toeplitz(dr, half):
        if (dr, half) not in tiles:
            row = jnp.broadcast_to(e_ref[dr:dr + 1, :], (GRID_W, LANE))
            shift = (LANE - NA_DIAG + half * GRID_W) % LANE
            tiles[dr, half] = pltpu.roll(row, shift, 1, stride=1, stride_axis=0)
        return tiles[dr, half]

    for kind, blk in enumerate(NA_BLOCK_KINDS):
        first_key_row = NA_QROWS * min(max(blk - 1, 0), NA_BLOCKS - NA_WIN_ROWS // NA_QROWS)
        for qr in range(NA_QROWS):
            q_row = NA_QROWS * blk + qr
            row_start = min(max(q_row - NA_KH // 2, 0), GRID_H - NA_KH)
            for pair in range(NA_WIN_ROWS // 2):
                halves = []
                for half in range(2):
                    k_row = first_key_row + 2 * pair + half
                    inside = row_start <= k_row < row_start + NA_KH
                    halves.append(toeplitz(k_row - q_row + NA_KH - 1, half) if inside else neg)
                tile = jnp.where(low_half, halves[0], halves[1])
                o_ref[kind, qr * GRID_W:(qr + 1) * GRID_W, pair * LANE:(pair + 1) * LANE] = (
                    jnp.where(col_ok, tile, NEG_INF))


def _na_bias_table(rpb):
    left = NA_DIAG - (NA_KW - 1)
    e = jnp.concatenate([jnp.broadcast_to(rpb[..., :1], rpb.shape[:-1] + (left,)), rpb,
                         jnp.broadcast_to(rpb[..., -1:], rpb.shape[:-1] + (LANE - left - rpb.shape[-1],))],
                        axis=-1) * (1.0 / NA_SCALE)
    return pl.pallas_call(
        _na_bias_kernel,
        grid=(DEPTH, NA_HEADS),
        in_specs=[pl.BlockSpec((None, None, NA_N_DR, LANE), lambda l, h: (l, h, 0, 0))],
        out_specs=pl.BlockSpec((None, len(NA_BLOCK_KINDS), None, NA_QB, NA_WIN), lambda l, h: (l, 0, h, 0, 0)),
        out_shape=jax.ShapeDtypeStruct((DEPTH, len(NA_BLOCK_KINDS), NA_HEADS, NA_QB, NA_WIN), F32),
        name="na_bias",
    )(e)


def _lat_na(q, k, v, cache_k, cache_v, bias, l):
    q0 = P_TOK // NA_QB
    r0 = P_TOK // DEC_SEQ
    kind = lambda j: (j > 0).astype(jnp.int32) + (j == NA_BLOCKS - 1).astype(jnp.int32)
    return pl.pallas_call(
        _lat_na_kernel,
        grid=(DEC_BATCH, NA_BLOCKS),
        in_specs=[pl.BlockSpec((NA_QB, NA_W), lambda b, j: (q0 + b * NA_BLOCKS + j, 0)),
                  pl.BlockSpec((DEC_SEQ, NA_W), lambda b, j: (r0 + b, 0)),
                  pl.BlockSpec((DEC_SEQ, NA_W), lambda b, j: (r0 + b, 0)),
                  pl.BlockSpec((None, None, PAST_LEN, NA_W), lambda b, j: (b, l, 0, 0)),
                  pl.BlockSpec((None, None, PAST_LEN, NA_W), lambda b, j: (b, l, 0, 0)),
                  pl.BlockSpec((None, None, NA_HEADS, NA_QB, NA_WIN), lambda b, j: (l, kind(j), 0, 0, 0))],
        out_specs=pl.BlockSpec((NA_QB, NA_W), lambda b, j: (b * NA_BLOCKS + j, 0)),
        out_shape=jax.ShapeDtypeStruct((S_TOK, NA_W), BF16),
        compiler_params=pltpu.CompilerParams(dimension_semantics=("arbitrary", "arbitrary"),
                                             vmem_limit_bytes=VMEM_LIMIT),
        name="lat_na",
    )(q, k, v, cache_k, cache_v, bias)


def _merge_kernel(x_ref, h_ref, mod_ref, yfc_ref, ymc_ref, ync_ref, yfl_ref, yml_ref, ynl_ref,
                  wg_ref, bg_ref, wf_ref, wm_ref, wn_ref, wo_ref, g_ref, b_ref, w1_ref, w3_ref, w2_ref,
                  o_ref, w1b_ref, w3b_ref, w2b_ref):
    _cast_ahead((w1_ref, w3_ref, w2_ref), (w1b_ref, w3b_ref, w2b_ref))
    i = pl.program_id(0)
    _, _, gate = _mod_rows(mod_ref, i)
    x = x_ref[...]
    h = h_ref[...]
    is_ctx = i < P_TILES
    mix = None
    for n, (yc_ref, yl_ref, w_ref) in enumerate(((yfc_ref, yfl_ref, wf_ref), (ymc_ref, yml_ref, wm_ref),
                                                 (ync_ref, ynl_ref, wn_ref))):
        sl = slice(n * D_MODEL, (n + 1) * D_MODEL)
        g = jax.nn.sigmoid(_dot(h, wg_ref[:, sl]) + bg_ref[:, sl])
        y = jnp.where(is_ctx, yc_ref[...], yl_ref[...])
        t = g * _dot(y, w_ref[...])
        mix = t if mix is None else mix + t
    z = ALPHA * x + gate * _dot(mix.astype(BF16), wo_ref[...])
    o_ref[...] = _ln(z, 1e-5) * g_ref[...] + b_ref[...]


def _merge(x, h, mod, y_ctx, y_lat, w_gate, b_gate, w_f, w_m, w_n, w_out, ln_g, ln_b, ffn_w, l):
    cast_in, cast_out, cast_shape = _cast_ahead_specs(l)
    tile = lambda w: pl.BlockSpec((TM, w), lambda i: (i, 0))
    ctx_tile = lambda w: pl.BlockSpec((TM, w), lambda i: (jnp.minimum(i, P_TILES - 1), 0))
    lat_tile = lambda w: pl.BlockSpec((TM, w), lambda i: (jnp.maximum(i - P_TILES, 0), 0))
    widths = (F_W, MLA_HEADS * MLA_V, NA_W)
    return pl.pallas_call(
        _merge_kernel,
        grid=(N_TILES,),
        in_specs=[tile(D_MODEL), tile(D_MODEL),
                  _resident((None, 8, 3 * D_MODEL), lambda i: (l, 0, 1))]
                 + [ctx_tile(w) for w in widths] + [lat_tile(w) for w in widths] + [
                  _resident((D_MODEL, 3 * D_MODEL), lambda i: (0, 0)),
                  _resident((None, 1, 3 * D_MODEL), lambda i: (l, 0, 0)),
                  _resident((F_W, D_MODEL), lambda i: (0, 0)),
                  _resident((MLA_HEADS * MLA_V, D_MODEL), lambda i: (0, 0)),
                  _resident((NA_W, D_MODEL), lambda i: (0, 0)),
                  _resident((D_MODEL, D_MODEL), lambda i: (0, 0)),
                  _resident((None, 1, D_MODEL), lambda i: (3 * l + 1, 0, 0)),
                  _resident((None, 1, D_MODEL), lambda i: (3 * l + 1, 0, 0))] + cast_in,
        out_specs=[tile(D_MODEL)] + cast_out,
        out_shape=[jax.ShapeDtypeStruct((TOKENS, D_MODEL), F32)] + cast_shape,
        compiler_params=pltpu.CompilerParams(dimension_semantics=("arbitrary",),
                                             vmem_limit_bytes=VMEM_LIMIT),
        name="merge",
    )(x, h, mod, *y_ctx, *y_lat, w_gate, b_gate, w_f, w_m, w_n, w_out, ln_g, ln_b, *ffn_w)


def _pad_heads(w, n_heads, width):
    lead = w.shape[:-1]
    w = w.reshape(lead + (n_heads, width))
    w = jnp.pad(w, [(0, 0)] * len(lead) + [(0, 0), (0, MLA_HP - width)])
    return w.reshape(lead + (n_heads * MLA_HP,))


def _rope_tables():
    t = jnp.arange(DEC_SEQ, dtype=jnp.int32)
    pos = jnp.stack([t // GRID_W, t % GRID_W], axis=-1).astype(F32)
    half = AXIS_DIM // 2
    inv_freq = ROPE_BASE ** (-jnp.arange(half, dtype=F32) / half)
    ang = pos[:, :, None] * inv_freq
    ang = jnp.concatenate([ang, ang], axis=-1).reshape(DEC_SEQ, MLA_ROPE)
    pad = lambda a, fill: jnp.pad(a, ((0, 0), (ROPE_LANE0, LANE - ROPE_LANE0 - MLA_ROPE)), constant_values=fill)
    cos = jnp.concatenate([jnp.ones((TM, LANE), F32), pad(jnp.cos(ang), 1.0)], axis=0)
    sin = jnp.concatenate([jnp.zeros((TM, LANE), F32), pad(jnp.sin(ang), 0.0)], axis=0)
    return cos, sin


def kernel(x_prompt, x_sample, cache_mla_ckv, cache_mla_krope, cache_na_k, cache_na_v, c, c_ctx, w_ada, b_ada, ffn1_w1, ffn1_w3, ffn1_w2, ffn2_w1, ffn2_w3, ffn2_w2, w_in, mla_q_norm, mla_w_uq, mla_kv_norm, mla_w_ukv, na_rpb, w_branch_f, w_branch_m, w_branch_n, w_gate, b_gate, w_out, ln_g, ln_b):
    bf = lambda w: w.astype(BF16)
    f1 = (ffn1_w1, ffn1_w3, ffn1_w2)
    f2 = (ffn2_w1, ffn2_w3, ffn2_w2)
    w_in_p = bf(jnp.swapaxes(w_in, 1, 2))
    w_uq_p = bf(_pad_heads(mla_w_uq, MLA_HEADS, MLA_NOPE + MLA_ROPE))
    ukv = mla_w_ukv.reshape(DEPTH, MLA_KV_LORA, MLA_HEADS, MLA_NOPE + MLA_V)
    w_uk_p = bf(_pad_heads(ukv[..., :MLA_NOPE].reshape(DEPTH, MLA_KV_LORA, MLA_HEADS * MLA_NOPE), MLA_HEADS, MLA_NOPE))
    w_uv = bf(ukv[..., MLA_NOPE:].reshape(DEPTH, MLA_KV_LORA, MLA_HEADS * MLA_V))
    merge_w = (w_gate, w_branch_f, w_branch_m, w_branch_n, w_out)
    q_norm = mla_q_norm.reshape(DEPTH, 1, MLA_Q_LORA)
    kv_norm = mla_kv_norm.reshape(DEPTH, 1, MLA_KV_LORA)
    b_gate3 = b_gate.reshape(DEPTH, 1, 3 * D_MODEL)
    g3 = ln_g.reshape(DEPTH * 3, 1, D_MODEL)
    b3 = ln_b.reshape(DEPTH * 3, 1, D_MODEL)
    cos_t, sin_t = _rope_tables()
    dft_ctx = _fourier_tables(SEQ)
    dft_lat = _fourier_tables(DEC_SEQ)
    kr_pad = jnp.pad(cache_mla_krope, ((0, 0), (0, 0), (0, 0), (ROPE_LANE0, LANE - ROPE_LANE0 - MLA_ROPE)))
    ch_major = lambda a: jnp.transpose(a, (0, 1, 3, 4, 2)).reshape(DEC_BATCH, DEPTH, NA_W, PAST_LEN)
    cache_k = ch_major(cache_na_k)
    cache_v = ch_major(cache_na_v)

    cvec = jnp.concatenate([c_ctx[None], c, jnp.zeros((8 - 1 - DEC_BATCH, D_MODEL), F32)], axis=0)
    mod = _adaln(cvec, w_ada, b_ada)
    kc, vc = _ctx_kv(cache_mla_ckv, kr_pad, w_uk_p, w_uv)
    na_bias = _na_bias_table(na_rpb)

    x = (x_prompt.reshape(P_TOK, D_MODEL), x_sample.reshape(S_TOK, D_MODEL))
    caches = [jnp.zeros(s, F32) for s in _cache_shapes()]
    f1_l = f1
    for l in range(DEPTH):
        x, h = _ffn(x, mod, *f1_l, g3, b3, l, 0, emit_h=True)
        u_f, q, k, v, q_n, k_nb, v_nb, *rest = _mixer_in(
            h, cos_t, sin_t, w_in_p, q_norm, w_uq_p, kv_norm, w_uk_p, w_uv, caches, l,
            next_ffn=(*f1, l + 1) if l + 1 < DEPTH else None)
        caches, f1_l = rest[:4], rest[4:]
        yf_c = _fourier(u_f, dft_ctx, SEQ, 0, BATCH)
        yf_l = _fourier(u_f, dft_lat, DEC_SEQ, P_TOK // DEC_SEQ, DEC_BATCH)
        ym_c, yn_c = _ctx_attn(q, k, v, q_n, k_nb, v_nb)
        ym_l, w_gate_b, w_f, w_m, w_n, w_out_b = _lat_mla(q, k, v, kc, vc, merge_w, l)
        yn_l = _lat_na(q_n, k_nb, v_nb, cache_k, cache_v, na_bias, l)
        x, *f2_l = _merge(x, h, mod, (yf_c, ym_c, yn_c), (yf_l, ym_l, yn_l), w_gate_b, b_gate3, w_f, w_m, w_n,
                          w_out_b, g3, b3, f2, l)
        x = _ffn(x, mod, *f2_l, g3, b3, l, 2, split_out=(l == DEPTH - 1))
        x = tuple(x) if len(x) > 1 else x[0]
    ckv, kr_t, nak_t, nav_t = caches
    per_head = lambda a: jnp.transpose(a.reshape(BATCH, DEPTH, NA_HEADS, NA_HEAD_DIM, SEQ), (0, 1, 4, 2, 3))
    return (x[0].reshape(BATCH, SEQ, D_MODEL), x[1].reshape(DEC_BATCH, DEC_SEQ, D_MODEL),
            ckv, jnp.transpose(kr_t, (0, 1, 3, 2)), per_head(nak_t), per_head(nav_t))
```
